```python
import jax, jax.numpy as jnp
from jax import lax
import numpy as np

D_MODEL = 1024
BATCH = 2
SEQ = 8192
DEPTH = 4

GLA_HEADS = 4
GLA_DK = 48
GLA_DV = 96
GLA_RANK = 16
GLA_TAU = 16.0
HG_HEADS = 4
HG_DK = 64
HG_DV = 64
RET_HEADS = 4
RET_DK = 48
RET_DV = 96
GLA_QK = GLA_HEADS * GLA_DK
GLA_V = GLA_HEADS * GLA_DV
HG_K = HG_HEADS * HG_DK
HG_V = HG_HEADS * HG_DV
RET_QK = RET_HEADS * RET_DK
RET_V = RET_HEADS * RET_DV
D_MIX = GLA_V + HG_V + RET_V
D_IN = 2 * GLA_QK + 2 * GLA_V + GLA_RANK + 2 * HG_K + 2 * HG_V + 2 * RET_QK + 2 * RET_V
IN_WIDTHS = (GLA_QK, GLA_QK, GLA_V, GLA_V, GLA_RANK, HG_K, HG_K, HG_V, HG_V, RET_QK, RET_QK, RET_V, RET_V)
CHUNK = 64
ROPE_BASE = 10000.0
N_EXPERTS = 32
TOP_K = 4
D_FF = D_MODEL
SWIGLU_LIMIT = 7.0
SWIGLU_ALPHA = 1.702
MOE_BLOCK = 512
EPS = 1e-6

kernel_name = "hybrid_gla_hgrn2_retention_moe_adaln"


def rmsnorm(x, w):
    x32 = x.astype(jnp.float32)
    y = x32 * lax.rsqrt(jnp.mean(x32 * x32, axis=-1, keepdims=True) + EPS)
    return (y * w.astype(jnp.float32)).astype(x.dtype)


def split_heads(t, n_heads):
    b, l, hd = t.shape
    return t.reshape(b, l, n_heads, hd // n_heads).transpose(0, 2, 1, 3)


def merge_heads(t):
    b, h, l, d = t.shape
    return t.transpose(0, 2, 1, 3).reshape(b, l, h * d)


def head_rmsnorm(o, w):
    h, d = o.shape[1], o.shape[3]
    o32 = o.astype(jnp.float32)
    y = o32 * lax.rsqrt(jnp.mean(o32 * o32, axis=-1, keepdims=True) + EPS)
    return (y * w.astype(jnp.float32).reshape(h, 1, d)).astype(o.dtype)


def head_layernorm(o):
    o32 = o.astype(jnp.float32)
    mu = jnp.mean(o32, axis=-1, keepdims=True)
    var = jnp.mean(jnp.square(o32 - mu), axis=-1, keepdims=True)
    return ((o32 - mu) * lax.rsqrt(var + EPS)).astype(o.dtype)


def apply_rotary(t, cos, sin):
    half = t.shape[-1] // 2
    t1, t2 = t[..., :half], t[..., half:]
    return jnp.concatenate([t1 * cos - t2 * sin, t2 * cos + t1 * sin], axis=-1)


def chunked_gated_recurrence(q, k, v, log_a):
    out_dtype = v.dtype
    b, h, l, dk = q.shape
    dv = v.shape[-1]
    n = l // CHUNK

    def chunks(t):
        return t.astype(jnp.float32).reshape(b, h, n, CHUNK, t.shape[-1]).transpose(2, 0, 1, 3, 4)

    qc, kc, vc, gc = chunks(q), chunks(k), chunks(v), chunks(log_a)
    causal = jnp.tril(jnp.ones((CHUNK, CHUNK), dtype=bool))[:, :, None]

    def step(state, inp):
        qi, ki, vi, gi = inp
        cum = jnp.cumsum(gi, axis=-2)
        o_inter = jnp.einsum('bhcd,bhde->bhce', qi * jnp.exp(cum), state)
        diff = cum[:, :, :, None, :] - cum[:, :, None, :, :]
        decay = jnp.exp(jnp.where(causal, diff, -jnp.inf))
        scores = jnp.einsum('bhid,bhjd,bhijd->bhij', qi, ki, decay)
        o = o_inter + jnp.einsum('bhij,bhje->bhie', scores, vi)
        last = cum[:, :, -1:, :]
        new_state = jnp.exp(last[:, :, 0, :])[..., None] * state + jnp.einsum(
            'bhcd,bhce->bhde', ki * jnp.exp(last - cum), vi)
        return new_state, o

    s0 = jnp.zeros((b, h, dk, dv), jnp.float32)
    _, o = lax.scan(step, s0, (qc, kc, vc, gc))
    return o.transpose(1, 2, 0, 3, 4).reshape(b, h, l, dv).astype(out_dtype)


def chunked_retention(q, k, v):
    out_dtype = v.dtype
    b, h, l, dk = q.shape
    dv = v.shape[-1]
    n = l // CHUNK
    log_gamma = jnp.log(1.0 - jnp.exp2(-5.0 - jnp.arange(h, dtype=jnp.float32)))
    idx = jnp.arange(CHUNK, dtype=jnp.float32)
    rel = idx[:, None] - idx[None, :]
    dmat = jnp.where(rel >= 0, jnp.exp(log_gamma[:, None, None] * jnp.maximum(rel, 0.0)), 0.0)
    qc = q.astype(jnp.float32).reshape(b, h, n, CHUNK, dk)
    kc = k.astype(jnp.float32).reshape(b, h, n, CHUNK, dk)
    vc = v.astype(jnp.float32).reshape(b, h, n, CHUNK, dv)
    scores = jnp.einsum('bhnid,bhnjd->bhnij', qc, kc) * dmat[None, :, None]
    o_intra = jnp.einsum('bhnij,bhnje->bhnie', scores, vc)
    k_decay = jnp.exp(log_gamma[:, None] * (CHUNK - 1.0 - idx))
    kv = jnp.einsum('bhnjd,bhnje->bhnde', kc * k_decay[None, :, None, :, None], vc)
    chunk_decay = jnp.exp(log_gamma * CHUNK)[None, :, None, None]

    def step(state, kv_n):
        return chunk_decay * state + kv_n, state

    s0 = jnp.zeros((b, h, dk, dv), jnp.float32)
    _, s_prev = lax.scan(step, s0, kv.transpose(2, 0, 1, 3, 4))
    s_prev = s_prev.transpose(1, 2, 0, 3, 4)
    q_decay = jnp.exp(log_gamma[:, None] * (idx + 1.0))
    o_inter = jnp.einsum('bhnid,bhnde->bhnie', qc * q_decay[None, :, None, :, None], s_prev)
    return (o_intra + o_inter).reshape(b, h, l, dv).astype(out_dtype)


def hybrid_mixer(h, w_in, gla_wa2, gla_ba2, lb, gla_norm_w, hg_norm_w, w_out, cos, sin):
    proj = h @ w_in
    parts = []
    start = 0
    for wd in IN_WIDTHS:
        parts.append(proj[..., start:start + wd])
        start += wd
    gq, gk, gv, gg, ga, hq, hf, hi, hg, rq, rk, rv, rg = parts

    log_alpha = jax.nn.log_sigmoid((ga @ gla_wa2 + gla_ba2).astype(jnp.float32)) / GLA_TAU
    o_gla = chunked_gated_recurrence(split_heads(gq, GLA_HEADS) * (GLA_DK ** -0.5),
                                     split_heads(gk, GLA_HEADS),
                                     split_heads(gv, GLA_HEADS),
                                     split_heads(log_alpha, GLA_HEADS))
    o_gla = merge_heads(head_rmsnorm(o_gla, gla_norm_w)) * jax.nn.silu(gg)

    log_f = jnp.logaddexp(jnp.log(lb), jnp.log1p(-lb) + jax.nn.log_sigmoid(hf.astype(jnp.float32)))
    k_hg = -jnp.expm1(log_f)
    o_hg = chunked_gated_recurrence(split_heads(hq, HG_HEADS),
                                    split_heads(k_hg.astype(hq.dtype), HG_HEADS),
                                    split_heads(hi, HG_HEADS),
                                    split_heads(log_f, HG_HEADS))
    o_hg = merge_heads(head_rmsnorm(o_hg, hg_norm_w)) * jax.nn.sigmoid(hg)

    q_r = apply_rotary(split_heads(rq, RET_HEADS), cos, sin) * (RET_DK ** -0.5)
    k_r = apply_rotary(split_heads(rk, RET_HEADS), cos, sin)
    o_ret = chunked_retention(q_r, k_r, split_heads(rv, RET_HEADS))
    o_ret = merge_heads(head_layernorm(o_ret)) * jax.nn.silu(rg)

    return jnp.concatenate([o_gla, o_hg, o_ret], axis=-1) @ w_out


def moe_ffn(h, w_r, b_r, w_gu, b_gu, w_dn, b_dn):
    b, l, d = h.shape
    t = b * l
    xf = h.reshape(t, d)
    logits = (xf @ w_r + b_r).astype(jnp.float32)
    top_v, top_e = lax.top_k(logits, TOP_K)
    top_w = jax.nn.softmax(top_v, axis=-1)
    n_assign = t * TOP_K
    flat_e = top_e.reshape(-1)
    flat_t = jnp.repeat(jnp.arange(t, dtype=jnp.int32), TOP_K)
    flat_w = top_w.reshape(-1)
    order = jnp.argsort(flat_e)
    se, st, sw = flat_e[order], flat_t[order], flat_w[order]
    counts = jnp.bincount(flat_e, length=N_EXPERTS)
    padded = (counts + MOE_BLOCK - 1) // MOE_BLOCK * MOE_BLOCK
    starts = jnp.cumsum(counts) - counts
    pends = jnp.cumsum(padded)
    pstarts = pends - padded
    dest = pstarts[se] + jnp.arange(n_assign) - starts[se]
    n_blocks = (n_assign + MOE_BLOCK - 1) // MOE_BLOCK + N_EXPERTS
    p = n_blocks * MOE_BLOCK
    buf_t = jnp.zeros((p,), jnp.int32).at[dest].set(st)
    buf_w = jnp.zeros((p,), h.dtype).at[dest].set(sw.astype(h.dtype))
    block_start = jnp.arange(n_blocks) * MOE_BLOCK
    block_e = jnp.minimum(jnp.searchsorted(pends, block_start, side='right'), N_EXPERTS - 1)

    def expert_block(args):
        tok, e = args
        xb = xf[tok]
        gu = xb @ w_gu[e] + b_gu[e]
        gate = jnp.minimum(gu[:, :D_FF], SWIGLU_LIMIT)
        up = jnp.clip(gu[:, D_FF:], -SWIGLU_LIMIT, SWIGLU_LIMIT)
        act = (up + 1.0) * gate * jax.nn.sigmoid(SWIGLU_ALPHA * gate)
        return act @ w_dn[e] + b_dn[e]

    yb = lax.map(expert_block, (buf_t.reshape(n_blocks, MOE_BLOCK), block_e))
    y = jnp.zeros((t, d), h.dtype).at[buf_t].add(yb.reshape(p, d) * buf_w[:, None])
    return y.reshape(b, l, d)


def setup_inputs(seed: int = 0) -> dict:
    key = jax.random.key(seed)
    ks = jax.random.split(key, 24)
    f32 = jnp.float32
    nrm = lambda k, shape, s: jax.random.normal(k, shape, f32) * s
    x = nrm(ks[0], (BATCH, SEQ, D_MODEL), 1.0)
    c = nrm(ks[1], (BATCH, D_MODEL), 1.0)
    offset = jax.random.randint(ks[2], (BATCH, 1), 0, 1024, dtype=jnp.int32)
    positions = jnp.arange(SEQ, dtype=jnp.int32)[None, :] + offset
    return {
        "x": x,
        "c": c,
        "positions": positions,
        "w_mod": nrm(ks[3], (DEPTH, D_MODEL, 6 * D_MODEL), 0.5 * D_MODEL ** -0.5),
        "b_mod": nrm(ks[4], (DEPTH, 6 * D_MODEL), 0.02),
        "norm1_w": 1.0 + nrm(ks[5], (DEPTH, D_MODEL), 0.02),
        "w_in": nrm(ks[6], (DEPTH, D_MODEL, D_IN), D_MODEL ** -0.5),
        "gla_wa2": nrm(ks[7], (DEPTH, GLA_RANK, GLA_QK), GLA_RANK ** -0.5),
        "gla_ba2": nrm(ks[8], (DEPTH, GLA_QK), 0.1),
        "hg_lb": nrm(ks[9], (DEPTH, HG_K), 0.1),
        "gla_norm_w": 1.0 + nrm(ks[10], (DEPTH, GLA_V), 0.02),
        "hg_norm_w": 1.0 + nrm(ks[11], (DEPTH, HG_V), 0.02),
        "w_out": nrm(ks[12], (DEPTH, D_MIX, D_MODEL), D_MIX ** -0.5),
        "norm2_w": 1.0 + nrm(ks[13], (DEPTH, D_MODEL), 0.02),
        "w_r": nrm(ks[14], (DEPTH, D_MODEL, N_EXPERTS), D_MODEL ** -0.5),
        "b_r": nrm(ks[15], (DEPTH, N_EXPERTS), 0.01),
        "w_gu": nrm(ks[16], (DEPTH, N_EXPERTS, D_MODEL, 2 * D_FF), D_MODEL ** -0.5),
        "b_gu": nrm(ks[17], (DEPTH, N_EXPERTS, 2 * D_FF), 0.01),
        "w_dn": nrm(ks[18], (DEPTH, N_EXPERTS, D_FF, D_MODEL), D_FF ** -0.5),
        "b_dn": nrm(ks[19], (DEPTH, N_EXPERTS, D_MODEL), 0.01),
        "final_norm_w": 1.0 + nrm(ks[20], (D_MODEL,), 0.02),
    }


def reference(x, c, positions, w_mod, b_mod, norm1_w, w_in, gla_wa2, gla_ba2, hg_lb,
              gla_norm_w, hg_norm_w, w_out, norm2_w, w_r, b_r, w_gu, b_gu, w_dn, b_dn,
              final_norm_w):
    c_act = jax.nn.silu(c)
    half = RET_DK // 2
    inv_freq = ROPE_BASE ** (-jnp.arange(half, dtype=jnp.float32) / half)
    ang = positions.astype(jnp.float32)[..., None] * inv_freq
    cos = jnp.cos(ang)[:, None].astype(x.dtype)
    sin = jnp.sin(ang)[:, None].astype(x.dtype)
    lb_all = jnp.cumsum(jax.nn.softmax(hg_lb.astype(jnp.float32), axis=0), axis=0)
    lb_all = lb_all - lb_all[0:1]

    for layer in range(DEPTH):
        mod = (c_act @ w_mod[layer] + b_mod[layer])[:, None, :]
        sh1, sc1, g1, sh2, sc2, g2 = jnp.split(mod, 6, axis=-1)
        h = rmsnorm(x, norm1_w[layer]) * (1.0 + sc1) + sh1
        x = x + g1 * hybrid_mixer(h, w_in[layer], gla_wa2[layer], gla_ba2[layer], lb_all[layer],
                                  gla_norm_w[layer], hg_norm_w[layer], w_out[layer], cos, sin)
        h = rmsnorm(x, norm2_w[layer]) * (1.0 + sc2) + sh2
        x = x + g2 * moe_ffn(h, w_r[layer], b_r[layer], w_gu[layer], b_gu[layer],
                             w_dn[layer], b_dn[layer])
    return rmsnorm(x, final_norm_w)
```

```python
import functools

import numpy as np
import jax
import jax.numpy as jnp
from jax import lax
from jax.experimental import pallas as pl
from jax.experimental.pallas import tpu as pltpu

F32 = jnp.float32
BF16 = jnp.bfloat16

N_HEADS = 4
GLA_DK, GLA_DV, GLA_RANK, GLA_TAU = 48, 96, 16, 16.0
HG_DK, HG_DV = 64, 64
RET_DK, RET_DV = 48, 96
ROPE_BASE = 10000.0
N_EXPERTS, TOP_K = 32, 4
SWIGLU_LIMIT, SWIGLU_ALPHA = 7.0, 1.702
MOE_BLOCK = 512
EPS = 1e-6

LANE = 128
HEAD_W = N_HEADS * LANE
VMEM_LIMIT = 56 * 1024 * 1024
NEG_BIG = -1e30


def _cparams(sem):
    return pltpu.CompilerParams(dimension_semantics=sem, vmem_limit_bytes=VMEM_LIMIT)


def _split3(a):
    hi = a.astype(BF16)
    r = a - hi.astype(F32)
    mid = r.astype(BF16)
    lo = (r - mid.astype(F32)).astype(BF16)
    return hi, mid, lo


def _dot(a, b):
    return jnp.dot(a, b, preferred_element_type=F32)


def _dot_nt(a, b):
    return lax.dot_general(a, b, (((1,), (1,)), ((), ())), preferred_element_type=F32)


def _dot_tn(a, b):
    return lax.dot_general(a, b, (((0,), (0,)), ((), ())), preferred_element_type=F32)


def _dot_f32(a, b):
    a_hi = a.astype(BF16)
    a_lo = (a - a_hi.astype(F32)).astype(BF16)
    b_hi = b.astype(BF16)
    b_lo = (b - b_hi.astype(F32)).astype(BF16)
    return _dot(a_hi, b_hi) + _dot(a_hi, b_lo) + _dot(a_lo, b_hi)


def _sigmoid(x):
    return 1.0 / (1.0 + jnp.exp(-x))


def _log_sigmoid(x):
    return jnp.minimum(x, 0.0) - jnp.log1p(jnp.exp(-jnp.abs(x)))


def _mod_kernel(c_ref, w_ref, b_ref, o_ref):
    c = c_ref[...]
    o_ref[0] = _dot_f32(c * _sigmoid(c), w_ref[0]) + b_ref[0]


def _mod_call(c, w_mod, b_mod):
    depth, d, d6 = w_mod.shape
    nb = c.shape[0]
    rows = 8
    c_pad = jnp.zeros((rows, d), F32).at[:nb].set(c)
    out = pl.pallas_call(
        _mod_kernel,
        grid=(depth, d6 // d),
        in_specs=[
            pl.BlockSpec((rows, d), lambda l, j: (0, 0)),
            pl.BlockSpec((1, d, d), lambda l, j: (l, 0, j)),
            pl.BlockSpec((1, 1, d), lambda l, j: (l, 0, j)),
        ],
        out_specs=pl.BlockSpec((1, rows, d), lambda l, j: (l, 0, j)),
        out_shape=jax.ShapeDtypeStruct((depth, rows, d6), F32),
        compiler_params=_cparams(("parallel", "parallel")),
        name="mod",
    )(c_pad, w_mod, b_mod.reshape(depth, 1, d6))
    return out[:, :nb]


def _rope_kernel(pos_ref, freq_ref, cos_ref, sin_ref):
    ang = pos_ref[0] * freq_ref[...]
    cos_ref[0] = jnp.cos(ang)
    sin_ref[0] = jnp.sin(ang)


def _rope_call(positions):
    nb, l = positions.shape
    half = RET_DK // 2
    inv = (ROPE_BASE ** (-np.arange(half, dtype=np.float32) / half)).astype(np.float32)
    freq = np.zeros((1, LANE), np.float32)
    freq[0, :half] = inv
    freq[0, half:2 * half] = inv
    tl = min(l, 512)
    pos = positions.astype(F32).reshape(nb, l, 1)
    shp = jax.ShapeDtypeStruct((nb, l, LANE), F32)
    return pl.pallas_call(
        _rope_kernel,
        grid=(nb, l // tl),
        in_specs=[pl.BlockSpec((1, tl, 1), lambda b, i: (b, i, 0)),
                  pl.BlockSpec((1, LANE), lambda b, i: (0, 0))],
        out_specs=[pl.BlockSpec((1, tl, LANE), lambda b, i: (b, i, 0))] * 2,
        out_shape=[shp, shp],
        compiler_params=_cparams(("parallel", "parallel")),
        name="rope_tables",
    )(pos, jnp.asarray(freq))


def _lb_kernel(p_ref, o_ref):
    p = p_ref[...]
    depth = p.shape[0]
    m = jnp.max(p, axis=0, keepdims=True)
    e = jnp.exp(p - m)
    sm = e / jnp.sum(e, axis=0, keepdims=True)
    acc = jnp.zeros_like(sm[0:1])
    for i in range(depth):
        if i > 0:
            acc = acc + sm[i:i + 1]
        o_ref[i:i + 1, :] = acc


def _lb_call(hg_lb):
    return pl.pallas_call(
        _lb_kernel,
        out_shape=jax.ShapeDtypeStruct(hg_lb.shape, F32),
        name="hg_lower_bounds",
    )(hg_lb.astype(F32))


def _inproj_kernel(x_ref, nw_ref, sc_ref, sh_ref, w_ref, *out_refs, widths):
    x = x_ref[...]
    y = x * lax.rsqrt(jnp.mean(x * x, axis=-1, keepdims=True) + EPS) * nw_ref[...]
    hb = (y * (1.0 + sc_ref[0]) + sh_ref[0]).astype(BF16)
    off = 0
    for o_ref, wd in zip(out_refs, widths):
        o_ref[...] = _dot(hb, w_ref[:, off:off + wd]).astype(o_ref.dtype)
        off += wd


def _inproj_call(x, norm_w, sc, sh, w_cat, widths, dtypes, seq):
    t, d = x.shape
    tm = min(256, seq)
    per_b = seq // tm
    return pl.pallas_call(
        functools.partial(_inproj_kernel, widths=widths),
        grid=(t // tm,),
        in_specs=[
            pl.BlockSpec((tm, d), lambda i: (i, 0)),
            pl.BlockSpec((1, d), lambda i: (0, 0)),
            pl.BlockSpec((1, 1, d), lambda i: (i // per_b, 0, 0)),
            pl.BlockSpec((1, 1, d), lambda i: (i // per_b, 0, 0)),
            pl.BlockSpec(w_cat.shape, lambda i: (0, 0)),
        ],
        out_specs=[pl.BlockSpec((tm, wd), lambda i: (i, 0)) for wd in widths],
        out_shape=[jax.ShapeDtypeStruct((t, wd), dt) for wd, dt in zip(widths, dtypes)],
        compiler_params=_cparams(("parallel",)),
        name="norm1_inproj",
    )(x, norm_w.reshape(1, d), sc, sh, w_cat)


def _level_tables(c):
    nl = int(np.log2(c))
    assert 1 << nl == c
    idx = np.arange(c)
    mats, masks = [], [np.eye(c, dtype=np.float32)]
    t = idx[None, :]
    i = idx[:, None]
    for lvl in range(nl):
        h = c >> (lvl + 1)
        blk, pos = idx // (2 * h), idx % (2 * h)
        m = (blk * 2 * h + h - 1)[:, None]
        right = (pos >= h)[:, None]
        a = np.where(right, (t > m) & (t <= i), (t > i) & (t <= m))
        mats.append(a.astype(np.float32))
        same = blk[:, None] == blk[None, :]
        masks.append((same & right & (pos < h)[None, :]).astype(np.float32))
    mats.append((t <= i).astype(np.float32))
    mats.append((t > i).astype(np.float32))
    return np.concatenate(mats, 0), np.stack(masks, 0), nl


def _gated_chunk(q, k, v, g, mall_ref, masks_ref, st_ref, hd, c, nl):
    g3 = jnp.concatenate(_split3(g), axis=1)
    gs = _dot(mall_ref[...], g3)
    e = jnp.exp(gs[:, :LANE] + gs[:, LANE:2 * LANE] + gs[:, 2 * LANE:])
    scores = masks_ref[0] * _dot_nt(q.astype(BF16), k.astype(BF16))
    for lvl in range(nl):
        el = e[lvl * c:(lvl + 1) * c]
        scores = scores + masks_ref[lvl + 1] * _dot_nt((q * el).astype(BF16), (k * el).astype(BF16))
    e_cum = e[nl * c:(nl + 1) * c]
    e_suf = e[(nl + 1) * c:(nl + 2) * c]
    st = st_ref[hd]
    o = _dot(scores.astype(BF16), v) + _dot_nt((q * e_cum).astype(BF16), st.astype(BF16))
    st_ref[hd] = e_cum[c - 1:c, :] * st + _dot_tn(v, (k * e_suf).astype(BF16))
    return o


def _gla_kernel(q_ref, k_ref, v_ref, gg_ref, ga_ref, wa2_ref, ba2_ref, nw_ref, mall_ref, masks_ref,
                o_ref, st_ref, *, c, nl):
    @pl.when(pl.program_id(1) == 0)
    def _():
        st_ref[...] = jnp.zeros_like(st_ref)

    log_alpha = _log_sigmoid(_dot_f32(ga_ref[...], wa2_ref[...]) + ba2_ref[...]) * (1.0 / GLA_TAU)
    for hd in range(N_HEADS):
        sl = slice(hd * LANE, (hd + 1) * LANE)
        q = q_ref[:, sl].astype(F32) * (GLA_DK ** -0.5)
        k = k_ref[:, sl].astype(F32)
        o = _gated_chunk(q, k, v_ref[:, sl], log_alpha[:, sl], mall_ref, masks_ref, st_ref, hd, c, nl)
        y = o * lax.rsqrt(jnp.sum(o * o, axis=-1, keepdims=True) * (1.0 / GLA_DV) + EPS) * nw_ref[:, sl]
        gate = gg_ref[:, sl].astype(F32)
        o_ref[:, sl] = (y * gate * _sigmoid(gate)).astype(o_ref.dtype)


def _hg_kernel(q_ref, f_ref, v_ref, og_ref, lb_ref, nw_ref, mall_ref, masks_ref,
               o_ref, st_ref, *, c, nl):
    @pl.when(pl.program_id(1) == 0)
    def _():
        st_ref[...] = jnp.zeros_like(st_ref)

    lane = lax.broadcasted_iota(jnp.int32, (c, LANE), 1)
    for hd in range(N_HEADS):
        sl = slice(hd * LANE, (hd + 1) * LANE)
        lb = lb_ref[:, sl]
        hf = f_ref[:, sl]
        a = jnp.log(lb)
        b = jnp.log1p(-lb) + _log_sigmoid(hf)
        log_f = jnp.maximum(a, b) + jnp.log1p(jnp.exp(-jnp.abs(a - b)))
        k = jnp.where(lane < HG_DK, (1.0 - lb) * _sigmoid(-hf), 0.0)
        q = q_ref[:, sl].astype(F32)
        o = _gated_chunk(q, k, v_ref[:, sl], log_f, mall_ref, masks_ref, st_ref, hd, c, nl)
        y = o * lax.rsqrt(jnp.sum(o * o, axis=-1, keepdims=True) * (1.0 / HG_DV) + EPS) * nw_ref[:, sl]
        o_ref[:, sl] = (y * _sigmoid(og_ref[:, sl].astype(F32))).astype(o_ref.dtype)


def _chunk_specs(c, nb, seq, n_tok_parts):
    n = seq // c
    tok = pl.BlockSpec((c, HEAD_W), lambda b, i: (b * n + i, 0))
    return n, tok


def _const_spec(shape):
    nd = len(shape)
    return pl.BlockSpec(shape, lambda b, i: (0,) * nd)


def _gla_call(gq, gk, gv, gg, ga, wa2p, ba2p, nwp, nb, seq):
    c = min(128, seq)
    mall, masks, nl = _level_tables(c)
    n, tok = _chunk_specs(c, nb, seq, 4)
    return pl.pallas_call(
        functools.partial(_gla_kernel, c=c, nl=nl),
        grid=(nb, n),
        in_specs=[tok, tok, tok, tok,
                  pl.BlockSpec((c, LANE), lambda b, i: (b * n + i, 0)),
                  _const_spec(wa2p.shape), _const_spec(ba2p.shape), _const_spec(nwp.shape),
                  _const_spec(mall.shape), _const_spec(masks.shape)],
        out_specs=tok,
        out_shape=jax.ShapeDtypeStruct(gq.shape, BF16),
        scratch_shapes=[pltpu.VMEM((N_HEADS, LANE, LANE), F32)],
        compiler_params=_cparams(("parallel", "arbitrary")),
        name="gla_recurrence",
    )(gq, gk, gv, gg, ga, wa2p, ba2p, nwp, jnp.asarray(mall, BF16), jnp.asarray(masks))


def _hg_call(hq, hf, hi, hg, lbp, nwp, nb, seq):
    c = min(128, seq)
    mall, masks, nl = _level_tables(c)
    n, tok = _chunk_specs(c, nb, seq, 4)
    return pl.pallas_call(
        functools.partial(_hg_kernel, c=c, nl=nl),
        grid=(nb, n),
        in_specs=[tok, tok, tok, tok,
                  _const_spec(lbp.shape), _const_spec(nwp.shape),
                  _const_spec(mall.shape), _const_spec(masks.shape)],
        out_specs=tok,
        out_shape=jax.ShapeDtypeStruct(hq.shape, BF16),
        scratch_shapes=[pltpu.VMEM((N_HEADS, LANE, LANE), F32)],
        compiler_params=_cparams(("parallel", "arbitrary")),
        name="hgrn2_recurrence",
    )(hq, hf, hi, hg, lbp, nwp, jnp.asarray(mall, BF16), jnp.asarray(masks))


def _ret_tables(c):
    hs = np.arange(N_HEADS, dtype=np.float64)
    log_gamma = np.log(1.0 - np.exp2(-5.0 - hs))
    idx = np.arange(c, dtype=np.float64)
    rel = idx[:, None] - idx[None, :]
    dmat = np.where(rel >= 0, np.exp(log_gamma[:, None, None] * np.maximum(rel, 0.0)), 0.0)
    qdec = np.exp(log_gamma[:, None] * (idx + 1.0))
    kdec = np.exp(log_gamma[:, None] * (c - 1.0 - idx))
    cdec = np.exp(log_gamma * c)
    full = lambda a: np.broadcast_to(a[:, :, None], (N_HEADS, c, LANE)).astype(np.float32)
    return dmat.astype(np.float32), full(qdec), full(kdec), [float(np.float32(v)) for v in cdec]


def _ret_kernel(q_ref, k_ref, v_ref, og_ref, cos_ref, sin_ref, dmat_ref, qdec_ref, kdec_ref,
                o_ref, st_ref, *, c, cdec):
    @pl.when(pl.program_id(1) == 0)
    def _():
        st_ref[...] = jnp.zeros_like(st_ref)

    half = RET_DK // 2
    lane = lax.broadcasted_iota(jnp.int32, (c, LANE), 1)
    cos = cos_ref[0]
    sin = sin_ref[0]

    def rotary(t):
        rot = jnp.where(lane < half, -pltpu.roll(t, LANE - half, 1), pltpu.roll(t, half, 1))
        return t * cos + rot * sin

    for hd in range(N_HEADS):
        sl = slice(hd * LANE, (hd + 1) * LANE)
        q = rotary(q_ref[:, sl].astype(F32)) * (RET_DK ** -0.5)
        k = rotary(k_ref[:, sl].astype(F32))
        v = v_ref[:, sl]
        st = st_ref[hd]
        scores = _dot_nt(q.astype(BF16), k.astype(BF16)) * dmat_ref[hd]
        o = _dot(scores.astype(BF16), v) + _dot_nt((q * qdec_ref[hd]).astype(BF16), st.astype(BF16))
        st_ref[hd] = cdec[hd] * st + _dot_tn(v, (k * kdec_ref[hd]).astype(BF16))
        mu = jnp.sum(o, axis=-1, keepdims=True) * (1.0 / RET_DV)
        dlt = jnp.where(lane < RET_DV, o - mu, 0.0)
        var = jnp.sum(dlt * dlt, axis=-1, keepdims=True) * (1.0 / RET_DV)
        gate = og_ref[:, sl].astype(F32)
        o_ref[:, sl] = (dlt * lax.rsqrt(var + EPS) * gate * _sigmoid(gate)).astype(o_ref.dtype)


def _ret_call(rq, rk, rv, rg, cos, sin, nb, seq):
    c = min(128, seq)
    dmat, qdec, kdec, cdec = _ret_tables(c)
    n, tok = _chunk_specs(c, nb, seq, 4)
    rope = pl.BlockSpec((1, c, LANE), lambda b, i: (b, i, 0))
    return pl.pallas_call(
        functools.partial(_ret_kernel, c=c, cdec=cdec),
        grid=(nb, n),
        in_specs=[tok, tok, tok, tok, rope, rope,
                  _const_spec(dmat.shape), _const_spec(qdec.shape), _const_spec(kdec.shape)],
        out_specs=tok,
        out_shape=jax.ShapeDtypeStruct(rq.shape, BF16),
        scratch_shapes=[pltpu.VMEM((N_HEADS, LANE, LANE), F32)],
        compiler_params=_cparams(("parallel", "arbitrary")),
        name="retention",
    )(rq, rk, rv, rg, cos, sin, jnp.asarray(dmat), jnp.asarray(qdec), jnp.asarray(kdec))


def _outproj_router_kernel(x_ref, oa_ref, ob_ref, oc_ref, wo_ref, g1_ref, nw_ref, sc_ref, sh_ref,
                           wr_ref, br_ref, tri_ref,
                           xo_ref, h_ref, te_ref, tw_ref, rk_ref, cnt_ref, *, tm):
    @pl.when(pl.program_id(0) == 0)
    def _():
        cnt_ref[...] = jnp.zeros_like(cnt_ref)

    mix = (_dot(oa_ref[...], wo_ref[0:HEAD_W, :]) + _dot(ob_ref[...], wo_ref[HEAD_W:2 * HEAD_W, :])
           + _dot(oc_ref[...], wo_ref[2 * HEAD_W:3 * HEAD_W, :]))
    x = x_ref[...] + g1_ref[0] * mix
    xo_ref[...] = x
    y = x * lax.rsqrt(jnp.mean(x * x, axis=-1, keepdims=True) + EPS) * nw_ref[...]
    h = y * (1.0 + sc_ref[0]) + sh_ref[0]
    h_ref[...] = h

    lg = _dot_f32(h, wr_ref[...]) + br_ref[...]
    lane = lax.broadcasted_iota(jnp.int32, (tm, LANE), 1)
    sel_e, sel_v = [], []
    for _ in range(TOP_K):
        m = jnp.max(lg, axis=-1, keepdims=True)
        idx = jnp.min(jnp.where(lg == m, lane, LANE), axis=-1, keepdims=True)
        sel_e.append(idx)
        sel_v.append(m)
        lg = jnp.where(lane == idx, -jnp.inf, lg)
    ex = [jnp.exp(v - sel_v[0]) for v in sel_v]
    den = ex[0] + ex[1] + ex[2] + ex[3]
    hot = [(lane == idx) for idx in sel_e]
    onehot = jnp.zeros((tm, LANE), F32)
    for hk in hot:
        onehot = onehot + jnp.where(hk, 1.0, 0.0)
    before = _dot(tri_ref[...], onehot.astype(BF16)) + cnt_ref[...]
    te = jnp.zeros((tm, LANE), jnp.int32)
    tw = jnp.zeros((tm, LANE), F32)
    rk = jnp.zeros((tm, LANE), jnp.int32)
    for kk in range(TOP_K):
        rank = jnp.sum(jnp.where(hot[kk], before, 0.0), axis=-1, keepdims=True).astype(jnp.int32)
        te = jnp.where(lane == kk, sel_e[kk], te)
        tw = jnp.where(lane == kk, ex[kk] / den, tw)
        rk = jnp.where(lane == kk, rank, rk)
    te_ref[...] = te
    tw_ref[...] = tw
    rk_ref[...] = rk
    cnt_ref[...] = cnt_ref[...] + jnp.sum(onehot, axis=0, keepdims=True)


def _outproj_router_call(x, oa, ob, oc, wo, g1, nw, sc, sh, wr, br, seq):
    t, d = x.shape
    tm = min(256, seq)
    per_b = seq // tm
    tri = np.tril(np.ones((tm, tm), np.float32), -1)
    row = lambda w: pl.BlockSpec((tm, w), lambda i: (i, 0))
    const = lambda shape: pl.BlockSpec(shape, lambda i: (0,) * len(shape))
    perb = pl.BlockSpec((1, 1, d), lambda i: (i // per_b, 0, 0))
    return pl.pallas_call(
        functools.partial(_outproj_router_kernel, tm=tm),
        grid=(t // tm,),
        in_specs=[row(d), row(HEAD_W), row(HEAD_W), row(HEAD_W), const(wo.shape), perb,
                  const((1, d)), perb, perb, const(wr.shape), const(br.shape), const(tri.shape)],
        out_specs=[row(d), row(d), row(LANE), row(LANE), row(LANE), const((1, LANE))],
        out_shape=[jax.ShapeDtypeStruct((t, d), F32), jax.ShapeDtypeStruct((t, d), F32),
                   jax.ShapeDtypeStruct((t, LANE), jnp.int32), jax.ShapeDtypeStruct((t, LANE), F32),
                   jax.ShapeDtypeStruct((t, LANE), jnp.int32), jax.ShapeDtypeStruct((1, LANE), F32)],
        compiler_params=_cparams(("arbitrary",)),
        name="outproj_norm2_router",
    )(x, oa, ob, oc, wo, g1, nw.reshape(1, d), sc, sh, wr, br, jnp.asarray(tri, BF16))


def _dispatch_kernel(dest_ref, h_ref, buf_in_ref, buf_ref, sem, *, tm):
    del buf_in_ref

    def copy(r, kk):
        return pltpu.make_async_copy(h_ref.at[pl.ds(r, 1)],
                                     buf_ref.at[pl.ds(dest_ref[r * TOP_K + kk], 1)], sem)

    def issue(r, carry):
        for kk in range(TOP_K):
            copy(r, kk).start()
        return carry

    def drain(r, carry):
        for kk in range(TOP_K):
            copy(r, kk).wait()
        return carry

    lax.fori_loop(0, tm, issue, 0)
    lax.fori_loop(0, tm, drain, 0)


def _dispatch_call(h, dest_flat, n_slots):
    t, d = h.shape
    tm = min(256, t)
    buf0 = jnp.zeros((n_slots, d), F32)
    return pl.pallas_call(
        functools.partial(_dispatch_kernel, tm=tm),
        grid=(t // tm,),
        in_specs=[pl.BlockSpec((tm * TOP_K,), lambda i: (i,), memory_space=pltpu.SMEM),
                  pl.BlockSpec((tm, d), lambda i: (i, 0)),
                  pl.BlockSpec(memory_space=pl.ANY)],
        out_specs=pl.BlockSpec(memory_space=pl.ANY),
        out_shape=jax.ShapeDtypeStruct((n_slots, d), F32),
        scratch_shapes=[pltpu.SemaphoreType.DMA(())],
        input_output_aliases={2: 0},
        compiler_params=_cparams(("arbitrary",)),
        name="moe_dispatch",
    )(dest_flat, h, buf0)


def _expert_kernel(be_ref, nu_ref, x_ref, wgu_ref, bgu_ref, wdn_ref, bdn_ref, o_ref, *, d_ff):
    @pl.when(pl.program_id(0) < nu_ref[0])
    def _():
        gu = _dot(x_ref[...].astype(BF16), wgu_ref[0]) + bgu_ref[0]
        gate = jnp.minimum(gu[:, :d_ff], SWIGLU_LIMIT)
        up = jnp.clip(gu[:, d_ff:], -SWIGLU_LIMIT, SWIGLU_LIMIT)
        act = (up + 1.0) * gate * _sigmoid(SWIGLU_ALPHA * gate)
        o_ref[...] = _dot(act.astype(BF16), wdn_ref[0]) + bdn_ref[0]

    @pl.when(pl.program_id(0) >= nu_ref[0])
    def _():
        o_ref[...] = jnp.zeros_like(o_ref)


def _expert_call(buf, block_e, n_used, wgu, bgu, wdn, bdn):
    n_slots, d = buf.shape
    n_blocks = n_slots // MOE_BLOCK
    ne, _, f2 = wgu.shape
    d_ff = f2 // 2
    blk = lambda i, be, nu: (jnp.minimum(i, nu[0] - 1), 0)
    exp3 = lambda i, be, nu: (be[jnp.minimum(i, nu[0] - 1)], 0, 0)
    grid_spec = pltpu.PrefetchScalarGridSpec(
        num_scalar_prefetch=2,
        grid=(n_blocks,),
        in_specs=[pl.BlockSpec((MOE_BLOCK, d), blk),
                  pl.BlockSpec((1, d, f2), exp3),
                  pl.BlockSpec((1, 1, f2), exp3),
                  pl.BlockSpec((1, d_ff, d), exp3),
                  pl.BlockSpec((1, 1, d), exp3)],
        out_specs=pl.BlockSpec((MOE_BLOCK, d), lambda i, be, nu: (i, 0)),
    )
    return pl.pallas_call(
        functools.partial(_expert_kernel, d_ff=d_ff),
        grid_spec=grid_spec,
        out_shape=jax.ShapeDtypeStruct((n_slots, d), F32),
        compiler_params=_cparams(("arbitrary",)),
        name="moe_experts",
    )(block_e, n_used, buf, wgu, bgu.reshape(ne, 1, f2), wdn, bdn.reshape(ne, 1, d))


def _combine_kernel(dest_ref, yb_ref, w_ref, x_ref, g2_ref, o_ref, rows_ref, sem, *, tm):
    def copy(r, kk):
        return pltpu.make_async_copy(yb_ref.at[pl.ds(dest_ref[r * TOP_K + kk], 1)],
                                     rows_ref.at[kk, pl.ds(r, 1)], sem)

    def issue(r, carry):
        for kk in range(TOP_K):
            copy(r, kk).start()
        return carry

    def drain(r, carry):
        for kk in range(TOP_K):
            copy(r, kk).wait()
        return carry

    lax.fori_loop(0, tm, issue, 0)
    lax.fori_loop(0, tm, drain, 0)
    w = w_ref[...]
    y = w[:, 0:1] * rows_ref[0]
    for kk in range(1, TOP_K):
        y = y + w[:, kk:kk + 1] * rows_ref[kk]
    o_ref[...] = x_ref[...] + g2_ref[0] * y


def _combine_call(yb, dest_flat, tw, x, g2, seq):
    t, d = x.shape
    tm = min(256, seq)
    per_b = seq // tm
    return pl.pallas_call(
        functools.partial(_combine_kernel, tm=tm),
        grid=(t // tm,),
        in_specs=[pl.BlockSpec((tm * TOP_K,), lambda i: (i,), memory_space=pltpu.SMEM),
                  pl.BlockSpec(memory_space=pl.ANY),
                  pl.BlockSpec((tm, LANE), lambda i: (i, 0)),
                  pl.BlockSpec((tm, d), lambda i: (i, 0)),
                  pl.BlockSpec((1, 1, d), lambda i: (i // per_b, 0, 0))],
        out_specs=pl.BlockSpec((tm, d), lambda i: (i, 0)),
        out_shape=jax.ShapeDtypeStruct((t, d), F32),
        scratch_shapes=[pltpu.VMEM((TOP_K, tm, d), F32), pltpu.SemaphoreType.DMA(())],
        compiler_params=_cparams(("arbitrary",)),
        name="moe_combine",
    )(dest_flat, yb, tw, x, g2)


def _final_norm_kernel(x_ref, w_ref, o_ref):
    x = x_ref[...]
    o_ref[...] = x * lax.rsqrt(jnp.mean(x * x, axis=-1, keepdims=True) + EPS) * w_ref[...]


def _final_norm_call(x, w):
    t, d = x.shape
    tm = min(512, t)
    return pl.pallas_call(
        _final_norm_kernel,
        grid=(t // tm,),
        in_specs=[pl.BlockSpec((tm, d), lambda i: (i, 0)), pl.BlockSpec((1, d), lambda i: (0, 0))],
        out_specs=pl.BlockSpec((tm, d), lambda i: (i, 0)),
        out_shape=jax.ShapeDtypeStruct((t, d), F32),
        compiler_params=_cparams(("parallel",)),
        name="final_norm",
    )(x, w.reshape(1, d))


def _pad_heads_cols(w, hd):
    r = w.shape[0]
    w = w.reshape(r, N_HEADS, hd)
    return jnp.pad(w, ((0, 0), (0, 0), (0, LANE - hd))).reshape(r, HEAD_W)


def _pad_heads_rows(w, hd):
    return _pad_heads_cols(w.T, hd).T


IN_PARTS = (("gq", GLA_DK), ("gk", GLA_DK), ("gv", GLA_DV), ("gg", GLA_DV), ("ga", None),
            ("hq", HG_DK), ("hf", HG_DK), ("hi", HG_DV), ("hg", HG_DV),
            ("rq", RET_DK), ("rk", RET_DK), ("rv", RET_DV), ("rg", RET_DV))
F32_PARTS = ("ga", "hf")


def _layout_w_in(w_in):
    cols, off = [], 0
    for name, hd in IN_PARTS:
        if hd is None:
            part = jnp.pad(w_in[:, off:off + GLA_RANK], ((0, 0), (0, LANE - GLA_RANK)))
            off += GLA_RANK
        else:
            part = _pad_heads_cols(w_in[:, off:off + N_HEADS * hd], hd)
            off += N_HEADS * hd
        cols.append(part)
    return jnp.concatenate(cols, axis=1).astype(BF16)


def kernel(x, c, positions, w_mod, b_mod, norm1_w, w_in, gla_wa2, gla_ba2, hg_lb, gla_norm_w, hg_norm_w,
           w_out, norm2_w, w_r, b_r, w_gu, b_gu, w_dn, b_dn, final_norm_w):
    nb, seq, d = x.shape
    t = nb * seq
    depth = w_mod.shape[0]
    widths = tuple(LANE if hd is None else HEAD_W for _, hd in IN_PARTS)
    dtypes = tuple(F32 if name in F32_PARTS else BF16 for name, _ in IN_PARTS)

    mod = _mod_call(c, w_mod, b_mod)
    cos, sin = _rope_call(positions)
    lb_all = _lb_call(hg_lb)

    n_assign = t * TOP_K
    n_blocks = (n_assign + MOE_BLOCK - 1) // MOE_BLOCK + N_EXPERTS
    n_slots = n_blocks * MOE_BLOCK

    xf = x.reshape(t, d)
    for layer in range(depth):
        sh1, sc1, g1, sh2, sc2, g2 = [m.reshape(nb, 1, d) for m in jnp.split(mod[layer], 6, axis=-1)]

        parts = _inproj_call(xf, norm1_w[layer], sc1, sh1, _layout_w_in(w_in[layer]), widths, dtypes, seq)
        gq, gk, gv, gg, ga, hq, hf, hi, hg, rq, rk, rv, rg = parts
        wa2p = jnp.pad(_pad_heads_cols(gla_wa2[layer], GLA_DK), ((0, LANE - GLA_RANK), (0, 0)))
        ba2p = _pad_heads_cols(gla_ba2[layer].reshape(1, -1), GLA_DK)
        o_gla = _gla_call(gq, gk, gv, gg, ga, wa2p, ba2p,
                          _pad_heads_cols(gla_norm_w[layer].reshape(1, -1), GLA_DV), nb, seq)
        o_hg = _hg_call(hq, hf, hi, hg, _pad_heads_cols(lb_all[layer].reshape(1, -1), HG_DK),
                        _pad_heads_cols(hg_norm_w[layer].reshape(1, -1), HG_DV), nb, seq)
        o_ret = _ret_call(rq, rk, rv, rg, cos, sin, nb, seq)

        wo = w_out[layer]
        n_gla, n_hg = N_HEADS * GLA_DV, N_HEADS * HG_DV
        wo_p = jnp.concatenate([_pad_heads_rows(wo[:n_gla], GLA_DV),
                                _pad_heads_rows(wo[n_gla:n_gla + n_hg], HG_DV),
                                _pad_heads_rows(wo[n_gla + n_hg:], RET_DV)], axis=0).astype(BF16)
        wr_p = jnp.pad(w_r[layer], ((0, 0), (0, LANE - N_EXPERTS)))
        br_p = jnp.pad(b_r[layer].reshape(1, -1), ((0, 0), (0, LANE - N_EXPERTS)), constant_values=NEG_BIG)
        xf, h2, te, tw, rk_, cnt = _outproj_router_call(xf, o_gla, o_hg, o_ret, wo_p, g1, norm2_w[layer],
                                                        sc2, sh2, wr_p, br_p, seq)

        counts = cnt[0, :N_EXPERTS].astype(jnp.int32)
        padded = (counts + MOE_BLOCK - 1) // MOE_BLOCK * MOE_BLOCK
        pends = jnp.cumsum(padded)
        pstarts = pends - padded
        top_e = te[:, :TOP_K]
        dest = (pstarts[top_e] + rk_[:, :TOP_K]).reshape(-1).astype(jnp.int32)
        block_start = jnp.arange(n_blocks, dtype=jnp.int32) * MOE_BLOCK
        block_e = jnp.minimum(jnp.searchsorted(pends, block_start, side="right"), N_EXPERTS - 1).astype(jnp.int32)
        n_used = (pends[-1:] // MOE_BLOCK).astype(jnp.int32)

        buf = _dispatch_call(h2, dest, n_slots)
        yb = _expert_call(buf, block_e, n_used, w_gu[layer].astype(BF16), b_gu[layer],
                          w_dn[layer].astype(BF16), b_dn[layer])
        xf = _combine_call(yb, dest, tw, xf, g2, seq)

    return _final_norm_call(xf, final_norm_w).reshape(nb, seq, d)
```

```python
import functools

import numpy as np
import jax
import jax.numpy as jnp
from jax import lax
from jax.experimental import pallas as pl
from jax.experimental.pallas import tpu as pltpu

F32 = jnp.float32
BF16 = jnp.bfloat16

N_HEADS = 4
GLA_DK, GLA_DV, GLA_RANK, GLA_TAU = 48, 96, 16, 16.0
HG_DK, HG_DV = 64, 64
RET_DK, RET_DV = 48, 96
ROPE_BASE = 10000.0
N_EXPERTS, TOP_K = 32, 4
SWIGLU_LIMIT, SWIGLU_ALPHA = 7.0, 1.702
MOE_BLOCK = 512
EPS = 1e-6

LANE = 128
HEAD_W = N_HEADS * LANE
VMEM_LIMIT = 56 * 1024 * 1024
NEG_BIG = -1e30


def _cparams(sem):
    return pltpu.CompilerParams(dimension_semantics=sem, vmem_limit_bytes=VMEM_LIMIT)


def _split3(a):
    hi = a.astype(BF16)
    r = a - hi.astype(F32)
    mid = r.astype(BF16)
    lo = (r - mid.astype(F32)).astype(BF16)
    return hi, mid, lo


def _dot(a, b):
    return jnp.dot(a, b, preferred_element_type=F32)


def _dot_nt(a, b):
    return lax.dot_general(a, b, (((1,), (1,)), ((), ())), preferred_element_type=F32)


def _dot_tn(a, b):
    return lax.dot_general(a, b, (((0,), (0,)), ((), ())), preferred_element_type=F32)


def _dot_f32(a, b):
    a_hi = a.astype(BF16)
    a_lo = (a - a_hi.astype(F32)).astype(BF16)
    b_hi = b.astype(BF16)
    b_lo = (b - b_hi.astype(F32)).astype(BF16)
    return _dot(a_hi, b_hi) + _dot(a_hi, b_lo) + _dot(a_lo, b_hi)


def _sigmoid(x):
    return 1.0 / (1.0 + jnp.exp(-x))


def _log_sigmoid(x):
    return jnp.minimum(x, 0.0) - jnp.log1p(jnp.exp(-jnp.abs(x)))


def _mod_kernel(c_ref, w_ref, b_ref, o_ref):
    c = c_ref[...]
    o_ref[0] = _dot_f32(c * _sigmoid(c), w_ref[0]) + b_ref[0]


def _mod_call(c, w_mod, b_mod):
    depth, d, d6 = w_mod.shape
    nb = c.shape[0]
    rows = 8
    c_pad = jnp.zeros((rows, d), F32).at[:nb].set(c)
    out = pl.pallas_call(
        _mod_kernel,
        grid=(depth, d6 // d),
        in_specs=[
            pl.BlockSpec((rows, d), lambda l, j: (0, 0)),
            pl.BlockSpec((1, d, d), lambda l, j: (l, 0, j)),
            pl.BlockSpec((1, 1, d), lambda l, j: (l, 0, j)),
        ],
        out_specs=pl.BlockSpec((1, rows, d), lambda l, j: (l, 0, j)),
        out_shape=jax.ShapeDtypeStruct((depth, rows, d6), F32),
        compiler_params=_cparams(("parallel", "parallel")),
        name="mod",
    )(c_pad, w_mod, b_mod.reshape(depth, 1, d6))
    return out[:, :nb]


def _rope_kernel(pos_ref, freq_ref, cos_ref, sin_ref):
    ang = pos_ref[0] * freq_ref[...]
    cos_ref[0] = jnp.cos(ang)
    sin_ref[0] = jnp.sin(ang)


def _rope_call(positions):
    nb, l = positions.shape
    half = RET_DK // 2
    inv = (ROPE_BASE ** (-np.arange(half, dtype=np.float32) / half)).astype(np.float32)
    freq = np.zeros((1, LANE), np.float32)
    freq[0, :half] = inv
    freq[0, half:2 * half] = inv
    tl = min(l, 512)
    pos = positions.astype(F32).reshape(nb, l, 1)
    shp = jax.ShapeDtypeStruct((nb, l, LANE), F32)
    return pl.pallas_call(
        _rope_kernel,
        grid=(nb, l // tl),
        in_specs=[pl.BlockSpec((1, tl, 1), lambda b, i: (b, i, 0)),
                  pl.BlockSpec((1, LANE), lambda b, i: (0, 0))],
        out_specs=[pl.BlockSpec((1, tl, LANE), lambda b, i: (b, i, 0))] * 2,
        out_shape=[shp, shp],
        compiler_params=_cparams(("parallel", "parallel")),
        name="rope_tables",
    )(pos, jnp.asarray(freq))


def _lb_kernel(p_ref, o_ref):
    p = p_ref[...]
    depth = p.shape[0]
    m = jnp.max(p, axis=0, keepdims=True)
    e = jnp.exp(p - m)
    sm = e / jnp.sum(e, axis=0, keepdims=True)
    acc = jnp.zeros_like(sm[0:1])
    for i in range(depth):
        if i > 0:
            acc = acc + sm[i:i + 1]
        o_ref[i:i + 1, :] = acc


def _lb_call(hg_lb):
    return pl.pallas_call(
        _lb_kernel,
        out_shape=jax.ShapeDtypeStruct(hg_lb.shape, F32),
        name="hg_lower_bounds",
    )(hg_lb.astype(F32))


def _inproj_kernel(x_ref, nw_ref, sc_ref, sh_ref, w_ref, *out_refs, widths):
    x = x_ref[...]
    y = x * lax.rsqrt(jnp.mean(x * x, axis=-1, keepdims=True) + EPS) * nw_ref[...]
    hb = (y * (1.0 + sc_ref[0]) + sh_ref[0]).astype(BF16)
    off = 0
    for o_ref, wd in zip(out_refs, widths):
        o_ref[...] = _dot(hb, w_ref[:, off:off + wd]).astype(o_ref.dtype)
        off += wd


def _inproj_call(x, norm_w, sc, sh, w_cat, widths, dtypes, seq):
    t, d = x.shape
    tm = min(256, seq)
    per_b = seq // tm
    return pl.pallas_call(
        functools.partial(_inproj_kernel, widths=widths),
        grid=(t // tm,),
        in_specs=[
            pl.BlockSpec((tm, d), lambda i: (i, 0)),
            pl.BlockSpec((1, d), lambda i: (0, 0)),
            pl.BlockSpec((1, 1, d), lambda i: (i // per_b, 0, 0)),
            pl.BlockSpec((1, 1, d), lambda i: (i // per_b, 0, 0)),
            pl.BlockSpec(w_cat.shape, lambda i: (0, 0)),
        ],
        out_specs=[pl.BlockSpec((tm, wd), lambda i: (i, 0)) for wd in widths],
        out_shape=[jax.ShapeDtypeStruct((t, wd), dt) for wd, dt in zip(widths, dtypes)],
        compiler_params=_cparams(("parallel",)),
        name="norm1_inproj",
    )(x, norm_w.reshape(1, d), sc, sh, w_cat)


def _level_tables(c):
    nl = int(np.log2(c))
    assert 1 << nl == c
    idx = np.arange(c)
    mats, masks = [], [np.eye(c, dtype=np.float32)]
    t = idx[None, :]
    i = idx[:, None]
    for lvl in range(nl):
        h = c >> (lvl + 1)
        blk, pos = idx // (2 * h), idx % (2 * h)
        m = (blk * 2 * h + h - 1)[:, None]
        right = (pos >= h)[:, None]
        a = np.where(right, (t > m) & (t <= i), (t > i) & (t <= m))
        mats.append(a.astype(np.float32))
        same = blk[:, None] == blk[None, :]
        masks.append((same & right & (pos < h)[None, :]).astype(np.float32))
    mats.append((t <= i).astype(np.float32))
    mats.append((t > i).astype(np.float32))
    return np.concatenate(mats, 0), np.stack(masks, 0), nl


def _gated_chunk(q, k, v, g, mall_ref, masks_ref, st_ref, hd, c, nl):
    g3 = jnp.concatenate(_split3(g), axis=1)
    gs = _dot(mall_ref[...], g3)
    e = jnp.exp(gs[:, :LANE] + gs[:, LANE:2 * LANE] + gs[:, 2 * LANE:])
    scores = masks_ref[0] * _dot_nt(q.astype(BF16), k.astype(BF16))
    for lvl in range(nl):
        el = e[lvl * c:(lvl + 1) * c]
        scores = scores + masks_ref[lvl + 1] * _dot_nt((q * el).astype(BF16), (k * el).astype(BF16))
    e_cum = e[nl * c:(nl + 1) * c]
    e_suf = e[(nl + 1) * c:(nl + 2) * c]
    st = st_ref[hd]
    o = _dot(scores.astype(BF16), v) + _dot_nt((q * e_cum).astype(BF16), st.astype(BF16))
    st_ref[hd] = e_cum[c - 1:c, :] * st + _dot_tn(v, (k * e_suf).astype(BF16))
    return o


def _gla_kernel(q_ref, k_ref, v_ref, gg_ref, ga_ref, wa2_ref, ba2_ref, nw_ref, mall_ref, masks_ref,
                o_ref, st_ref, *, c, nl):
    @pl.when(pl.program_id(1) == 0)
    def _():
        st_ref[...] = jnp.zeros_like(st_ref)

    log_alpha = _log_sigmoid(_dot_f32(ga_ref[...], wa2_ref[...]) + ba2_ref[...]) * (1.0 / GLA_TAU)
    for hd in range(N_HEADS):
        sl = slice(hd * LANE, (hd + 1) * LANE)
        q = q_ref[:, sl].astype(F32) * (GLA_DK ** -0.5)
        k = k_ref[:, sl].astype(F32)
        o = _gated_chunk(q, k, v_ref[:, sl], log_alpha[:, sl], mall_ref, masks_ref, st_ref, hd, c, nl)
        y = o * lax.rsqrt(jnp.sum(o * o, axis=-1, keepdims=True) * (1.0 / GLA_DV) + EPS) * nw_ref[:, sl]
        gate = gg_ref[:, sl].astype(F32)
        o_ref[:, sl] = (y * gate * _sigmoid(gate)).astype(o_ref.dtype)


def _hg_kernel(q_ref, f_ref, v_ref, og_ref, lb_ref, nw_ref, mall_ref, masks_ref,
               o_ref, st_ref, *, c, nl):
    @pl.when(pl.program_id(1) == 0)
    def _():
        st_ref[...] = jnp.zeros_like(st_ref)

    lane = lax.broadcasted_iota(jnp.int32, (c, LANE), 1)
    for hd in range(N_HEADS):
        sl = slice(hd * LANE, (hd + 1) * LANE)
        lb = lb_ref[:, sl]
        hf = f_ref[:, sl]
        a = jnp.log(lb)
        b = jnp.log1p(-lb) + _log_sigmoid(hf)
        log_f = jnp.maximum(a, b) + jnp.log1p(jnp.exp(-jnp.abs(a - b)))
        k = jnp.where(lane < HG_DK, (1.0 - lb) * _sigmoid(-hf), 0.0)
        q = q_ref[:, sl].astype(F32)
        o = _gated_chunk(q, k, v_ref[:, sl], log_f, mall_ref, masks_ref, st_ref, hd, c, nl)
        y = o * lax.rsqrt(jnp.sum(o * o, axis=-1, keepdims=True) * (1.0 / HG_DV) + EPS) * nw_ref[:, sl]
        o_ref[:, sl] = (y * _sigmoid(og_ref[:, sl].astype(F32))).astype(o_ref.dtype)


def _chunk_specs(c, nb, seq, n_tok_parts):
    n = seq // c
    tok = pl.BlockSpec((c, HEAD_W), lambda b, i: (b * n + i, 0))
    return n, tok


def _const_spec(shape):
    nd = len(shape)
    return pl.BlockSpec(shape, lambda b, i: (0,) * nd)


def _gla_call(gq, gk, gv, gg, ga, wa2p, ba2p, nwp, nb, seq):
    c = min(128, seq)
    mall, masks, nl = _level_tables(c)
    n, tok = _chunk_specs(c, nb, seq, 4)
    return pl.pallas_call(
        functools.partial(_gla_kernel, c=c, nl=nl),
        grid=(nb, n),
        in_specs=[tok, tok, tok, tok,
                  pl.BlockSpec((c, LANE), lambda b, i: (b * n + i, 0)),
                  _const_spec(wa2p.shape), _const_spec(ba2p.shape), _const_spec(nwp.shape),
                  _const_spec(mall.shape), _const_spec(masks.shape)],
        out_specs=tok,
        out_shape=jax.ShapeDtypeStruct(gq.shape, BF16),
        scratch_shapes=[pltpu.VMEM((N_HEADS, LANE, LANE), F32)],
        compiler_params=_cparams(("parallel", "arbitrary")),
        name="gla_recurrence",
    )(gq, gk, gv, gg, ga, wa2p, ba2p, nwp, jnp.asarray(mall, BF16), jnp.asarray(masks))


def _hg_call(hq, hf, hi, hg, lbp, nwp, nb, seq):
    c = min(128, seq)
    mall, masks, nl = _level_tables(c)
    n, tok = _chunk_specs(c, nb, seq, 4)
    return pl.pallas_call(
        functools.partial(_hg_kernel, c=c, nl=nl),
        grid=(nb, n),
        in_specs=[tok, tok, tok, tok,
                  _const_spec(lbp.shape), _const_spec(nwp.shape),
                  _const_spec(mall.shape), _const_spec(masks.shape)],
        out_specs=tok,
        out_shape=jax.ShapeDtypeStruct(hq.shape, BF16),
        scratch_shapes=[pltpu.VMEM((N_HEADS, LANE, LANE), F32)],
        compiler_params=_cparams(("parallel", "arbitrary")),
        name="hgrn2_recurrence",
    )(hq, hf, hi, hg, lbp, nwp, jnp.asarray(mall, BF16), jnp.asarray(masks))


def _ret_tables(c):
    hs = np.arange(N_HEADS, dtype=np.float64)
    log_gamma = np.log(1.0 - np.exp2(-5.0 - hs))
    idx = np.arange(c, dtype=np.float64)
    rel = idx[:, None] - idx[None, :]
    dmat = np.where(rel >= 0, np.exp(log_gamma[:, None, None] * np.maximum(rel, 0.0)), 0.0)
    qdec = np.exp(log_gamma[:, None] * (idx + 1.0))
    kdec = np.exp(log_gamma[:, None] * (c - 1.0 - idx))
    cdec = np.exp(log_gamma * c)
    full = lambda a: np.broadcast_to(a[:, :, None], (N_HEADS, c, LANE)).astype(np.float32)
    return dmat.astype(np.float32), full(qdec), full(kdec), [float(np.float32(v)) for v in cdec]


def _ret_kernel(q_ref, k_ref, v_ref, og_ref, cos_ref, sin_ref, dmat_ref, qdec_ref, kdec_ref,
                o_ref, st_ref, *, c, cdec):
    @pl.when(pl.program_id(1) == 0)
    def _():
        st_ref[...] = jnp.zeros_like(st_ref)

    half = RET_DK // 2
    lane = lax.broadcasted_iota(jnp.int32, (c, LANE), 1)
    cos = cos_ref[0]
    sin = sin_ref[0]

    def rotary(t):
        rot = jnp.where(lane < half, -pltpu.roll(t, LANE - half, 1), pltpu.roll(t, half, 1))
        return t * cos + rot * sin

    for hd in range(N_HEADS):
        sl = slice(hd * LANE, (hd + 1) * LANE)
        q = rotary(q_ref[:, sl].astype(F32)) * (RET_DK ** -0.5)
        k = rotary(k_ref[:, sl].astype(F32))
        v = v_ref[:, sl]
        st = st_ref[hd]
        scores = _dot_nt(q.astype(BF16), k.astype(BF16)) * dmat_ref[hd]
        o = _dot(scores.astype(BF16), v) + _dot_nt((q * qdec_ref[hd]).astype(BF16), st.astype(BF16))
        st_ref[hd] = cdec[hd] * st + _dot_tn(v, (k * kdec_ref[hd]).astype(BF16))
        mu = jnp.sum(o, axis=-1, keepdims=True) * (1.0 / RET_DV)
        dlt = jnp.where(lane < RET_DV, o - mu, 0.0)
        var = jnp.sum(dlt * dlt, axis=-1, keepdims=True) * (1.0 / RET_DV)
        gate = og_ref[:, sl].astype(F32)
        o_ref[:, sl] = (dlt * lax.rsqrt(var + EPS) * gate * _sigmoid(gate)).astype(o_ref.dtype)


def _ret_call(rq, rk, rv, rg, cos, sin, nb, seq):
    c = min(128, seq)
    dmat, qdec, kdec, cdec = _ret_tables(c)
    n, tok = _chunk_specs(c, nb, seq, 4)
    rope = pl.BlockSpec((1, c, LANE), lambda b, i: (b, i, 0))
    return pl.pallas_call(
        functools.partial(_ret_kernel, c=c, cdec=cdec),
        grid=(nb, n),
        in_specs=[tok, tok, tok, tok, rope, rope,
                  _const_spec(dmat.shape), _const_spec(qdec.shape), _const_spec(kdec.shape)],
        out_specs=tok,
        out_shape=jax.ShapeDtypeStruct(rq.shape, BF16),
        scratch_shapes=[pltpu.VMEM((N_HEADS, LANE, LANE), F32)],
        compiler_params=_cparams(("parallel", "arbitrary")),
        name="retention",
    )(rq, rk, rv, rg, cos, sin, jnp.asarray(dmat), jnp.asarray(qdec), jnp.asarray(kdec))


def _outproj_router_kernel(x_ref, oa_ref, ob_ref, oc_ref, wo_ref, g1_ref, nw_ref, sc_ref, sh_ref,
                           wr_ref, br_ref, tri_ref,
                           xo_ref, h_ref, te_ref, tw_ref, rk_ref, cnt_ref, *, tm):
    @pl.when(pl.program_id(0) == 0)
    def _():
        cnt_ref[...] = jnp.zeros_like(cnt_ref)

    mix = (_dot(oa_ref[...], wo_ref[0:HEAD_W, :]) + _dot(ob_ref[...], wo_ref[HEAD_W:2 * HEAD_W, :])
           + _dot(oc_ref[...], wo_ref[2 * HEAD_W:3 * HEAD_W, :]))
    x = x_ref[...] + g1_ref[0] * mix
    xo_ref[...] = x
    y = x * lax.rsqrt(jnp.mean(x * x, axis=-1, keepdims=True) + EPS) * nw_ref[...]
    h = y * (1.0 + sc_ref[0]) + sh_ref[0]
    h_ref[...] = h

    lg = _dot_f32(h, wr_ref[...]) + br_ref[...]
    lane = lax.broadcasted_iota(jnp.int32, (tm, LANE), 1)
    sel_e, sel_v = [], []
    for _ in range(TOP_K):
        m = jnp.max(lg, axis=-1, keepdims=True)
        idx = jnp.min(jnp.where(lg == m, lane, LANE), axis=-1, keepdims=True)
        sel_e.append(idx)
        sel_v.append(m)
        lg = jnp.where(lane == idx, -jnp.inf, lg)
    ex = [jnp.exp(v - sel_v[0]) for v in sel_v]
    den = ex[0] + ex[1] + ex[2] + ex[3]
    hot = [(lane == idx) for idx in sel_e]
    onehot = jnp.zeros((tm, LANE), F32)
    for hk in hot:
        onehot = onehot + jnp.where(hk, 1.0, 0.0)
    before = _dot(tri_ref[...], onehot.astype(BF16)) + cnt_ref[...]
    te = jnp.zeros((tm, LANE), jnp.int32)
    tw = jnp.zeros((tm, LANE), F32)
    rk = jnp.zeros((tm, LANE), jnp.int32)
    for kk in range(TOP_K):
        rank = jnp.sum(jnp.where(hot[kk], before, 0.0), axis=-1, keepdims=True).astype(jnp.int32)
        te = jnp.where(lane == kk, sel_e[kk], te)
        tw = jnp.where(lane == kk, ex[kk] / den, tw)
        rk = jnp.where(lane == kk, rank, rk)
    te_ref[...] = te
    tw_ref[...] = tw
    rk_ref[...] = rk
    cnt_ref[...] = cnt_ref[...] + jnp.sum(onehot, axis=0, keepdims=True)


def _outproj_router_call(x, oa, ob, oc, wo, g1, nw, sc, sh, wr, br, seq):
    t, d = x.shape
    tm = min(256, seq)
    per_b = seq // tm
    tri = np.tril(np.ones((tm, tm), np.float32), -1)
    row = lambda w: pl.BlockSpec((tm, w), lambda i: (i, 0))
    const = lambda shape: pl.BlockSpec(shape, lambda i: (0,) * len(shape))
    perb = pl.BlockSpec((1, 1, d), lambda i: (i // per_b, 0, 0))
    return pl.pallas_call(
        functools.partial(_outproj_router_kernel, tm=tm),
        grid=(t // tm,),
        in_specs=[row(d), row(HEAD_W), row(HEAD_W), row(HEAD_W), const(wo.shape), perb,
                  const((1, d)), perb, perb, const(wr.shape), const(br.shape), const(tri.shape)],
        out_specs=[row(d), row(d), row(LANE), row(LANE), row(LANE), const((1, LANE))],
        out_shape=[jax.ShapeDtypeStruct((t, d), F32), jax.ShapeDtypeStruct((t, d), F32),
                   jax.ShapeDtypeStruct((t, LANE), jnp.int32), jax.ShapeDtypeStruct((t, LANE), F32),
                   jax.ShapeDtypeStruct((t, LANE), jnp.int32), jax.ShapeDtypeStruct((1, LANE), F32)],
        compiler_params=_cparams(("arbitrary",)),
        name="outproj_norm2_router",
    )(x, oa, ob, oc, wo, g1, nw.reshape(1, d), sc, sh, wr, br, jnp.asarray(tri, BF16))


def _expert_kernel(be_ref, nu_ref, tok_ref, h_ref, wgu_ref, bgu_ref, wdn_ref, bdn_ref, o_ref,
                   wgu_bf, wdn_bf, xa, xb, sem, *, d_ff):
    i = pl.program_id(0)
    nu = nu_ref[0]
    live = i < nu

    def gather_start(block, dst, s):
        base = block * MOE_BLOCK
        for r in range(MOE_BLOCK):
            pltpu.make_async_copy(h_ref.at[pl.ds(tok_ref[base + r], 1)], dst.at[pl.ds(r, 1)], s).start()

    def gather_wait(dst, s):
        pltpu.make_async_copy(h_ref.at[pl.ds(0, MOE_BLOCK)], dst, s).wait()

    @pl.when(i == 0)
    def _():
        gather_start(0, xa, sem.at[0])

    @pl.when(jnp.logical_and(live, jnp.logical_or(i == 0, be_ref[i] != be_ref[jnp.maximum(i - 1, 0)])))
    def _():
        wgu_bf[...] = wgu_ref[0, 0].astype(BF16)
        wdn_bf[...] = wdn_ref[0, 0].astype(BF16)

    def block(cur, nxt, s_cur, s_nxt):
        gather_wait(cur, s_cur)
        x = cur[...].astype(BF16)
        gather_start(jnp.minimum(i + 1, nu - 1), nxt, s_nxt)
        gu = _dot(x, wgu_bf[...]) + bgu_ref[0, 0]
        gate = jnp.minimum(gu[:, :d_ff], SWIGLU_LIMIT)
        up = jnp.clip(gu[:, d_ff:], -SWIGLU_LIMIT, SWIGLU_LIMIT)
        act = (up + 1.0) * gate * _sigmoid(SWIGLU_ALPHA * gate)
        o_ref[...] = _dot(act.astype(BF16), wdn_bf[...]) + bdn_ref[0, 0]

        @pl.when(i == nu - 1)
        def _():
            gather_wait(nxt, s_nxt)

    @pl.when(jnp.logical_and(live, i % 2 == 0))
    def _():
        block(xa, xb, sem.at[0], sem.at[1])

    @pl.when(jnp.logical_and(live, i % 2 == 1))
    def _():
        block(xb, xa, sem.at[1], sem.at[0])

    @pl.when(jnp.logical_not(live))
    def _():
        o_ref[...] = jnp.zeros_like(o_ref)


def _expert_call(h, slot_tok, block_e, n_used, wgu, bgu, wdn, bdn, layer):
    t, d = h.shape
    n_slots = slot_tok.shape[0]
    n_blocks = n_slots // MOE_BLOCK
    depth, ne, _, f2 = wgu.shape
    d_ff = f2 // 2
    exp4 = lambda i, be, nu, tk: (layer, be[jnp.minimum(i, nu[0] - 1)], 0, 0)
    grid_spec = pltpu.PrefetchScalarGridSpec(
        num_scalar_prefetch=3,
        grid=(n_blocks,),
        in_specs=[pl.BlockSpec(memory_space=pl.ANY),
                  pl.BlockSpec((1, 1, d, f2), exp4),
                  pl.BlockSpec((1, 1, 1, f2), exp4),
                  pl.BlockSpec((1, 1, d_ff, d), exp4),
                  pl.BlockSpec((1, 1, 1, d), exp4)],
        out_specs=pl.BlockSpec((MOE_BLOCK, d), lambda i, be, nu, tk: (i, 0)),
        scratch_shapes=[pltpu.VMEM((d, f2), BF16), pltpu.VMEM((d_ff, d), BF16),
                        pltpu.VMEM((MOE_BLOCK, d), F32), pltpu.VMEM((MOE_BLOCK, d), F32),
                        pltpu.SemaphoreType.DMA((2,))],
    )
    return pl.pallas_call(
        functools.partial(_expert_kernel, d_ff=d_ff),
        grid_spec=grid_spec,
        out_shape=jax.ShapeDtypeStruct((n_slots, d), F32),
        compiler_params=_cparams(("arbitrary",)),
        name="moe_experts",
    )(block_e, n_used, slot_tok, h, wgu, bgu.reshape(depth, ne, 1, f2), wdn, bdn.reshape(depth, ne, 1, d))


def _combine_kernel(dest_ref, yb_ref, w_ref, x_ref, g2_ref, o_ref, rows_ref, sem, *, tm):
    def issue(r, carry):
        for kk in range(TOP_K):
            pltpu.make_async_copy(yb_ref.at[pl.ds(dest_ref[r * TOP_K + kk], 1)],
                                  rows_ref.at[pl.ds(kk * tm + r, 1)], sem).start()
        return carry

    lax.fori_loop(0, tm, issue, 0, unroll=8)
    pltpu.make_async_copy(yb_ref.at[pl.ds(0, TOP_K * tm)], rows_ref, sem).wait()
    w = w_ref[...]
    y = w[:, 0:1] * rows_ref[0:tm, :]
    for kk in range(1, TOP_K):
        y = y + w[:, kk:kk + 1] * rows_ref[kk * tm:(kk + 1) * tm, :]
    o_ref[...] = x_ref[...] + g2_ref[0] * y


def _combine_call(yb, dest_flat, tw, x, g2, seq):
    t, d = x.shape
    tm = min(256, seq)
    per_b = seq // tm
    return pl.pallas_call(
        functools.partial(_combine_kernel, tm=tm),
        grid=(t // tm,),
        in_specs=[pl.BlockSpec((tm * TOP_K,), lambda i: (i,), memory_space=pltpu.SMEM),
                  pl.BlockSpec(memory_space=pl.ANY),
                  pl.BlockSpec((tm, LANE), lambda i: (i, 0)),
                  pl.BlockSpec((tm, d), lambda i: (i, 0)),
                  pl.BlockSpec((1, 1, d), lambda i: (i // per_b, 0, 0))],
        out_specs=pl.BlockSpec((tm, d), lambda i: (i, 0)),
        out_shape=jax.ShapeDtypeStruct((t, d), F32),
        scratch_shapes=[pltpu.VMEM((TOP_K * tm, d), F32), pltpu.SemaphoreType.DMA(())],
        compiler_params=_cparams(("arbitrary",)),
        name="moe_combine",
    )(dest_flat, yb, tw, x, g2)


def _final_norm_kernel(x_ref, w_ref, o_ref):
    x = x_ref[...]
    o_ref[...] = x * lax.rsqrt(jnp.mean(x * x, axis=-1, keepdims=True) + EPS) * w_ref[...]


def _final_norm_call(x, w):
    t, d = x.shape
    tm = min(512, t)
    return pl.pallas_call(
        _final_norm_kernel,
        grid=(t // tm,),
        in_specs=[pl.BlockSpec((tm, d), lambda i: (i, 0)), pl.BlockSpec((1, d), lambda i: (0, 0))],
        out_specs=pl.BlockSpec((tm, d), lambda i: (i, 0)),
        out_shape=jax.ShapeDtypeStruct((t, d), F32),
        compiler_params=_cparams(("parallel",)),
        name="final_norm",
    )(x, w.reshape(1, d))


def _pad_heads_cols(w, hd):
    r = w.shape[0]
    w = w.reshape(r, N_HEADS, hd)
    return jnp.pad(w, ((0, 0), (0, 0), (0, LANE - hd))).reshape(r, HEAD_W)


def _pad_heads_rows(w, hd):
    return _pad_heads_cols(w.T, hd).T


IN_PARTS = (("gq", GLA_DK), ("gk", GLA_DK), ("gv", GLA_DV), ("gg", GLA_DV), ("ga", None),
            ("hq", HG_DK), ("hf", HG_DK), ("hi", HG_DV), ("hg", HG_DV),
            ("rq", RET_DK), ("rk", RET_DK), ("rv", RET_DV), ("rg", RET_DV))
F32_PARTS = ("ga", "hf")


def _layout_w_in(w_in):
    cols, off = [], 0
    for name, hd in IN_PARTS:
        if hd is None:
            part = jnp.pad(w_in[:, off:off + GLA_RANK], ((0, 0), (0, LANE - GLA_RANK)))
            off += GLA_RANK
        else:
            part = _pad_heads_cols(w_in[:, off:off + N_HEADS * hd], hd)
            off += N_HEADS * hd
        cols.append(part)
    return jnp.concatenate(cols, axis=1).astype(BF16)


def kernel(x, c, positions, w_mod, b_mod, norm1_w, w_in, gla_wa2, gla_ba2, hg_lb, gla_norm_w, hg_norm_w,
           w_out, norm2_w, w_r, b_r, w_gu, b_gu, w_dn, b_dn, final_norm_w):
    nb, seq, d = x.shape
    t = nb * seq
    depth = w_mod.shape[0]
    widths = tuple(LANE if hd is None else HEAD_W for _, hd in IN_PARTS)
    dtypes = tuple(F32 if name in F32_PARTS else BF16 for name, _ in IN_PARTS)

    mod = _mod_call(c, w_mod, b_mod)
    cos, sin = _rope_call(positions)
    lb_all = _lb_call(hg_lb)

    n_assign = t * TOP_K
    n_blocks = (n_assign + MOE_BLOCK - 1) // MOE_BLOCK + N_EXPERTS
    n_slots = n_blocks * MOE_BLOCK

    xf = x.reshape(t, d)
    for layer in range(depth):
        sh1, sc1, g1, sh2, sc2, g2 = [m.reshape(nb, 1, d) for m in jnp.split(mod[layer], 6, axis=-1)]

        parts = _inproj_call(xf, norm1_w[layer], sc1, sh1, _layout_w_in(w_in[layer]), widths, dtypes, seq)
        gq, gk, gv, gg, ga, hq, hf, hi, hg, rq, rk, rv, rg = parts
        wa2p = jnp.pad(_pad_heads_cols(gla_wa2[layer], GLA_DK), ((0, LANE - GLA_RANK), (0, 0)))
        ba2p = _pad_heads_cols(gla_ba2[layer].reshape(1, -1), GLA_DK)
        o_gla = _gla_call(gq, gk, gv, gg, ga, wa2p, ba2p,
                          _pad_heads_cols(gla_norm_w[layer].reshape(1, -1), GLA_DV), nb, seq)
        o_hg = _hg_call(hq, hf, hi, hg, _pad_heads_cols(lb_all[layer].reshape(1, -1), HG_DK),
                        _pad_heads_cols(hg_norm_w[layer].reshape(1, -1), HG_DV), nb, seq)
        o_ret = _ret_call(rq, rk, rv, rg, cos, sin, nb, seq)

        wo = w_out[layer]
        n_gla, n_hg = N_HEADS * GLA_DV, N_HEADS * HG_DV
        wo_p = jnp.concatenate([_pad_heads_rows(wo[:n_gla], GLA_DV),
                                _pad_heads_rows(wo[n_gla:n_gla + n_hg], HG_DV),
                                _pad_heads_rows(wo[n_gla + n_hg:], RET_DV)], axis=0).astype(BF16)
        wr_p = jnp.pad(w_r[layer], ((0, 0), (0, LANE - N_EXPERTS)))
        br_p = jnp.pad(b_r[layer].reshape(1, -1), ((0, 0), (0, LANE - N_EXPERTS)), constant_values=NEG_BIG)
        xf, h2, te, tw, rk_, cnt = _outproj_router_call(xf, o_gla, o_hg, o_ret, wo_p, g1, norm2_w[layer],
                                                        sc2, sh2, wr_p, br_p, seq)

        counts = cnt[0, :N_EXPERTS].astype(jnp.int32)
        padded = (counts + MOE_BLOCK - 1) // MOE_BLOCK * MOE_BLOCK
        pends = jnp.cumsum(padded)
        pstarts = pends - padded
        top_e = te[:, :TOP_K]
        dest = (pstarts[top_e] + rk_[:, :TOP_K]).reshape(-1).astype(jnp.int32)
        block_start = jnp.arange(n_blocks, dtype=jnp.int32) * MOE_BLOCK
        block_e = jnp.minimum(jnp.sum((pends[None, :] <= block_start[:, None]).astype(jnp.int32), axis=1),
                              N_EXPERTS - 1)
        n_used = (pends[-1:] // MOE_BLOCK).astype(jnp.int32)
        slot_tok = jnp.zeros((n_slots,), jnp.int32).at[dest].set(
            jnp.arange(n_assign, dtype=jnp.int32) // TOP_K, unique_indices=True)

        yb = _expert_call(h2, slot_tok, block_e, n_used, w_gu, b_gu, w_dn, b_dn, layer)
        xf = _combine_call(yb, dest, tw, xf, g2, seq)

    return _final_norm_call(xf, final_norm_w).reshape(nb, seq, d)
```

```python
import functools

import numpy as np
import jax
import jax.numpy as jnp
from jax import lax
from jax.experimental import pallas as pl
from jax.experimental.pallas import tpu as pltpu

F32 = jnp.float32
BF16 = jnp.bfloat16

N_HEADS = 4
GLA_DK, GLA_DV, GLA_RANK, GLA_TAU = 48, 96, 16, 16.0
HG_DK, HG_DV = 64, 64
RET_DK, RET_DV = 48, 96
ROPE_BASE = 10000.0
N_EXPERTS, TOP_K = 32, 4
SWIGLU_LIMIT, SWIGLU_ALPHA = 7.0, 1.702
MOE_BLOCK = 512
EPS = 1e-6

LANE = 128
HALF = LANE // 2
N_PAIRS = N_HEADS // 2
HEAD_W = N_HEADS * LANE
PAIR_W = N_PAIRS * LANE
VMEM_LIMIT = 56 * 1024 * 1024
NEG_BIG = -1e30


def _cparams(sem):
    return pltpu.CompilerParams(dimension_semantics=sem, vmem_limit_bytes=VMEM_LIMIT)


def _dot(a, b):
    return jnp.dot(a, b, preferred_element_type=F32)


def _dot_nt(a, b):
    return lax.dot_general(a, b, (((1,), (1,)), ((), ())), preferred_element_type=F32)


def _dot_tn(a, b):
    return lax.dot_general(a, b, (((0,), (0,)), ((), ())), preferred_element_type=F32)


def _split2(a):
    hi = a.astype(BF16)
    return hi, (a - hi.astype(F32)).astype(BF16)


def _dot_f32(a, b):
    a_hi, a_lo = _split2(a)
    b_hi, b_lo = _split2(b)
    return _dot(a_hi, b_hi) + _dot(a_hi, b_lo) + _dot(a_lo, b_hi)


def _sigmoid(x):
    return 1.0 / (1.0 + jnp.exp(-x))


def _log_sigmoid(x):
    return jnp.minimum(x, 0.0) - jnp.log1p(jnp.exp(-jnp.abs(x)))


def _mod_kernel(c_ref, w_ref, b_ref, o_ref):
    c = c_ref[...]
    o_ref[0] = _dot_f32(c * _sigmoid(c), w_ref[0]) + b_ref[0]


def _mod_call(c, w_mod, b_mod):
    depth, d, d6 = w_mod.shape
    nb = c.shape[0]
    rows = 8
    c_pad = jnp.zeros((rows, d), F32).at[:nb].set(c)
    out = pl.pallas_call(
        _mod_kernel,
        grid=(depth, d6 // d),
        in_specs=[
            pl.BlockSpec((rows, d), lambda l, j: (0, 0)),
            pl.BlockSpec((1, d, d), lambda l, j: (l, 0, j)),
            pl.BlockSpec((1, 1, d), lambda l, j: (l, 0, j)),
        ],
        out_specs=pl.BlockSpec((1, rows, d), lambda l, j: (l, 0, j)),
        out_shape=jax.ShapeDtypeStruct((depth, rows, d6), F32),
        compiler_params=_cparams(("parallel", "parallel")),
        name="mod",
    )(c_pad, w_mod, b_mod.reshape(depth, 1, d6))
    return out[:, :nb]


def _rope_kernel(pos_ref, freq_ref, cos_ref, sin_ref):
    ang = pos_ref[0] * freq_ref[...]
    cos_ref[0] = jnp.cos(ang)
    sin_ref[0] = jnp.sin(ang)


def _rope_call(positions):
    nb, l = positions.shape
    half = RET_DK // 2
    inv = (ROPE_BASE ** (-np.arange(half, dtype=np.float32) / half)).astype(np.float32)
    freq = np.zeros((1, LANE), np.float32)
    for base in (0, HALF):
        freq[0, base:base + half] = inv
        freq[0, base + half:base + 2 * half] = inv
    tl = min(l, 512)
    pos = positions.astype(F32).reshape(nb, l, 1)
    shp = jax.ShapeDtypeStruct((nb, l, LANE), F32)
    return pl.pallas_call(
        _rope_kernel,
        grid=(nb, l // tl),
        in_specs=[pl.BlockSpec((1, tl, 1), lambda b, i: (b, i, 0)),
                  pl.BlockSpec((1, LANE), lambda b, i: (0, 0))],
        out_specs=[pl.BlockSpec((1, tl, LANE), lambda b, i: (b, i, 0))] * 2,
        out_shape=[shp, shp],
        compiler_params=_cparams(("parallel", "parallel")),
        name="rope_tables",
    )(pos, jnp.asarray(freq))


def _lb_kernel(p_ref, o_ref):
    p = p_ref[...]
    depth = p.shape[0]
    m = jnp.max(p, axis=0, keepdims=True)
    e = jnp.exp(p - m)
    sm = e / jnp.sum(e, axis=0, keepdims=True)
    acc = jnp.zeros_like(sm[0:1])
    for i in range(depth):
        if i > 0:
            acc = acc + sm[i:i + 1]
        o_ref[i:i + 1, :] = acc


def _lb_call(hg_lb):
    return pl.pallas_call(
        _lb_kernel,
        out_shape=jax.ShapeDtypeStruct(hg_lb.shape, F32),
        name="hg_lower_bounds",
    )(hg_lb.astype(F32))


def _inproj_kernel(x_ref, nw_ref, sc_ref, sh_ref, w_ref, *out_refs, widths):
    x = x_ref[...]
    y = x * lax.rsqrt(jnp.mean(x * x, axis=-1, keepdims=True) + EPS) * nw_ref[...]
    hb = (y * (1.0 + sc_ref[0]) + sh_ref[0]).astype(BF16)
    off = 0
    for o_ref, wd in zip(out_refs, widths):
        o_ref[...] = _dot(hb, w_ref[:, off:off + wd]).astype(o_ref.dtype)
        off += wd


def _inproj_call(x, norm_w, sc, sh, w_cat, widths, dtypes, seq):
    t, d = x.shape
    tm = min(512, seq)
    per_b = seq // tm
    return pl.pallas_call(
        functools.partial(_inproj_kernel, widths=widths),
        grid=(t // tm,),
        in_specs=[
            pl.BlockSpec((tm, d), lambda i: (i, 0)),
            pl.BlockSpec((1, d), lambda i: (0, 0)),
            pl.BlockSpec((1, 1, d), lambda i: (i // per_b, 0, 0)),
            pl.BlockSpec((1, 1, d), lambda i: (i // per_b, 0, 0)),
            pl.BlockSpec(w_cat.shape, lambda i: (0, 0)),
        ],
        out_specs=[pl.BlockSpec((tm, wd), lambda i: (i, 0)) for wd in widths],
        out_shape=[jax.ShapeDtypeStruct((t, wd), dt) for wd, dt in zip(widths, dtypes)],
        compiler_params=_cparams(("parallel",)),
        name="norm1_inproj",
    )(x, norm_w.reshape(1, d), sc, sh, w_cat)


def _level_tables(c):
    nl = int(np.log2(c))
    assert 1 << nl == c
    idx = np.arange(c)
    mats, masks = [], [np.eye(c, dtype=np.float32)]
    t = idx[None, :]
    i = idx[:, None]
    for lvl in range(nl):
        h = c >> (lvl + 1)
        blk, pos = idx // (2 * h), idx % (2 * h)
        m = (blk * 2 * h + h - 1)[:, None]
        right = (pos >= h)[:, None]
        a = np.where(right, (t > m) & (t <= i), (t > i) & (t <= m))
        mats.append(a.astype(np.float32))
        same = blk[:, None] == blk[None, :]
        masks.append((same & right & (pos < h)[None, :]).astype(np.float32))
    mats.append((t <= i).astype(np.float32))
    mats.append((t > i).astype(np.float32))
    return np.concatenate(mats, 0), np.stack(masks, 0), nl


def _pair_decays(g, mall_ref):
    gs = _dot(mall_ref[...], jnp.concatenate(_split2(g), axis=1))
    return jnp.exp(gs[:, :LANE] + gs[:, LANE:])


def _pair_keys(k, e, c, nl):
    levels = [(k * e[lvl * c:(lvl + 1) * c]).astype(BF16) for lvl in range(nl)]
    return k.astype(BF16), levels, (k * e[(nl + 1) * c:(nl + 2) * c]).astype(BF16)


def _gated_head(q, kb, klv, e, v, st, masks_ref, c, nl):
    scores = masks_ref[0] * _dot_nt(q.astype(BF16), kb)
    for lvl in range(nl):
        scores = scores + masks_ref[lvl + 1] * _dot_nt((q * e[lvl * c:(lvl + 1) * c]).astype(BF16), klv[lvl])
    q_in = (q * e[nl * c:(nl + 1) * c]).astype(BF16)
    return _dot(scores.astype(BF16), v) + _dot_nt(q_in, st.astype(BF16))


def _gla_kernel(q_ref, k_ref, v_ref, gg_ref, ga_ref, wa2_ref, ba2_ref, nw_ref, mall_ref, masks_ref,
                o_ref, st_ref, *, c, nl):
    @pl.when(pl.program_id(0) == 0)
    def _():
        st_ref[...] = jnp.zeros_like(st_ref)

    for b in range(q_ref.shape[0]):
        log_alpha = _log_sigmoid(_dot_f32(ga_ref[b], wa2_ref[...]) + ba2_ref[...]) * (1.0 / GLA_TAU)
        for pr in range(N_PAIRS):
            psl = slice(pr * LANE, (pr + 1) * LANE)
            e = _pair_decays(log_alpha[:, psl], mall_ref)
            kb, klv, k_end = _pair_keys(k_ref[b, :, psl].astype(F32), e, c, nl)
            dec = e[(nl + 1) * c - 1:(nl + 1) * c, :]
            for hd in (2 * pr, 2 * pr + 1):
                sl = slice(hd * LANE, (hd + 1) * LANE)
                q = q_ref[b, :, sl].astype(F32) * (GLA_DK ** -0.5)
                v = v_ref[b, :, sl]
                st = st_ref[b, hd]
                o = _gated_head(q, kb, klv, e, v, st, masks_ref, c, nl)
                st_ref[b, hd] = dec * st + _dot_tn(v, k_end)
                y = o * lax.rsqrt(jnp.sum(o * o, axis=-1, keepdims=True) * (1.0 / GLA_DV) + EPS) * nw_ref[:, sl]
                gate = gg_ref[b, :, sl].astype(F32)
                o_ref[b, :, sl] = (y * gate * _sigmoid(gate)).astype(o_ref.dtype)


def _hg_kernel(q_ref, f_ref, v_ref, og_ref, lb_ref, nw_ref, mall_ref, masks_ref,
               o_ref, st_ref, *, c, nl):
    @pl.when(pl.program_id(0) == 0)
    def _():
        st_ref[...] = jnp.zeros_like(st_ref)

    low = lax.broadcasted_iota(jnp.int32, (c, LANE), 1) < HALF
    for b in range(q_ref.shape[0]):
        for pr in range(N_PAIRS):
            psl = slice(pr * LANE, (pr + 1) * LANE)
            lb = lb_ref[:, psl]
            hf = f_ref[b, :, psl]
            la = jnp.log(lb)
            lc = jnp.log1p(-lb) + _log_sigmoid(hf)
            log_f = jnp.maximum(la, lc) + jnp.log1p(jnp.exp(-jnp.abs(la - lc)))
            e = _pair_decays(log_f, mall_ref)
            kb, klv, k_end = _pair_keys((1.0 - lb) * _sigmoid(-hf), e, c, nl)
            v = v_ref[b, :, psl]
            st = st_ref[b, pr]
            o_even = _gated_head(q_ref[b, :, (2 * pr) * LANE:(2 * pr + 1) * LANE].astype(F32),
                                 kb, klv, e, v, st, masks_ref, c, nl)
            o_odd = _gated_head(q_ref[b, :, (2 * pr + 1) * LANE:(2 * pr + 2) * LANE].astype(F32),
                                kb, klv, e, v, st, masks_ref, c, nl)
            o = jnp.where(low, o_even, o_odd)
            st_ref[b, pr] = e[(nl + 1) * c - 1:(nl + 1) * c, :] * st + _dot_tn(v, k_end)
            sq = o * o
            ms = jnp.where(low, jnp.sum(jnp.where(low, sq, 0.0), axis=-1, keepdims=True),
                           jnp.sum(jnp.where(low, 0.0, sq), axis=-1, keepdims=True)) * (1.0 / HG_DV)
            y = o * lax.rsqrt(ms + EPS) * nw_ref[:, psl]
            o_ref[b, :, psl] = (y * _sigmoid(og_ref[b, :, psl].astype(F32))).astype(o_ref.dtype)


def _tok_spec(nb, c, width):
    return pl.BlockSpec((nb, c, width), lambda i: (0, i, 0))


def _const_spec(shape):
    nd = len(shape)
    return pl.BlockSpec(shape, lambda i: (0,) * nd)


def _seq_view(a, nb, seq):
    return a.reshape(nb, seq, a.shape[-1])


def _gla_call(gq, gk, gv, gg, ga, wa2p, ba2p, nwp, nb, seq):
    c = min(128, seq)
    n = seq // c
    mall, masks, nl = _level_tables(c)
    sv = lambda a: _seq_view(a, nb, seq)
    return pl.pallas_call(
        functools.partial(_gla_kernel, c=c, nl=nl),
        grid=(n,),
        in_specs=[_tok_spec(nb, c, HEAD_W), _tok_spec(nb, c, PAIR_W), _tok_spec(nb, c, HEAD_W),
                  _tok_spec(nb, c, HEAD_W), _tok_spec(nb, c, LANE),
                  _const_spec(wa2p.shape), _const_spec(ba2p.shape), _const_spec(nwp.shape),
                  _const_spec(mall.shape), _const_spec(masks.shape)],
        out_specs=_tok_spec(nb, c, HEAD_W),
        out_shape=jax.ShapeDtypeStruct((nb, seq, HEAD_W), BF16),
        scratch_shapes=[pltpu.VMEM((nb, N_HEADS, LANE, LANE), F32)],
        compiler_params=_cparams(("arbitrary",)),
        name="gla_recurrence",
    )(sv(gq), sv(gk), sv(gv), sv(gg), sv(ga), wa2p, ba2p, nwp,
      jnp.asarray(mall, BF16), jnp.asarray(masks)).reshape(nb * seq, HEAD_W)


def _hg_call(hq, hf, hi, hg, lbp, nwp, nb, seq):
    c = min(128, seq)
    n = seq // c
    mall, masks, nl = _level_tables(c)
    sv = lambda a: _seq_view(a, nb, seq)
    return pl.pallas_call(
        functools.partial(_hg_kernel, c=c, nl=nl),
        grid=(n,),
        in_specs=[_tok_spec(nb, c, HEAD_W), _tok_spec(nb, c, PAIR_W), _tok_spec(nb, c, PAIR_W),
                  _tok_spec(nb, c, PAIR_W),
                  _const_spec(lbp.shape), _const_spec(nwp.shape),
                  _const_spec(mall.shape), _const_spec(masks.shape)],
        out_specs=_tok_spec(nb, c, PAIR_W),
        out_shape=jax.ShapeDtypeStruct((nb, seq, PAIR_W), BF16),
        scratch_shapes=[pltpu.VMEM((nb, N_PAIRS, LANE, LANE), F32)],
        compiler_params=_cparams(("arbitrary",)),
        name="hgrn2_recurrence",
    )(sv(hq), sv(hf), sv(hi), sv(hg), lbp, nwp,
      jnp.asarray(mall, BF16), jnp.asarray(masks)).reshape(nb * seq, PAIR_W)


def _ret_tables(c):
    hs = np.arange(N_HEADS, dtype=np.float64)
    log_gamma = np.log(1.0 - np.exp2(-5.0 - hs))
    idx = np.arange(c, dtype=np.float64)
    rel = idx[:, None] - idx[None, :]
    dmat = np.where(rel >= 0, np.exp(log_gamma[:, None, None] * np.maximum(rel, 0.0)), 0.0)
    qdec = np.exp(log_gamma[:, None] * (idx + 1.0))
    kdec = np.exp(log_gamma[:, None] * (c - 1.0 - idx))
    cdec = np.exp(log_gamma * c)
    qfull = np.broadcast_to(qdec[:, :, None], (N_HEADS, c, LANE)).astype(np.float32)
    kpair = np.zeros((N_PAIRS, c, LANE), np.float32)
    for hd in range(N_HEADS):
        kpair[hd // 2, :, (hd % 2) * HALF:(hd % 2 + 1) * HALF] = kdec[hd][:, None]
    return dmat.astype(np.float32), qfull, kpair, [float(np.float32(v)) for v in cdec]


def _ret_kernel(q_ref, k_ref, v_ref, og_ref, cos_ref, sin_ref, dmat_ref, qdec_ref, kdec_ref,
                o_ref, st_ref, *, c, cdec):
    @pl.when(pl.program_id(0) == 0)
    def _():
        st_ref[...] = jnp.zeros_like(st_ref)

    half = RET_DK // 2
    lane = lax.broadcasted_iota(jnp.int32, (c, LANE), 1)
    first = (lane & (HALF - 1)) < half

    for b in range(q_ref.shape[0]):
        cos = cos_ref[b]
        sin = sin_ref[b]

        def rotary(t):
            rot = jnp.where(first, -pltpu.roll(t, LANE - half, 1), pltpu.roll(t, half, 1))
            return t * cos + rot * sin

        for pr in range(N_PAIRS):
            psl = slice(pr * LANE, (pr + 1) * LANE)
            k = rotary(k_ref[b, :, psl].astype(F32))
            kb = k.astype(BF16)
            k_end = (k * kdec_ref[pr]).astype(BF16)
            for hd in (2 * pr, 2 * pr + 1):
                sl = slice(hd * LANE, (hd + 1) * LANE)
                q = rotary(q_ref[b, :, sl].astype(F32)) * (RET_DK ** -0.5)
                v = v_ref[b, :, sl]
                st = st_ref[b, hd]
                scores = _dot_nt(q.astype(BF16), kb) * dmat_ref[hd]
                o = _dot(scores.astype(BF16), v) + _dot_nt((q * qdec_ref[hd]).astype(BF16), st.astype(BF16))
                st_ref[b, hd] = cdec[hd] * st + _dot_tn(v, k_end)
                mu = jnp.sum(o, axis=-1, keepdims=True) * (1.0 / RET_DV)
                dlt = jnp.where(lane < RET_DV, o - mu, 0.0)
                var = jnp.sum(dlt * dlt, axis=-1, keepdims=True) * (1.0 / RET_DV)
                gate = og_ref[b, :, sl].astype(F32)
                o_ref[b, :, sl] = (dlt * lax.rsqrt(var + EPS) * gate * _sigmoid(gate)).astype(o_ref.dtype)


def _ret_call(rq, rk, rv, rg, cos, sin, nb, seq):
    c = min(128, seq)
    n = seq // c
    dmat, qdec, kdec, cdec = _ret_tables(c)
    sv = lambda a: _seq_view(a, nb, seq)
    return pl.pallas_call(
        functools.partial(_ret_kernel, c=c, cdec=cdec),
        grid=(n,),
        in_specs=[_tok_spec(nb, c, HEAD_W), _tok_spec(nb, c, PAIR_W), _tok_spec(nb, c, HEAD_W),
                  _tok_spec(nb, c, HEAD_W), _tok_spec(nb, c, LANE), _tok_spec(nb, c, LANE),
                  _const_spec(dmat.shape), _const_spec(qdec.shape), _const_spec(kdec.shape)],
        out_specs=_tok_spec(nb, c, HEAD_W),
        out_shape=jax.ShapeDtypeStruct((nb, seq, HEAD_W), BF16),
        scratch_shapes=[pltpu.VMEM((nb, N_HEADS, LANE, LANE), F32)],
        compiler_params=_cparams(("arbitrary",)),
        name="retention",
    )(sv(rq), sv(rk), sv(rv), sv(rg), cos, sin,
      jnp.asarray(dmat), jnp.asarray(qdec), jnp.asarray(kdec)).reshape(nb * seq, HEAD_W)


def _outproj_router_kernel(x_ref, oa_ref, ob_ref, oc_ref, wo_ref, g1_ref, nw_ref, sc_ref, sh_ref,
                           wr_ref, br_ref, tri_ref,
                           xo_ref, h_ref, te_ref, tw_ref, rk_ref, cnt_ref, *, tm):
    @pl.when(pl.program_id(0) == 0)
    def _():
        cnt_ref[...] = jnp.zeros_like(cnt_ref)

    wa, wb = oa_ref.shape[1], ob_ref.shape[1]
    mix = (_dot(oa_ref[...], wo_ref[0:wa, :]) + _dot(ob_ref[...], wo_ref[wa:wa + wb, :])
           + _dot(oc_ref[...], wo_ref[wa + wb:, :]))
    x = x_ref[...] + g1_ref[0] * mix
    xo_ref[...] = x
    y = x * lax.rsqrt(jnp.mean(x * x, axis=-1, keepdims=True) + EPS) * nw_ref[...]
    h = y * (1.0 + sc_ref[0]) + sh_ref[0]
    h_ref[...] = h

    lg = _dot_f32(h, wr_ref[...]) + br_ref[...]
    lane = lax.broadcasted_iota(jnp.int32, (tm, LANE), 1)
    sel_e, sel_v = [], []
    for _ in range(TOP_K):
        m = jnp.max(lg, axis=-1, keepdims=True)
        idx = jnp.min(jnp.where(lg == m, lane, LANE), axis=-1, keepdims=True)
        sel_e.append(idx)
        sel_v.append(m)
        lg = jnp.where(lane == idx, -jnp.inf, lg)
    ex = [jnp.exp(v - sel_v[0]) for v in sel_v]
    den = ex[0] + ex[1] + ex[2] + ex[3]
    hot = [(lane == idx) for idx in sel_e]
    onehot = jnp.zeros((tm, LANE), F32)
    for hk in hot:
        onehot = onehot + jnp.where(hk, 1.0, 0.0)
    before = _dot(tri_ref[...], onehot.astype(BF16)) + cnt_ref[...]
    te = jnp.zeros((tm, LANE), jnp.int32)
    tw = jnp.zeros((tm, LANE), F32)
    rk = jnp.zeros((tm, LANE), jnp.int32)
    for kk in range(TOP_K):
        rank = jnp.sum(jnp.where(hot[kk], before, 0.0), axis=-1, keepdims=True).astype(jnp.int32)
        te = jnp.where(lane == kk, sel_e[kk], te)
        tw = jnp.where(lane == kk, ex[kk] / den, tw)
        rk = jnp.where(lane == kk, rank, rk)
    te_ref[...] = te
    tw_ref[...] = tw
    rk_ref[...] = rk
    cnt_ref[...] = cnt_ref[...] + jnp.sum(onehot, axis=0, keepdims=True)


def _outproj_router_call(x, oa, ob, oc, wo, g1, nw, sc, sh, wr, br, seq):
    t, d = x.shape
    tm = min(512, seq)
    per_b = seq // tm
    tri = np.tril(np.ones((tm, tm), np.float32), -1)
    row = lambda w: pl.BlockSpec((tm, w), lambda i: (i, 0))
    const = lambda shape: pl.BlockSpec(shape, lambda i: (0,) * len(shape))
    perb = pl.BlockSpec((1, 1, d), lambda i: (i // per_b, 0, 0))
    return pl.pallas_call(
        functools.partial(_outproj_router_kernel, tm=tm),
        grid=(t // tm,),
        in_specs=[row(d), row(oa.shape[1]), row(ob.shape[1]), row(oc.shape[1]), const(wo.shape), perb,
                  const((1, d)), perb, perb, const(wr.shape), const(br.shape), const(tri.shape)],
        out_specs=[row(d), row(d), row(LANE), row(LANE), row(LANE), const((1, LANE))],
        out_shape=[jax.ShapeDtypeStruct((t, d), F32), jax.ShapeDtypeStruct((t, d), F32),
                   jax.ShapeDtypeStruct((t, LANE), jnp.int32), jax.ShapeDtypeStruct((t, LANE), F32),
                   jax.ShapeDtypeStruct((t, LANE), jnp.int32), jax.ShapeDtypeStruct((1, LANE), F32)],
        compiler_params=_cparams(("arbitrary",)),
        name="outproj_norm2_router",
    )(x, oa, ob, oc, wo, g1, nw.reshape(1, d), sc, sh, wr, br, jnp.asarray(tri, BF16))


def _dispatch_kernel(dest_ref, h_ref, buf_in_ref, buf_ref, sem, *, tm):
    del buf_in_ref

    def issue(r, carry):
        for kk in range(TOP_K):
            pltpu.make_async_copy(h_ref.at[pl.ds(r, 1)],
                                  buf_ref.at[pl.ds(dest_ref[r * TOP_K + kk], 1)], sem).start()
        return carry

    lax.fori_loop(0, tm, issue, 0, unroll=8)
    for _ in range(TOP_K):
        pltpu.make_async_copy(h_ref, buf_ref.at[pl.ds(0, tm)], sem).wait()


def _dispatch_call(h, dest_flat, n_slots):
    t, d = h.shape
    tm = min(256, t)
    buf0 = jnp.zeros((n_slots, d), F32)
    return pl.pallas_call(
        functools.partial(_dispatch_kernel, tm=tm),
        grid=(t // tm,),
        in_specs=[pl.BlockSpec((tm * TOP_K,), lambda i: (i,), memory_space=pltpu.SMEM),
                  pl.BlockSpec((tm, d), lambda i: (i, 0)),
                  pl.BlockSpec(memory_space=pl.ANY)],
        out_specs=pl.BlockSpec(memory_space=pl.ANY),
        out_shape=jax.ShapeDtypeStruct((n_slots, d), F32),
        scratch_shapes=[pltpu.SemaphoreType.DMA(())],
        input_output_aliases={2: 0},
        compiler_params=_cparams(("arbitrary",)),
        name="moe_dispatch",
    )(dest_flat, h, buf0)


def _expert_kernel(be_ref, nu_ref, x_ref, wgu_ref, bgu_ref, wdn_ref, bdn_ref, o_ref, wgu_bf, wdn_bf, *, d_ff):
    i = pl.program_id(0)
    live = i < nu_ref[0]

    @pl.when(jnp.logical_and(live, jnp.logical_or(i == 0, be_ref[i] != be_ref[jnp.maximum(i - 1, 0)])))
    def _():
        wgu_bf[...] = wgu_ref[0, 0].astype(BF16)
        wdn_bf[...] = wdn_ref[0, 0].astype(BF16)

    @pl.when(live)
    def _():
        gu = _dot(x_ref[...].astype(BF16), wgu_bf[...]) + bgu_ref[0, 0]
        gate = jnp.minimum(gu[:, :d_ff], SWIGLU_LIMIT)
        up = jnp.clip(gu[:, d_ff:], -SWIGLU_LIMIT, SWIGLU_LIMIT)
        act = (up + 1.0) * gate * _sigmoid(SWIGLU_ALPHA * gate)
        o_ref[...] = _dot(act.astype(BF16), wdn_bf[...]) + bdn_ref[0, 0]

    @pl.when(jnp.logical_not(live))
    def _():
        o_ref[...] = jnp.zeros_like(o_ref)


def _expert_call(buf, block_e, n_used, wgu, bgu, wdn, bdn, layer):
    n_slots, d = buf.shape
    n_blocks = n_slots // MOE_BLOCK
    depth, ne, _, f2 = wgu.shape
    d_ff = f2 // 2
    blk = lambda i, be, nu: (jnp.minimum(i, nu[0] - 1), 0)
    exp4 = lambda i, be, nu: (layer, be[jnp.minimum(i, nu[0] - 1)], 0, 0)
    grid_spec = pltpu.PrefetchScalarGridSpec(
        num_scalar_prefetch=2,
        grid=(n_blocks,),
        in_specs=[pl.BlockSpec((MOE_BLOCK, d), blk),
                  pl.BlockSpec((1, 1, d, f2), exp4),
                  pl.BlockSpec((1, 1, 1, f2), exp4),
                  pl.BlockSpec((1, 1, d_ff, d), exp4),
                  pl.BlockSpec((1, 1, 1, d), exp4)],
        out_specs=pl.BlockSpec((MOE_BLOCK, d), lambda i, be, nu: (i, 0)),
        scratch_shapes=[pltpu.VMEM((d, f2), BF16), pltpu.VMEM((d_ff, d), BF16)],
    )
    return pl.pallas_call(
        functools.partial(_expert_kernel, d_ff=d_ff),
        grid_spec=grid_spec,
        out_shape=jax.ShapeDtypeStruct((n_slots, d), F32),
        compiler_params=_cparams(("arbitrary",)),
        name="moe_experts",
    )(block_e, n_used, buf, wgu, bgu.reshape(depth, ne, 1, f2), wdn, bdn.reshape(depth, ne, 1, d))


def _combine_kernel(dest_ref, yb_ref, w_ref, x_ref, g2_ref, o_ref, rows_ref, sem, *, tm):
    def issue(r, carry):
        for kk in range(TOP_K):
            pltpu.make_async_copy(yb_ref.at[pl.ds(dest_ref[r * TOP_K + kk], 1)],
                                  rows_ref.at[pl.ds(kk * tm + r, 1)], sem).start()
        return carry

    lax.fori_loop(0, tm, issue, 0, unroll=8)
    pltpu.make_async_copy(yb_ref.at[pl.ds(0, TOP_K * tm)], rows_ref, sem).wait()
    w = w_ref[...]
    y = w[:, 0:1] * rows_ref[0:tm, :]
    for kk in range(1, TOP_K):
        y = y + w[:, kk:kk + 1] * rows_ref[kk * tm:(kk + 1) * tm, :]
    o_ref[...] = x_ref[...] + g2_ref[0] * y


def _combine_call(yb, dest_flat, tw, x, g2, seq):
    t, d = x.shape
    tm = min(256, seq)
    per_b = seq // tm
    return pl.pallas_call(
        functools.partial(_combine_kernel, tm=tm),
        grid=(t // tm,),
        in_specs=[pl.BlockSpec((tm * TOP_K,), lambda i: (i,), memory_space=pltpu.SMEM),
                  pl.BlockSpec(memory_space=pl.ANY),
                  pl.BlockSpec((tm, LANE), lambda i: (i, 0)),
                  pl.BlockSpec((tm, d), lambda i: (i, 0)),
                  pl.BlockSpec((1, 1, d), lambda i: (i // per_b, 0, 0))],
        out_specs=pl.BlockSpec((tm, d), lambda i: (i, 0)),
        out_shape=jax.ShapeDtypeStruct((t, d), F32),
        scratch_shapes=[pltpu.VMEM((TOP_K * tm, d), F32), pltpu.SemaphoreType.DMA(())],
        compiler_params=_cparams(("arbitrary",)),
        name="moe_combine",
    )(dest_flat, yb, tw, x, g2)


def _final_norm_kernel(x_ref, w_ref, o_ref):
    x = x_ref[...]
    o_ref[...] = x * lax.rsqrt(jnp.mean(x * x, axis=-1, keepdims=True) + EPS) * w_ref[...]


def _final_norm_call(x, w):
    t, d = x.shape
    tm = min(512, t)
    return pl.pallas_call(
        _final_norm_kernel,
        grid=(t // tm,),
        in_specs=[pl.BlockSpec((tm, d), lambda i: (i, 0)), pl.BlockSpec((1, d), lambda i: (0, 0))],
        out_specs=pl.BlockSpec((tm, d), lambda i: (i, 0)),
        out_shape=jax.ShapeDtypeStruct((t, d), F32),
        compiler_params=_cparams(("parallel",)),
        name="final_norm",
    )(x, w.reshape(1, d))


def _tile_cols(w, hd):
    r = w.shape[0]
    w = w.reshape(r, N_HEADS, hd)
    return jnp.pad(w, ((0, 0), (0, 0), (0, LANE - hd))).reshape(r, HEAD_W)


def _pair_cols(w, hd):
    r = w.shape[0]
    w = w.reshape(r, N_HEADS, hd)
    return jnp.pad(w, ((0, 0), (0, 0), (0, HALF - hd))).reshape(r, PAIR_W)


def _query_cols(w, hd):
    r = w.shape[0]
    w = w.reshape(r, N_HEADS, hd)
    tiles = []
    for h in range(N_HEADS):
        lo = (h % 2) * HALF
        tiles.append(jnp.pad(w[:, h], ((0, 0), (lo, LANE - lo - hd))))
    return jnp.concatenate(tiles, axis=1)


def _tile_rows(w, hd):
    return _tile_cols(w.T, hd).T


IN_PARTS = (("gq", GLA_DK, _query_cols), ("gk", GLA_DK, _pair_cols), ("gv", GLA_DV, _tile_cols),
            ("gg", GLA_DV, _tile_cols), ("ga", None, None),
            ("hq", HG_DK, _query_cols), ("hf", HG_DK, _pair_cols), ("hi", HG_DV, _pair_cols),
            ("hg", HG_DV, _pair_cols),
            ("rq", RET_DK, _query_cols), ("rk", RET_DK, _pair_cols), ("rv", RET_DV, _tile_cols),
            ("rg", RET_DV, _tile_cols))
F32_PARTS = ("ga", "hf")


def _layout_w_in(w_in):
    cols, off = [], 0
    for _, hd, layout in IN_PARTS:
        if layout is None:
            part = jnp.pad(w_in[:, off:off + GLA_RANK], ((0, 0), (0, LANE - GLA_RANK)))
            off += GLA_RANK
        else:
            part = layout(w_in[:, off:off + N_HEADS * hd], hd)
            off += N_HEADS * hd
        cols.append(part)
    return jnp.concatenate(cols, axis=1).astype(BF16), tuple(int(p.shape[1]) for p in cols)


def kernel(x, c, positions, w_mod, b_mod, norm1_w, w_in, gla_wa2, gla_ba2, hg_lb, gla_norm_w, hg_norm_w,
           w_out, norm2_w, w_r, b_r, w_gu, b_gu, w_dn, b_dn, final_norm_w):
    nb, seq, d = x.shape
    t = nb * seq
    depth = w_mod.shape[0]
    dtypes = tuple(F32 if name in F32_PARTS else BF16 for name, _, _ in IN_PARTS)

    mod = _mod_call(c, w_mod, b_mod)
    cos, sin = _rope_call(positions)
    lb_all = _lb_call(hg_lb)

    n_assign = t * TOP_K
    n_blocks = (n_assign + MOE_BLOCK - 1) // MOE_BLOCK + N_EXPERTS
    n_slots = n_blocks * MOE_BLOCK

    xf = x.reshape(t, d)
    for layer in range(depth):
        sh1, sc1, g1, sh2, sc2, g2 = [m.reshape(nb, 1, d) for m in jnp.split(mod[layer], 6, axis=-1)]

        w_cat, widths = _layout_w_in(w_in[layer])
        parts = _inproj_call(xf, norm1_w[layer], sc1, sh1, w_cat, widths, dtypes, seq)
        gq, gk, gv, gg, ga, hq, hf, hi, hg, rq, rk, rv, rg = parts
        wa2p = jnp.pad(_pair_cols(gla_wa2[layer], GLA_DK), ((0, LANE - GLA_RANK), (0, 0)))
        ba2p = _pair_cols(gla_ba2[layer].reshape(1, -1), GLA_DK)
        o_gla = _gla_call(gq, gk, gv, gg, ga, wa2p, ba2p,
                          _tile_cols(gla_norm_w[layer].reshape(1, -1), GLA_DV), nb, seq)
        o_hg = _hg_call(hq, hf, hi, hg, lb_all[layer].reshape(1, -1), hg_norm_w[layer].reshape(1, -1), nb, seq)
        o_ret = _ret_call(rq, rk, rv, rg, cos, sin, nb, seq)

        wo = w_out[layer]
        n_gla, n_hg = N_HEADS * GLA_DV, N_HEADS * HG_DV
        wo_p = jnp.concatenate([_tile_rows(wo[:n_gla], GLA_DV), wo[n_gla:n_gla + n_hg],
                                _tile_rows(wo[n_gla + n_hg:], RET_DV)], axis=0).astype(BF16)
        wr_p = jnp.pad(w_r[layer], ((0, 0), (0, LANE - N_EXPERTS)))
        br_p = jnp.pad(b_r[layer].reshape(1, -1), ((0, 0), (0, LANE - N_EXPERTS)), constant_values=NEG_BIG)
        xf, h2, te, tw, rk_, cnt = _outproj_router_call(xf, o_gla, o_hg, o_ret, wo_p, g1, norm2_w[layer],
                                                        sc2, sh2, wr_p, br_p, seq)

        counts = cnt[0, :N_EXPERTS].astype(jnp.int32)
        padded = (counts + MOE_BLOCK - 1) // MOE_BLOCK * MOE_BLOCK
        pends = jnp.cumsum(padded)
        pstarts = pends - padded
        top_e = te[:, :TOP_K]
        dest = (pstarts[top_e] + rk_[:, :TOP_K]).reshape(-1).astype(jnp.int32)
        block_start = jnp.arange(n_blocks, dtype=jnp.int32) * MOE_BLOCK
        block_e = jnp.minimum(jnp.sum((pends[None, :] <= block_start[:, None]).astype(jnp.int32), axis=1),
                              N_EXPERTS - 1)
        n_used = (pends[-1:] // MOE_BLOCK).astype(jnp.int32)

        buf = _dispatch_call(h2, dest, n_slots)
        yb = _expert_call(buf, block_e, n_used, w_gu, b_gu, w_dn, b_dn, layer)
        xf = _combine_call(yb, dest, tw, xf, g2, seq)

    return _final_norm_call(xf, final_norm_w).reshape(nb, seq, d)
```

```python
import functools

import numpy as np
import jax
import jax.numpy as jnp
from jax import lax
from jax.experimental import pallas as pl
from jax.experimental.pallas import tpu as pltpu

F32 = jnp.float32
BF16 = jnp.bfloat16

N_HEADS = 4
GLA_DK, GLA_DV, GLA_RANK, GLA_TAU = 48, 96, 16, 16.0
HG_DK, HG_DV = 64, 64
RET_DK, RET_DV = 48, 96
ROPE_BASE = 10000.0
N_EXPERTS, TOP_K = 32, 4
SWIGLU_LIMIT, SWIGLU_ALPHA = 7.0, 1.702
MOE_BLOCK = 512
EPS = 1e-6

LANE = 128
HALF = LANE // 2
N_PAIRS = N_HEADS // 2
HEAD_W = N_HEADS * LANE
PAIR_W = N_PAIRS * LANE
VMEM_LIMIT = 56 * 1024 * 1024
NEG_BIG = -1e30


def _cparams(sem):
    return pltpu.CompilerParams(dimension_semantics=sem, vmem_limit_bytes=VMEM_LIMIT)


def _dot(a, b):
    return jnp.dot(a, b, preferred_element_type=F32)


def _dot_nt(a, b):
    return lax.dot_general(a, b, (((1,), (1,)), ((), ())), preferred_element_type=F32)


def _dot_tn(a, b):
    return lax.dot_general(a, b, (((0,), (0,)), ((), ())), preferred_element_type=F32)


def _split2(a):
    hi = a.astype(BF16)
    return hi, (a - hi.astype(F32)).astype(BF16)


def _dot_f32(a, b):
    a_hi, a_lo = _split2(a)
    b_hi, b_lo = _split2(b)
    return _dot(a_hi, b_hi) + _dot(a_hi, b_lo) + _dot(a_lo, b_hi)


def _sigmoid(x):
    return 1.0 / (1.0 + jnp.exp(-x))


def _log_sigmoid(x):
    return jnp.minimum(x, 0.0) - jnp.log1p(jnp.exp(-jnp.abs(x)))


def _mod_kernel(c_ref, w_ref, b_ref, o_ref):
    c = c_ref[...]
    o_ref[0] = _dot_f32(c * _sigmoid(c), w_ref[0]) + b_ref[0]


def _mod_call(c, w_mod, b_mod):
    depth, d, d6 = w_mod.shape
    nb = c.shape[0]
    rows = 8
    c_pad = jnp.zeros((rows, d), F32).at[:nb].set(c)
    out = pl.pallas_call(
        _mod_kernel,
        grid=(depth, d6 // d),
        in_specs=[
            pl.BlockSpec((rows, d), lambda l, j: (0, 0)),
            pl.BlockSpec((1, d, d), lambda l, j: (l, 0, j)),
            pl.BlockSpec((1, 1, d), lambda l, j: (l, 0, j)),
        ],
        out_specs=pl.BlockSpec((1, rows, d), lambda l, j: (l, 0, j)),
        out_shape=jax.ShapeDtypeStruct((depth, rows, d6), F32),
        compiler_params=_cparams(("parallel", "parallel")),
        name="mod",
    )(c_pad, w_mod, b_mod.reshape(depth, 1, d6))
    return out[:, :nb]


def _rope_kernel(pos_ref, freq_ref, cos_ref, sin_ref):
    ang = pos_ref[0] * freq_ref[...]
    cos_ref[0] = jnp.cos(ang)
    sin_ref[0] = jnp.sin(ang)


def _rope_call(positions):
    nb, l = positions.shape
    half = RET_DK // 2
    inv = (ROPE_BASE ** (-np.arange(half, dtype=np.float32) / half)).astype(np.float32)
    freq = np.zeros((1, LANE), np.float32)
    for base in (0, HALF):
        freq[0, base:base + half] = inv
        freq[0, base + half:base + 2 * half] = inv
    tl = min(l, 512)
    pos = positions.astype(F32).reshape(nb, l, 1)
    shp = jax.ShapeDtypeStruct((nb, l, LANE), F32)
    return pl.pallas_call(
        _rope_kernel,
        grid=(nb, l // tl),
        in_specs=[pl.BlockSpec((1, tl, 1), lambda b, i: (b, i, 0)),
                  pl.BlockSpec((1, LANE), lambda b, i: (0, 0))],
        out_specs=[pl.BlockSpec((1, tl, LANE), lambda b, i: (b, i, 0))] * 2,
        out_shape=[shp, shp],
        compiler_params=_cparams(("parallel", "parallel")),
        name="rope_tables",
    )(pos, jnp.asarray(freq))


def _lb_kernel(p_ref, o_ref):
    p = p_ref[...]
    depth = p.shape[0]
    m = jnp.max(p, axis=0, keepdims=True)
    e = jnp.exp(p - m)
    sm = e / jnp.sum(e, axis=0, keepdims=True)
    acc = jnp.zeros_like(sm[0:1])
    for i in range(depth):
        if i > 0:
            acc = acc + sm[i:i + 1]
        o_ref[i:i + 1, :] = acc


def _lb_call(hg_lb):
    return pl.pallas_call(
        _lb_kernel,
        out_shape=jax.ShapeDtypeStruct(hg_lb.shape, F32),
        name="hg_lower_bounds",
    )(hg_lb.astype(F32))


def _inproj_kernel(x_ref, nw_ref, sc_ref, sh_ref, w_ref, *out_refs, widths):
    x = x_ref[...]
    y = x * lax.rsqrt(jnp.mean(x * x, axis=-1, keepdims=True) + EPS) * nw_ref[...]
    hb = (y * (1.0 + sc_ref[0]) + sh_ref[0]).astype(BF16)
    off = 0
    for o_ref, wd in zip(out_refs, widths):
        o_ref[...] = _dot(hb, w_ref[:, off:off + wd]).astype(o_ref.dtype)
        off += wd


def _inproj_call(x, norm_w, sc, sh, w_cat, layer, widths, dtypes, seq):
    t, d = x.shape
    tm = min(512, seq)
    per_b = seq // tm
    return pl.pallas_call(
        functools.partial(_inproj_kernel, widths=widths),
        grid=(t // tm,),
        in_specs=[
            pl.BlockSpec((tm, d), lambda i: (i, 0)),
            pl.BlockSpec((1, d), lambda i: (0, 0)),
            pl.BlockSpec((1, 1, d), lambda i: (i // per_b, 0, 0)),
            pl.BlockSpec((1, 1, d), lambda i: (i // per_b, 0, 0)),
            pl.BlockSpec((None,) + w_cat.shape[1:], lambda i: (layer, 0, 0)),
        ],
        out_specs=[pl.BlockSpec((tm, wd), lambda i: (i, 0)) for wd in widths],
        out_shape=[jax.ShapeDtypeStruct((t, wd), dt) for wd, dt in zip(widths, dtypes)],
        compiler_params=_cparams(("parallel",)),
        name="norm1_inproj",
    )(x, norm_w.reshape(1, d), sc, sh, w_cat)


def _level_tables(c):
    nl = int(np.log2(c))
    assert 1 << nl == c
    idx = np.arange(c)
    mats, masks = [], [np.eye(c, dtype=np.float32)]
    t = idx[None, :]
    i = idx[:, None]
    for lvl in range(nl):
        h = c >> (lvl + 1)
        blk, pos = idx // (2 * h), idx % (2 * h)
        m = (blk * 2 * h + h - 1)[:, None]
        right = (pos >= h)[:, None]
        a = np.where(right, (t > m) & (t <= i), (t > i) & (t <= m))
        mats.append(a.astype(np.float32))
        same = blk[:, None] == blk[None, :]
        masks.append((same & right & (pos < h)[None, :]).astype(np.float32))
    mats.append((t <= i).astype(np.float32))
    mats.append((t > i).astype(np.float32))
    return np.concatenate(mats, 0), np.stack(masks, 0), nl


def _pair_decays(g, mall_ref):
    gs = _dot(mall_ref[...], jnp.concatenate(_split2(g), axis=1))
    return jnp.exp(gs[:, :LANE] + gs[:, LANE:])


def _pair_keys(k, e, c, nl):
    levels = [(k * e[lvl * c:(lvl + 1) * c]).astype(BF16) for lvl in range(nl)]
    return k.astype(BF16), levels, (k * e[(nl + 1) * c:(nl + 2) * c]).astype(BF16)


def _gated_head(q, kb, klv, e, v, st, masks_ref, c, nl):
    scores = masks_ref[0] * _dot_nt(q.astype(BF16), kb)
    for lvl in range(nl):
        scores = scores + masks_ref[lvl + 1] * _dot_nt((q * e[lvl * c:(lvl + 1) * c]).astype(BF16), klv[lvl])
    q_in = (q * e[nl * c:(nl + 1) * c]).astype(BF16)
    return _dot(scores.astype(BF16), v) + _dot_nt(q_in, st.astype(BF16))


def _gla_kernel(q_ref, k_ref, v_ref, gg_ref, ga_ref, wa2_ref, ba2_ref, nw_ref, mall_ref, masks_ref,
                o_ref, st_ref, *, c, nl):
    @pl.when(pl.program_id(0) == 0)
    def _():
        st_ref[...] = jnp.zeros_like(st_ref)

    for b in range(q_ref.shape[0]):
        log_alpha = _log_sigmoid(_dot_f32(ga_ref[b], wa2_ref[...]) + ba2_ref[...]) * (1.0 / GLA_TAU)
        for pr in range(N_PAIRS):
            psl = slice(pr * LANE, (pr + 1) * LANE)
            e = _pair_decays(log_alpha[:, psl], mall_ref)
            kb, klv, k_end = _pair_keys(k_ref[b, :, psl].astype(F32), e, c, nl)
            dec = e[(nl + 1) * c - 1:(nl + 1) * c, :]
            for hd in (2 * pr, 2 * pr + 1):
                sl = slice(hd * LANE, (hd + 1) * LANE)
                q = q_ref[b, :, sl].astype(F32) * (GLA_DK ** -0.5)
                v = v_ref[b, :, sl]
                st = st_ref[b, hd]
                o = _gated_head(q, kb, klv, e, v, st, masks_ref, c, nl)
                st_ref[b, hd] = dec * st + _dot_tn(v, k_end)
                y = o * lax.rsqrt(jnp.sum(o * o, axis=-1, keepdims=True) * (1.0 / GLA_DV) + EPS) * nw_ref[:, sl]
                gate = gg_ref[b, :, sl].astype(F32)
                o_ref[b, :, sl] = (y * gate * _sigmoid(gate)).astype(o_ref.dtype)


def _hg_kernel(q_ref, f_ref, v_ref, og_ref, lb_ref, nw_ref, mall_ref, masks_ref,
               o_ref, st_ref, *, c, nl):
    @pl.when(pl.program_id(0) == 0)
    def _():
        st_ref[...] = jnp.zeros_like(st_ref)

    low = lax.broadcasted_iota(jnp.int32, (c, LANE), 1) < HALF
    for b in range(q_ref.shape[0]):
        for pr in range(N_PAIRS):
            psl = slice(pr * LANE, (pr + 1) * LANE)
            lb = lb_ref[:, psl]
            hf = f_ref[b, :, psl]
            la = jnp.log(lb)
            lc = jnp.log1p(-lb) + _log_sigmoid(hf)
            log_f = jnp.maximum(la, lc) + jnp.log1p(jnp.exp(-jnp.abs(la - lc)))
            e = _pair_decays(log_f, mall_ref)
            kb, klv, k_end = _pair_keys((1.0 - lb) * _sigmoid(-hf), e, c, nl)
            v = v_ref[b, :, psl]
            st = st_ref[b, pr]
            o_even = _gated_head(q_ref[b, :, (2 * pr) * LANE:(2 * pr + 1) * LANE].astype(F32),
                                 kb, klv, e, v, st, masks_ref, c, nl)
            o_odd = _gated_head(q_ref[b, :, (2 * pr + 1) * LANE:(2 * pr + 2) * LANE].astype(F32),
                                kb, klv, e, v, st, masks_ref, c, nl)
            o = jnp.where(low, o_even, o_odd)
            st_ref[b, pr] = e[(nl + 1) * c - 1:(nl + 1) * c, :] * st + _dot_tn(v, k_end)
            sq = o * o
            ms = jnp.where(low, jnp.sum(jnp.where(low, sq, 0.0), axis=-1, keepdims=True),
                           jnp.sum(jnp.where(low, 0.0, sq), axis=-1, keepdims=True)) * (1.0 / HG_DV)
            y = o * lax.rsqrt(ms + EPS) * nw_ref[:, psl]
            o_ref[b, :, psl] = (y * _sigmoid(og_ref[b, :, psl].astype(F32))).astype(o_ref.dtype)


def _tok_spec(nb, c, width):
    return pl.BlockSpec((nb, c, width), lambda i: (0, i, 0))


def _const_spec(shape):
    nd = len(shape)
    return pl.BlockSpec(shape, lambda i: (0,) * nd)


def _seq_view(a, nb, seq):
    return a.reshape(nb, seq, a.shape[-1])


def _gla_call(gq, gk, gv, gg, ga, wa2p, ba2p, nwp, nb, seq):
    c = min(128, seq)
    n = seq // c
    mall, masks, nl = _level_tables(c)
    sv = lambda a: _seq_view(a, nb, seq)
    return pl.pallas_call(
        functools.partial(_gla_kernel, c=c, nl=nl),
        grid=(n,),
        in_specs=[_tok_spec(nb, c, HEAD_W), _tok_spec(nb, c, PAIR_W), _tok_spec(nb, c, HEAD_W),
                  _tok_spec(nb, c, HEAD_W), _tok_spec(nb, c, LANE),
                  _const_spec(wa2p.shape), _const_spec(ba2p.shape), _const_spec(nwp.shape),
                  _const_spec(mall.shape), _const_spec(masks.shape)],
        out_specs=_tok_spec(nb, c, HEAD_W),
        out_shape=jax.ShapeDtypeStruct((nb, seq, HEAD_W), BF16),
        scratch_shapes=[pltpu.VMEM((nb, N_HEADS, LANE, LANE), F32)],
        compiler_params=_cparams(("arbitrary",)),
        name="gla_recurrence",
    )(sv(gq), sv(gk), sv(gv), sv(gg), sv(ga), wa2p, ba2p, nwp,
      jnp.asarray(mall, BF16), jnp.asarray(masks)).reshape(nb * seq, HEAD_W)


def _hg_call(hq, hf, hi, hg, lbp, nwp, nb, seq):
    c = min(128, seq)
    n = seq // c
    mall, masks, nl = _level_tables(c)
    sv = lambda a: _seq_view(a, nb, seq)
    return pl.pallas_call(
        functools.partial(_hg_kernel, c=c, nl=nl),
        grid=(n,),
        in_specs=[_tok_spec(nb, c, HEAD_W), _tok_spec(nb, c, PAIR_W), _tok_spec(nb, c, PAIR_W),
                  _tok_spec(nb, c, PAIR_W),
                  _const_spec(lbp.shape), _const_spec(nwp.shape),
                  _const_spec(mall.shape), _const_spec(masks.shape)],
        out_specs=_tok_spec(nb, c, PAIR_W),
        out_shape=jax.ShapeDtypeStruct((nb, seq, PAIR_W), BF16),
        scratch_shapes=[pltpu.VMEM((nb, N_PAIRS, LANE, LANE), F32)],
        compiler_params=_cparams(("arbitrary",)),
        name="hgrn2_recurrence",
    )(sv(hq), sv(hf), sv(hi), sv(hg), lbp, nwp,
      jnp.asarray(mall, BF16), jnp.asarray(masks)).reshape(nb * seq, PAIR_W)


def _ret_tables(c):
    hs = np.arange(N_HEADS, dtype=np.float64)
    log_gamma = np.log(1.0 - np.exp2(-5.0 - hs))
    idx = np.arange(c, dtype=np.float64)
    rel = idx[:, None] - idx[None, :]
    dmat = np.where(rel >= 0, np.exp(log_gamma[:, None, None] * np.maximum(rel, 0.0)), 0.0)
    qdec = np.exp(log_gamma[:, None] * (idx + 1.0))
    kdec = np.exp(log_gamma[:, None] * (c - 1.0 - idx))
    cdec = np.exp(log_gamma * c)
    qfull = np.broadcast_to(qdec[:, :, None], (N_HEADS, c, LANE)).astype(np.float32)
    kpair = np.zeros((N_PAIRS, c, LANE), np.float32)
    for hd in range(N_HEADS):
        kpair[hd // 2, :, (hd % 2) * HALF:(hd % 2 + 1) * HALF] = kdec[hd][:, None]
    return dmat.astype(np.float32), qfull, kpair, [float(np.float32(v)) for v in cdec]


def _ret_kernel(q_ref, k_ref, v_ref, og_ref, cos_ref, sin_ref, dmat_ref, qdec_ref, kdec_ref,
                o_ref, st_ref, *, c, cdec):
    @pl.when(pl.program_id(0) == 0)
    def _():
        st_ref[...] = jnp.zeros_like(st_ref)

    half = RET_DK // 2
    lane = lax.broadcasted_iota(jnp.int32, (c, LANE), 1)
    first = (lane & (HALF - 1)) < half

    for b in range(q_ref.shape[0]):
        cos = cos_ref[b]
        sin = sin_ref[b]

        def rotary(t):
            rot = jnp.where(first, -pltpu.roll(t, LANE - half, 1), pltpu.roll(t, half, 1))
            return t * cos + rot * sin

        for pr in range(N_PAIRS):
            psl = slice(pr * LANE, (pr + 1) * LANE)
            k = rotary(k_ref[b, :, psl].astype(F32))
            kb = k.astype(BF16)
            k_end = (k * kdec_ref[pr]).astype(BF16)
            for hd in (2 * pr, 2 * pr + 1):
                sl = slice(hd * LANE, (hd + 1) * LANE)
                q = rotary(q_ref[b, :, sl].astype(F32)) * (RET_DK ** -0.5)
                v = v_ref[b, :, sl]
                st = st_ref[b, hd]
                scores = _dot_nt(q.astype(BF16), kb) * dmat_ref[hd]
                o = _dot(scores.astype(BF16), v) + _dot_nt((q * qdec_ref[hd]).astype(BF16), st.astype(BF16))
                st_ref[b, hd] = cdec[hd] * st + _dot_tn(v, k_end)
                mu = jnp.sum(o, axis=-1, keepdims=True) * (1.0 / RET_DV)
                dlt = jnp.where(lane < RET_DV, o - mu, 0.0)
                var = jnp.sum(dlt * dlt, axis=-1, keepdims=True) * (1.0 / RET_DV)
                gate = og_ref[b, :, sl].astype(F32)
                o_ref[b, :, sl] = (dlt * lax.rsqrt(var + EPS) * gate * _sigmoid(gate)).astype(o_ref.dtype)


def _ret_call(rq, rk, rv, rg, cos, sin, nb, seq):
    c = min(128, seq)
    n = seq // c
    dmat, qdec, kdec, cdec = _ret_tables(c)
    sv = lambda a: _seq_view(a, nb, seq)
    return pl.pallas_call(
        functools.partial(_ret_kernel, c=c, cdec=cdec),
        grid=(n,),
        in_specs=[_tok_spec(nb, c, HEAD_W), _tok_spec(nb, c, PAIR_W), _tok_spec(nb, c, HEAD_W),
                  _tok_spec(nb, c, HEAD_W), _tok_spec(nb, c, LANE), _tok_spec(nb, c, LANE),
                  _const_spec(dmat.shape), _const_spec(qdec.shape), _const_spec(kdec.shape)],
        out_specs=_tok_spec(nb, c, HEAD_W),
        out_shape=jax.ShapeDtypeStruct((nb, seq, HEAD_W), BF16),
        scratch_shapes=[pltpu.VMEM((nb, N_HEADS, LANE, LANE), F32)],
        compiler_params=_cparams(("arbitrary",)),
        name="retention",
    )(sv(rq), sv(rk), sv(rv), sv(rg), cos, sin,
      jnp.asarray(dmat), jnp.asarray(qdec), jnp.asarray(kdec)).reshape(nb * seq, HEAD_W)


def _outproj_router_kernel(x_ref, oa_ref, ob_ref, oc_ref, wo_ref, g1_ref, nw_ref, sc_ref, sh_ref,
                           wr_ref, br_ref, tri_ref,
                           xo_ref, h_ref, te_ref, tw_ref, rk_ref, cnt_ref, *, tm):
    @pl.when(pl.program_id(0) == 0)
    def _():
        cnt_ref[...] = jnp.zeros_like(cnt_ref)

    wa, wb = oa_ref.shape[1], ob_ref.shape[1]
    mix = (_dot(oa_ref[...], wo_ref[0:wa, :]) + _dot(ob_ref[...], wo_ref[wa:wa + wb, :])
           + _dot(oc_ref[...], wo_ref[wa + wb:, :]))
    x = x_ref[...] + g1_ref[0] * mix
    xo_ref[...] = x
    y = x * lax.rsqrt(jnp.mean(x * x, axis=-1, keepdims=True) + EPS) * nw_ref[...]
    h = y * (1.0 + sc_ref[0]) + sh_ref[0]
    h_ref[...] = h

    lg = _dot_f32(h, wr_ref[...]) + br_ref[...]
    lane = lax.broadcasted_iota(jnp.int32, (tm, LANE), 1)
    sel_e, sel_v = [], []
    for _ in range(TOP_K):
        m = jnp.max(lg, axis=-1, keepdims=True)
        idx = jnp.min(jnp.where(lg == m, lane, LANE), axis=-1, keepdims=True)
        sel_e.append(idx)
        sel_v.append(m)
        lg = jnp.where(lane == idx, -jnp.inf, lg)
    ex = [jnp.exp(v - sel_v[0]) for v in sel_v]
    den = ex[0] + ex[1] + ex[2] + ex[3]
    hot = [(lane == idx) for idx in sel_e]
    onehot = jnp.zeros((tm, LANE), F32)
    for hk in hot:
        onehot = onehot + jnp.where(hk, 1.0, 0.0)
    before = _dot(tri_ref[...], onehot.astype(BF16)) + cnt_ref[...]
    te = jnp.zeros((tm, LANE), jnp.int32)
    tw = jnp.zeros((tm, LANE), F32)
    rk = jnp.zeros((tm, LANE), jnp.int32)
    for kk in range(TOP_K):
        rank = jnp.sum(jnp.where(hot[kk], before, 0.0), axis=-1, keepdims=True).astype(jnp.int32)
        te = jnp.where(lane == kk, sel_e[kk], te)
        tw = jnp.where(lane == kk, ex[kk] / den, tw)
        rk = jnp.where(lane == kk, rank, rk)
    te_ref[...] = te
    tw_ref[...] = tw
    rk_ref[...] = rk
    cnt_ref[...] = cnt_ref[...] + jnp.sum(onehot, axis=0, keepdims=True)


def _outproj_router_call(x, oa, ob, oc, wo, layer, g1, nw, sc, sh, wr, br, seq):
    t, d = x.shape
    tm = min(512, seq)
    per_b = seq // tm
    tri = np.tril(np.ones((tm, tm), np.float32), -1)
    row = lambda w: pl.BlockSpec((tm, w), lambda i: (i, 0))
    const = lambda shape: pl.BlockSpec(shape, lambda i: (0,) * len(shape))
    perb = pl.BlockSpec((1, 1, d), lambda i: (i // per_b, 0, 0))
    return pl.pallas_call(
        functools.partial(_outproj_router_kernel, tm=tm),
        grid=(t // tm,),
        in_specs=[row(d), row(oa.shape[1]), row(ob.shape[1]), row(oc.shape[1]),
                  pl.BlockSpec((None,) + wo.shape[1:], lambda i: (layer, 0, 0)), perb,
                  const((1, d)), perb, perb, const(wr.shape), const(br.shape), const(tri.shape)],
        out_specs=[row(d), row(d), row(LANE), row(LANE), row(LANE), const((1, LANE))],
        out_shape=[jax.ShapeDtypeStruct((t, d), F32), jax.ShapeDtypeStruct((t, d), F32),
                   jax.ShapeDtypeStruct((t, LANE), jnp.int32), jax.ShapeDtypeStruct((t, LANE), F32),
                   jax.ShapeDtypeStruct((t, LANE), jnp.int32), jax.ShapeDtypeStruct((1, LANE), F32)],
        compiler_params=_cparams(("arbitrary",)),
        name="outproj_norm2_router",
    )(x, oa, ob, oc, wo, g1, nw.reshape(1, d), sc, sh, wr, br, jnp.asarray(tri, BF16))


def _dispatch_kernel(dest_ref, h_ref, buf_in_ref, buf_ref, sem, *, tm):
    del buf_in_ref

    def issue(r, carry):
        for kk in range(TOP_K):
            pltpu.make_async_copy(h_ref.at[pl.ds(r, 1)],
                                  buf_ref.at[pl.ds(dest_ref[r * TOP_K + kk], 1)], sem).start()
        return carry

    lax.fori_loop(0, tm, issue, 0, unroll=8)
    for _ in range(TOP_K):
        pltpu.make_async_copy(h_ref, buf_ref.at[pl.ds(0, tm)], sem).wait()


def _dispatch_call(h, dest_flat, buf):
    t, d = h.shape
    n_slots = buf.shape[0]
    tm = min(256, t)
    return pl.pallas_call(
        functools.partial(_dispatch_kernel, tm=tm),
        grid=(t // tm,),
        in_specs=[pl.BlockSpec((tm * TOP_K,), lambda i: (i,), memory_space=pltpu.SMEM),
                  pl.BlockSpec((tm, d), lambda i: (i, 0)),
                  pl.BlockSpec(memory_space=pl.ANY)],
        out_specs=pl.BlockSpec(memory_space=pl.ANY),
        out_shape=jax.ShapeDtypeStruct((n_slots, d), F32),
        scratch_shapes=[pltpu.SemaphoreType.DMA(())],
        input_output_aliases={2: 0},
        compiler_params=_cparams(("arbitrary",)),
        name="moe_dispatch",
    )(dest_flat, h, buf)


def _expert_kernel(be_ref, nu_ref, x_ref, wgu_ref, bgu_ref, wdn_ref, bdn_ref, o_ref, wgu_bf, wdn_bf, *, d_ff):
    i = pl.program_id(0)
    live = i < nu_ref[0]

    @pl.when(jnp.logical_and(live, jnp.logical_or(i == 0, be_ref[i] != be_ref[jnp.maximum(i - 1, 0)])))
    def _():
        wgu_bf[...] = wgu_ref[0, 0].astype(BF16)
        wdn_bf[...] = wdn_ref[0, 0].astype(BF16)

    @pl.when(live)
    def _():
        gu = _dot(x_ref[...].astype(BF16), wgu_bf[...]) + bgu_ref[0, 0]
        gate = jnp.minimum(gu[:, :d_ff], SWIGLU_LIMIT)
        up = jnp.clip(gu[:, d_ff:], -SWIGLU_LIMIT, SWIGLU_LIMIT)
        act = (up + 1.0) * gate * _sigmoid(SWIGLU_ALPHA * gate)
        o_ref[...] = _dot(act.astype(BF16), wdn_bf[...]) + bdn_ref[0, 0]

    @pl.when(jnp.logical_not(live))
    def _():
        o_ref[...] = jnp.zeros_like(o_ref)


def _expert_call(buf, block_e, n_used, wgu, bgu, wdn, bdn, layer):
    n_slots, d = buf.shape
    n_blocks = n_slots // MOE_BLOCK
    depth, ne, _, f2 = wgu.shape
    d_ff = f2 // 2
    blk = lambda i, be, nu: (jnp.minimum(i, nu[0] - 1), 0)
    exp4 = lambda i, be, nu: (layer, be[jnp.minimum(i, nu[0] - 1)], 0, 0)
    grid_spec = pltpu.PrefetchScalarGridSpec(
        num_scalar_prefetch=2,
        grid=(n_blocks,),
        in_specs=[pl.BlockSpec((MOE_BLOCK, d), blk),
                  pl.BlockSpec((1, 1, d, f2), exp4),
                  pl.BlockSpec((1, 1, 1, f2), exp4),
                  pl.BlockSpec((1, 1, d_ff, d), exp4),
                  pl.BlockSpec((1, 1, 1, d), exp4)],
        out_specs=pl.BlockSpec((MOE_BLOCK, d), lambda i, be, nu: (i, 0)),
        scratch_shapes=[pltpu.VMEM((d, f2), BF16), pltpu.VMEM((d_ff, d), BF16)],
    )
    return pl.pallas_call(
        functools.partial(_expert_kernel, d_ff=d_ff),
        grid_spec=grid_spec,
        out_shape=jax.ShapeDtypeStruct((n_slots, d), F32),
        compiler_params=_cparams(("arbitrary",)),
        name="moe_experts",
    )(block_e, n_used, buf, wgu, bgu.reshape(depth, ne, 1, f2), wdn, bdn.reshape(depth, ne, 1, d))


def _combine_kernel(dest_ref, yb_ref, w_ref, x_ref, g2_ref, o_ref, rows_ref, sem, *, tm):
    def issue(r, carry):
        for kk in range(TOP_K):
            pltpu.make_async_copy(yb_ref.at[pl.ds(dest_ref[r * TOP_K + kk], 1)],
                                  rows_ref.at[pl.ds(kk * tm + r, 1)], sem).start()
        return carry

    lax.fori_loop(0, tm, issue, 0, unroll=8)
    pltpu.make_async_copy(yb_ref.at[pl.ds(0, TOP_K * tm)], rows_ref, sem).wait()
    w = w_ref[...]
    y = w[:, 0:1] * rows_ref[0:tm, :]
    for kk in range(1, TOP_K):
        y = y + w[:, kk:kk + 1] * rows_ref[kk * tm:(kk + 1) * tm, :]
    o_ref[...] = x_ref[...] + g2_ref[0] * y


def _combine_call(yb, dest_flat, tw, x, g2, seq):
    t, d = x.shape
    tm = min(256, seq)
    per_b = seq // tm
    return pl.pallas_call(
        functools.partial(_combine_kernel, tm=tm),
        grid=(t // tm,),
        in_specs=[pl.BlockSpec((tm * TOP_K,), lambda i: (i,), memory_space=pltpu.SMEM),
                  pl.BlockSpec(memory_space=pl.ANY),
                  pl.BlockSpec((tm, LANE), lambda i: (i, 0)),
                  pl.BlockSpec((tm, d), lambda i: (i, 0)),
                  pl.BlockSpec((1, 1, d), lambda i: (i // per_b, 0, 0))],
        out_specs=pl.BlockSpec((tm, d), lambda i: (i, 0)),
        out_shape=jax.ShapeDtypeStruct((t, d), F32),
        scratch_shapes=[pltpu.VMEM((TOP_K * tm, d), F32), pltpu.SemaphoreType.DMA(())],
        compiler_params=_cparams(("arbitrary",)),
        name="moe_combine",
    )(dest_flat, yb, tw, x, g2)


def _final_norm_kernel(x_ref, w_ref, o_ref):
    x = x_ref[...]
    o_ref[...] = x * lax.rsqrt(jnp.mean(x * x, axis=-1, keepdims=True) + EPS) * w_ref[...]


def _final_norm_call(x, w):
    t, d = x.shape
    tm = min(512, t)
    return pl.pallas_call(
        _final_norm_kernel,
        grid=(t // tm,),
        in_specs=[pl.BlockSpec((tm, d), lambda i: (i, 0)), pl.BlockSpec((1, d), lambda i: (0, 0))],
        out_specs=pl.BlockSpec((tm, d), lambda i: (i, 0)),
        out_shape=jax.ShapeDtypeStruct((t, d), F32),
        compiler_params=_cparams(("parallel",)),
        name="final_norm",
    )(x, w.reshape(1, d))


def _split_heads(w, hd):
    return w.reshape(w.shape[:-1] + (N_HEADS, hd))


def _pad_last(w, lo, hi):
    return jnp.pad(w, [(0, 0)] * (w.ndim - 1) + [(lo, hi)])


def _tile_cols(w, hd):
    return _pad_last(_split_heads(w, hd), 0, LANE - hd).reshape(w.shape[:-1] + (HEAD_W,))


def _pair_cols(w, hd):
    return _pad_last(_split_heads(w, hd), 0, HALF - hd).reshape(w.shape[:-1] + (PAIR_W,))


def _query_cols(w, hd):
    w = _split_heads(w, hd)
    tiles = [_pad_last(w[..., h, :], (h % 2) * HALF, LANE - (h % 2) * HALF - hd) for h in range(N_HEADS)]
    return jnp.concatenate(tiles, axis=-1)


def _tile_rows(w, hd):
    return jnp.swapaxes(_tile_cols(jnp.swapaxes(w, -1, -2), hd), -1, -2)


IN_PARTS = (("gq", GLA_DK, _query_cols), ("gk", GLA_DK, _pair_cols), ("gv", GLA_DV, _tile_cols),
            ("gg", GLA_DV, _tile_cols), ("ga", None, None),
            ("hq", HG_DK, _query_cols), ("hf", HG_DK, _pair_cols), ("hi", HG_DV, _pair_cols),
            ("hg", HG_DV, _pair_cols),
            ("rq", RET_DK, _query_cols), ("rk", RET_DK, _pair_cols), ("rv", RET_DV, _tile_cols),
            ("rg", RET_DV, _tile_cols))
F32_PARTS = ("ga", "hf")


def _layout_w_in(w_in):
    cols, off = [], 0
    for _, hd, layout in IN_PARTS:
        if layout is None:
            part = _pad_last(w_in[..., off:off + GLA_RANK], 0, LANE - GLA_RANK)
            off += GLA_RANK
        else:
            part = layout(w_in[..., off:off + N_HEADS * hd], hd)
            off += N_HEADS * hd
        cols.append(part)
    return jnp.concatenate(cols, axis=-1).astype(BF16), tuple(int(p.shape[-1]) for p in cols)


def kernel(x, c, positions, w_mod, b_mod, norm1_w, w_in, gla_wa2, gla_ba2, hg_lb, gla_norm_w, hg_norm_w,
           w_out, norm2_w, w_r, b_r, w_gu, b_gu, w_dn, b_dn, final_norm_w):
    nb, seq, d = x.shape
    t = nb * seq
    depth = w_mod.shape[0]
    dtypes = tuple(F32 if name in F32_PARTS else BF16 for name, _, _ in IN_PARTS)

    mod = _mod_call(c, w_mod, b_mod)
    cos, sin = _rope_call(positions)
    lb_all = _lb_call(hg_lb)

    n_assign = t * TOP_K
    n_blocks = (n_assign + MOE_BLOCK - 1) // MOE_BLOCK + N_EXPERTS
    n_slots = n_blocks * MOE_BLOCK

    w_cat, widths = _layout_w_in(w_in)
    n_gla, n_hg = N_HEADS * GLA_DV, N_HEADS * HG_DV
    wo_p = jnp.concatenate([_tile_rows(w_out[:, :n_gla], GLA_DV), w_out[:, n_gla:n_gla + n_hg],
                            _tile_rows(w_out[:, n_gla + n_hg:], RET_DV)], axis=1).astype(BF16)
    wa2p = jnp.pad(_pair_cols(gla_wa2, GLA_DK), ((0, 0), (0, LANE - GLA_RANK), (0, 0)))
    ba2p = _pair_cols(gla_ba2, GLA_DK)
    gnw = _tile_cols(gla_norm_w, GLA_DV)
    wr_p = _pad_last(w_r, 0, LANE - N_EXPERTS)
    br_p = jnp.pad(b_r, ((0, 0), (0, LANE - N_EXPERTS)), constant_values=NEG_BIG)
    expert_ids = jnp.arange(N_EXPERTS, dtype=jnp.int32)

    buf = jnp.zeros((n_slots, d), F32)
    xf = x.reshape(t, d)
    for layer in range(depth):
        sh1, sc1, g1, sh2, sc2, g2 = [m.reshape(nb, 1, d) for m in jnp.split(mod[layer], 6, axis=-1)]

        parts = _inproj_call(xf, norm1_w[layer], sc1, sh1, w_cat, layer, widths, dtypes, seq)
        gq, gk, gv, gg, ga, hq, hf, hi, hg, rq, rk, rv, rg = parts
        o_gla = _gla_call(gq, gk, gv, gg, ga, wa2p[layer], ba2p[layer:layer + 1], gnw[layer:layer + 1], nb, seq)
        o_hg = _hg_call(hq, hf, hi, hg, lb_all[layer:layer + 1], hg_norm_w[layer:layer + 1], nb, seq)
        o_ret = _ret_call(rq, rk, rv, rg, cos, sin, nb, seq)
        xf, h2, te, tw, rk_, cnt = _outproj_router_call(xf, o_gla, o_hg, o_ret, wo_p, layer, g1, norm2_w[layer],
                                                        sc2, sh2, wr_p[layer], br_p[layer:layer + 1], seq)

        counts = cnt[0, :N_EXPERTS].astype(jnp.int32)
        padded = (counts + MOE_BLOCK - 1) // MOE_BLOCK * MOE_BLOCK
        pends = jnp.cumsum(padded)
        pstarts = pends - padded
        top_e = te[:, :TOP_K]
        start_of = jnp.sum(jnp.where(top_e[:, :, None] == expert_ids, pstarts, 0), axis=-1)
        dest = (start_of + rk_[:, :TOP_K]).reshape(-1).astype(jnp.int32)
        block_start = jnp.arange(n_blocks, dtype=jnp.int32) * MOE_BLOCK
        block_e = jnp.minimum(jnp.sum((pends[None, :] <= block_start[:, None]).astype(jnp.int32), axis=1),
                              N_EXPERTS - 1)
        n_used = (pends[-1:] // MOE_BLOCK).astype(jnp.int32)

        buf = _dispatch_call(h2, dest, buf)
        yb = _expert_call(buf, block_e, n_used, w_gu, b_gu, w_dn, b_dn, layer)
        xf = _combine_call(yb, dest, tw, xf, g2, seq)

    return _final_norm_call(xf, final_norm_w).reshape(nb, seq, d)
```

```python
import functools

import numpy as np
import jax
import jax.numpy as jnp
from jax import lax
from jax.experimental import pallas as pl
from jax.experimental.pallas import tpu as pltpu

F32 = jnp.float32
BF16 = jnp.bfloat16

N_HEADS = 4
GLA_DK, GLA_DV, GLA_RANK, GLA_TAU = 48, 96, 16, 16.0
HG_DK, HG_DV = 64, 64
RET_DK, RET_DV = 48, 96
ROPE_BASE = 10000.0
N_EXPERTS, TOP_K = 32, 4
SWIGLU_LIMIT, SWIGLU_ALPHA = 7.0, 1.702
MOE_BLOCK = 512
ROUTE_TILE = 512
RUN_ALIGN = 8
EPS = 1e-6

LANE = 128
HALF = LANE // 2
N_PAIRS = N_HEADS // 2
HEAD_W = N_HEADS * LANE
PAIR_W = N_PAIRS * LANE
VMEM_LIMIT = 56 * 1024 * 1024
NEG_BIG = -1e30


def _cparams(sem):
    return pltpu.CompilerParams(dimension_semantics=sem, vmem_limit_bytes=VMEM_LIMIT)


def _dot(a, b):
    return jnp.dot(a, b, preferred_element_type=F32)


def _dot_nt(a, b):
    return lax.dot_general(a, b, (((1,), (1,)), ((), ())), preferred_element_type=F32)


def _dot_tn(a, b):
    return lax.dot_general(a, b, (((0,), (0,)), ((), ())), preferred_element_type=F32)


def _split2(a):
    hi = a.astype(BF16)
    return hi, (a - hi.astype(F32)).astype(BF16)


def _dot_f32(a, b):
    a_hi, a_lo = _split2(a)
    b_hi, b_lo = _split2(b)
    return _dot(a_hi, b_hi) + _dot(a_hi, b_lo) + _dot(a_lo, b_hi)


def _sigmoid(x):
    return 1.0 / (1.0 + jnp.exp(-x))


def _log_sigmoid(x):
    return jnp.minimum(x, 0.0) - jnp.log1p(jnp.exp(-jnp.abs(x)))


def _mod_kernel(c_ref, w_ref, b_ref, o_ref):
    c = c_ref[...]
    o_ref[0] = _dot_f32(c * _sigmoid(c), w_ref[0]) + b_ref[0]


def _mod_call(c, w_mod, b_mod):
    depth, d, d6 = w_mod.shape
    nb = c.shape[0]
    rows = 8
    c_pad = jnp.zeros((rows, d), F32).at[:nb].set(c)
    out = pl.pallas_call(
        _mod_kernel,
        grid=(depth, d6 // d),
        in_specs=[
            pl.BlockSpec((rows, d), lambda l, j: (0, 0)),
            pl.BlockSpec((1, d, d), lambda l, j: (l, 0, j)),
            pl.BlockSpec((1, 1, d), lambda l, j: (l, 0, j)),
        ],
        out_specs=pl.BlockSpec((1, rows, d), lambda l, j: (l, 0, j)),
        out_shape=jax.ShapeDtypeStruct((depth, rows, d6), F32),
        compiler_params=_cparams(("parallel", "parallel")),
        name="mod",
    )(c_pad, w_mod, b_mod.reshape(depth, 1, d6))
    return out[:, :nb]


def _rope_kernel(pos_ref, freq_ref, cos_ref, sin_ref):
    ang = pos_ref[0] * freq_ref[...]
    cos_ref[0] = jnp.cos(ang)
    sin_ref[0] = jnp.sin(ang)


def _rope_call(positions):
    nb, l = positions.shape
    half = RET_DK // 2
    inv = (ROPE_BASE ** (-np.arange(half, dtype=np.float32) / half)).astype(np.float32)
    freq = np.zeros((1, LANE), np.float32)
    for base in (0, HALF):
        freq[0, base:base + half] = inv
        freq[0, base + half:base + 2 * half] = inv
    tl = min(l, 512)
    pos = positions.astype(F32).reshape(nb, l, 1)
    shp = jax.ShapeDtypeStruct((nb, l, LANE), F32)
    return pl.pallas_call(
        _rope_kernel,
        grid=(nb, l // tl),
        in_specs=[pl.BlockSpec((1, tl, 1), lambda b, i: (b, i, 0)),
                  pl.BlockSpec((1, LANE), lambda b, i: (0, 0))],
        out_specs=[pl.BlockSpec((1, tl, LANE), lambda b, i: (b, i, 0))] * 2,
        out_shape=[shp, shp],
        compiler_params=_cparams(("parallel", "parallel")),
        name="rope_tables",
    )(pos, jnp.asarray(freq))


def _lb_kernel(p_ref, o_ref):
    p = p_ref[...]
    depth = p.shape[0]
    m = jnp.max(p, axis=0, keepdims=True)
    e = jnp.exp(p - m)
    sm = e / jnp.sum(e, axis=0, keepdims=True)
    acc = jnp.zeros_like(sm[0:1])
    for i in range(depth):
        if i > 0:
            acc = acc + sm[i:i + 1]
        o_ref[i:i + 1, :] = acc


def _lb_call(hg_lb):
    return pl.pallas_call(
        _lb_kernel,
        out_shape=jax.ShapeDtypeStruct(hg_lb.shape, F32),
        name="hg_lower_bounds",
    )(hg_lb.astype(F32))


def _inproj_kernel(x_ref, nw_ref, sc_ref, sh_ref, w_ref, *out_refs, widths):
    x = x_ref[...]
    y = x * lax.rsqrt(jnp.mean(x * x, axis=-1, keepdims=True) + EPS) * nw_ref[...]
    hb = (y * (1.0 + sc_ref[0]) + sh_ref[0]).astype(BF16)
    off = 0
    for o_ref, wd in zip(out_refs, widths):
        o_ref[...] = _dot(hb, w_ref[:, off:off + wd]).astype(o_ref.dtype)
        off += wd


def _inproj_call(x, norm_w, sc, sh, w_cat, layer, widths, dtypes, seq):
    t, d = x.shape
    tm = min(512, seq)
    per_b = seq // tm
    return pl.pallas_call(
        functools.partial(_inproj_kernel, widths=widths),
        grid=(t // tm,),
        in_specs=[
            pl.BlockSpec((tm, d), lambda i: (i, 0)),
            pl.BlockSpec((1, d), lambda i: (0, 0)),
            pl.BlockSpec((1, 1, d), lambda i: (i // per_b, 0, 0)),
            pl.BlockSpec((1, 1, d), lambda i: (i // per_b, 0, 0)),
            pl.BlockSpec((None,) + w_cat.shape[1:], lambda i: (layer, 0, 0)),
        ],
        out_specs=[pl.BlockSpec((tm, wd), lambda i: (i, 0)) for wd in widths],
        out_shape=[jax.ShapeDtypeStruct((t, wd), dt) for wd, dt in zip(widths, dtypes)],
        compiler_params=_cparams(("parallel",)),
        name="norm1_inproj",
    )(x, norm_w.reshape(1, d), sc, sh, w_cat)


def _level_tables(c):
    nl = int(np.log2(c))
    assert 1 << nl == c
    idx = np.arange(c)
    mats, masks = [], [np.eye(c, dtype=np.float32)]
    t = idx[None, :]
    i = idx[:, None]
    for lvl in range(nl):
        h = c >> (lvl + 1)
        blk, pos = idx // (2 * h), idx % (2 * h)
        m = (blk * 2 * h + h - 1)[:, None]
        right = (pos >= h)[:, None]
        a = np.where(right, (t > m) & (t <= i), (t > i) & (t <= m))
        mats.append(a.astype(np.float32))
        same = blk[:, None] == blk[None, :]
        masks.append((same & right & (pos < h)[None, :]).astype(np.float32))
    mats.append((t <= i).astype(np.float32))
    mats.append((t > i).astype(np.float32))
    return np.concatenate(mats, 0), np.stack(masks, 0), nl


def _pair_decays(g, mall_ref):
    gs = _dot(mall_ref[...], jnp.concatenate(_split2(g), axis=1))
    return jnp.exp(gs[:, :LANE] + gs[:, LANE:])


def _pair_keys(k, e, c, nl):
    levels = [(k * e[lvl * c:(lvl + 1) * c]).astype(BF16) for lvl in range(nl)]
    return k.astype(BF16), levels, (k * e[(nl + 1) * c:(nl + 2) * c]).astype(BF16)


def _gated_head(q, kb, klv, e, v, st, masks_ref, c, nl):
    scores = masks_ref[0] * _dot_nt(q.astype(BF16), kb)
    for lvl in range(nl):
        scores = scores + masks_ref[lvl + 1] * _dot_nt((q * e[lvl * c:(lvl + 1) * c]).astype(BF16), klv[lvl])
    q_in = (q * e[nl * c:(nl + 1) * c]).astype(BF16)
    return _dot(scores.astype(BF16), v) + _dot_nt(q_in, st.astype(BF16))


def _gla_kernel(q_ref, k_ref, v_ref, gg_ref, ga_ref, wa2_ref, ba2_ref, nw_ref, mall_ref, masks_ref,
                o_ref, st_ref, *, c, nl):
    @pl.when(pl.program_id(0) == 0)
    def _():
        st_ref[...] = jnp.zeros_like(st_ref)

    for b in range(q_ref.shape[0]):
        log_alpha = _log_sigmoid(_dot_f32(ga_ref[b], wa2_ref[...]) + ba2_ref[...]) * (1.0 / GLA_TAU)
        for pr in range(N_PAIRS):
            psl = slice(pr * LANE, (pr + 1) * LANE)
            e = _pair_decays(log_alpha[:, psl], mall_ref)
            kb, klv, k_end = _pair_keys(k_ref[b, :, psl].astype(F32), e, c, nl)
            dec = e[(nl + 1) * c - 1:(nl + 1) * c, :]
            for hd in (2 * pr, 2 * pr + 1):
                sl = slice(hd * LANE, (hd + 1) * LANE)
                q = q_ref[b, :, sl].astype(F32) * (GLA_DK ** -0.5)
                v = v_ref[b, :, sl]
                st = st_ref[b, hd]
                o = _gated_head(q, kb, klv, e, v, st, masks_ref, c, nl)
                st_ref[b, hd] = dec * st + _dot_tn(v, k_end)
                y = o * lax.rsqrt(jnp.sum(o * o, axis=-1, keepdims=True) * (1.0 / GLA_DV) + EPS) * nw_ref[:, sl]
                gate = gg_ref[b, :, sl].astype(F32)
                o_ref[b, :, sl] = (y * gate * _sigmoid(gate)).astype(o_ref.dtype)


def _hg_kernel(q_ref, f_ref, v_ref, og_ref, lb_ref, nw_ref, mall_ref, masks_ref,
               o_ref, st_ref, *, c, nl):
    @pl.when(pl.program_id(0) == 0)
    def _():
        st_ref[...] = jnp.zeros_like(st_ref)

    low = lax.broadcasted_iota(jnp.int32, (c, LANE), 1) < HALF
    for b in range(q_ref.shape[0]):
        for pr in range(N_PAIRS):
            psl = slice(pr * LANE, (pr + 1) * LANE)
            lb = lb_ref[:, psl]
            hf = f_ref[b, :, psl]
            la = jnp.log(lb)
            lc = jnp.log1p(-lb) + _log_sigmoid(hf)
            log_f = jnp.maximum(la, lc) + jnp.log1p(jnp.exp(-jnp.abs(la - lc)))
            e = _pair_decays(log_f, mall_ref)
            kb, klv, k_end = _pair_keys((1.0 - lb) * _sigmoid(-hf), e, c, nl)
            v = v_ref[b, :, psl]
            st = st_ref[b, pr]
            o_even = _gated_head(q_ref[b, :, (2 * pr) * LANE:(2 * pr + 1) * LANE].astype(F32),
                                 kb, klv, e, v, st, masks_ref, c, nl)
            o_odd = _gated_head(q_ref[b, :, (2 * pr + 1) * LANE:(2 * pr + 2) * LANE].astype(F32),
                                kb, klv, e, v, st, masks_ref, c, nl)
            o = jnp.where(low, o_even, o_odd)
            st_ref[b, pr] = e[(nl + 1) * c - 1:(nl + 1) * c, :] * st + _dot_tn(v, k_end)
            sq = o * o
            ms = jnp.where(low, jnp.sum(jnp.where(low, sq, 0.0), axis=-1, keepdims=True),
                           jnp.sum(jnp.where(low, 0.0, sq), axis=-1, keepdims=True)) * (1.0 / HG_DV)
            y = o * lax.rsqrt(ms + EPS) * nw_ref[:, psl]
            o_ref[b, :, psl] = (y * _sigmoid(og_ref[b, :, psl].astype(F32))).astype(o_ref.dtype)


def _tok_spec(nb, c, width):
    return pl.BlockSpec((nb, c, width), lambda i: (0, i, 0))


def _const_spec(shape):
    nd = len(shape)
    return pl.BlockSpec(shape, lambda i: (0,) * nd)


def _seq_view(a, nb, seq):
    return a.reshape(nb, seq, a.shape[-1])


def _gla_call(gq, gk, gv, gg, ga, wa2p, ba2p, nwp, nb, seq):
    c = min(128, seq)
    n = seq // c
    mall, masks, nl = _level_tables(c)
    sv = lambda a: _seq_view(a, nb, seq)
    return pl.pallas_call(
        functools.partial(_gla_kernel, c=c, nl=nl),
        grid=(n,),
        in_specs=[_tok_spec(nb, c, HEAD_W), _tok_spec(nb, c, PAIR_W), _tok_spec(nb, c, HEAD_W),
                  _tok_spec(nb, c, HEAD_W), _tok_spec(nb, c, LANE),
                  _const_spec(wa2p.shape), _const_spec(ba2p.shape), _const_spec(nwp.shape),
                  _const_spec(mall.shape), _const_spec(masks.shape)],
        out_specs=_tok_spec(nb, c, HEAD_W),
        out_shape=jax.ShapeDtypeStruct((nb, seq, HEAD_W), BF16),
        scratch_shapes=[pltpu.VMEM((nb, N_HEADS, LANE, LANE), F32)],
        compiler_params=_cparams(("arbitrary",)),
        name="gla_recurrence",
    )(sv(gq), sv(gk), sv(gv), sv(gg), sv(ga), wa2p, ba2p, nwp,
      jnp.asarray(mall, BF16), jnp.asarray(masks)).reshape(nb * seq, HEAD_W)


def _hg_call(hq, hf, hi, hg, lbp, nwp, nb, seq):
    c = min(128, seq)
    n = seq // c
    mall, masks, nl = _level_tables(c)
    sv = lambda a: _seq_view(a, nb, seq)
    return pl.pallas_call(
        functools.partial(_hg_kernel, c=c, nl=nl),
        grid=(n,),
        in_specs=[_tok_spec(nb, c, HEAD_W), _tok_spec(nb, c, PAIR_W), _tok_spec(nb, c, PAIR_W),
                  _tok_spec(nb, c, PAIR_W),
                  _const_spec(lbp.shape), _const_spec(nwp.shape),
                  _const_spec(mall.shape), _const_spec(masks.shape)],
        out_specs=_tok_spec(nb, c, PAIR_W),
        out_shape=jax.ShapeDtypeStruct((nb, seq, PAIR_W), BF16),
        scratch_shapes=[pltpu.VMEM((nb, N_PAIRS, LANE, LANE), F32)],
        compiler_params=_cparams(("arbitrary",)),
        name="hgrn2_recurrence",
    )(sv(hq), sv(hf), sv(hi), sv(hg), lbp, nwp,
      jnp.asarray(mall, BF16), jnp.asarray(masks)).reshape(nb * seq, PAIR_W)


def _ret_tables(c):
    hs = np.arange(N_HEADS, dtype=np.float64)
    log_gamma = np.log(1.0 - np.exp2(-5.0 - hs))
    idx = np.arange(c, dtype=np.float64)
    rel = idx[:, None] - idx[None, :]
    dmat = np.where(rel >= 0, np.exp(log_gamma[:, None, None] * np.maximum(rel, 0.0)), 0.0)
    qdec = np.exp(log_gamma[:, None] * (idx + 1.0))
    kdec = np.exp(log_gamma[:, None] * (c - 1.0 - idx))
    cdec = np.exp(log_gamma * c)
    qfull = np.broadcast_to(qdec[:, :, None], (N_HEADS, c, LANE)).astype(np.float32)
    kpair = np.zeros((N_PAIRS, c, LANE), np.float32)
    for hd in range(N_HEADS):
        kpair[hd // 2, :, (hd % 2) * HALF:(hd % 2 + 1) * HALF] = kdec[hd][:, None]
    return dmat.astype(np.float32), qfull, kpair, [float(np.float32(v)) for v in cdec]


def _ret_kernel(q_ref, k_ref, v_ref, og_ref, cos_ref, sin_ref, dmat_ref, qdec_ref, kdec_ref,
                o_ref, st_ref, *, c, cdec):
    @pl.when(pl.program_id(0) == 0)
    def _():
        st_ref[...] = jnp.zeros_like(st_ref)

    half = RET_DK // 2
    lane = lax.broadcasted_iota(jnp.int32, (c, LANE), 1)
    first = (lane & (HALF - 1)) < half

    for b in range(q_ref.shape[0]):
        cos = cos_ref[b]
        sin = sin_ref[b]

        def rotary(t):
            rot = jnp.where(first, -pltpu.roll(t, LANE - half, 1), pltpu.roll(t, half, 1))
            return t * cos + rot * sin

        for pr in range(N_PAIRS):
            psl = slice(pr * LANE, (pr + 1) * LANE)
            k = rotary(k_ref[b, :, psl].astype(F32))
            kb = k.astype(BF16)
            k_end = (k * kdec_ref[pr]).astype(BF16)
            for hd in (2 * pr, 2 * pr + 1):
                sl = slice(hd * LANE, (hd + 1) * LANE)
                q = rotary(q_ref[b, :, sl].astype(F32)) * (RET_DK ** -0.5)
                v = v_ref[b, :, sl]
                st = st_ref[b, hd]
                scores = _dot_nt(q.astype(BF16), kb) * dmat_ref[hd]
                o = _dot(scores.astype(BF16), v) + _dot_nt((q * qdec_ref[hd]).astype(BF16), st.astype(BF16))
                st_ref[b, hd] = cdec[hd] * st + _dot_tn(v, k_end)
                mu = jnp.sum(o, axis=-1, keepdims=True) * (1.0 / RET_DV)
                dlt = jnp.where(lane < RET_DV, o - mu, 0.0)
                var = jnp.sum(dlt * dlt, axis=-1, keepdims=True) * (1.0 / RET_DV)
                gate = og_ref[b, :, sl].astype(F32)
                o_ref[b, :, sl] = (dlt * lax.rsqrt(var + EPS) * gate * _sigmoid(gate)).astype(o_ref.dtype)


def _ret_call(rq, rk, rv, rg, cos, sin, nb, seq):
    c = min(128, seq)
    n = seq // c
    dmat, qdec, kdec, cdec = _ret_tables(c)
    sv = lambda a: _seq_view(a, nb, seq)
    return pl.pallas_call(
        functools.partial(_ret_kernel, c=c, cdec=cdec),
        grid=(n,),
        in_specs=[_tok_spec(nb, c, HEAD_W), _tok_spec(nb, c, PAIR_W), _tok_spec(nb, c, HEAD_W),
                  _tok_spec(nb, c, HEAD_W), _tok_spec(nb, c, LANE), _tok_spec(nb, c, LANE),
                  _const_spec(dmat.shape), _const_spec(qdec.shape), _const_spec(kdec.shape)],
        out_specs=_tok_spec(nb, c, HEAD_W),
        out_shape=jax.ShapeDtypeStruct((nb, seq, HEAD_W), BF16),
        scratch_shapes=[pltpu.VMEM((nb, N_HEADS, LANE, LANE), F32)],
        compiler_params=_cparams(("arbitrary",)),
        name="retention",
    )(sv(rq), sv(rk), sv(rv), sv(rg), cos, sin,
      jnp.asarray(dmat), jnp.asarray(qdec), jnp.asarray(kdec)).reshape(nb * seq, HEAD_W)


def _outproj_router_kernel(x_ref, oa_ref, ob_ref, oc_ref, wo_ref, g1_ref, nw_ref, sc_ref, sh_ref,
                           wr_ref, br_ref, tri_ref, upper_ref,
                           xo_ref, h_ref, te_ref, tw_ref, rk_ref, lo_ref, tab_ref, cnt_ref, *, tm):
    @pl.when(pl.program_id(0) == 0)
    def _():
        cnt_ref[...] = jnp.zeros_like(cnt_ref)

    wa, wb = oa_ref.shape[1], ob_ref.shape[1]
    mix = (_dot(oa_ref[...], wo_ref[0:wa, :]) + _dot(ob_ref[...], wo_ref[wa:wa + wb, :])
           + _dot(oc_ref[...], wo_ref[wa + wb:, :]))
    x = x_ref[...] + g1_ref[0] * mix
    xo_ref[...] = x
    y = x * lax.rsqrt(jnp.mean(x * x, axis=-1, keepdims=True) + EPS) * nw_ref[...]
    h = y * (1.0 + sc_ref[0]) + sh_ref[0]
    h_ref[...] = h

    lg = _dot_f32(h, wr_ref[...]) + br_ref[...]
    lane = lax.broadcasted_iota(jnp.int32, (tm, LANE), 1)
    sel_e, sel_v = [], []
    for _ in range(TOP_K):
        m = jnp.max(lg, axis=-1, keepdims=True)
        idx = jnp.min(jnp.where(lg == m, lane, LANE), axis=-1, keepdims=True)
        sel_e.append(idx)
        sel_v.append(m)
        lg = jnp.where(lane == idx, -jnp.inf, lg)
    ex = [jnp.exp(v - sel_v[0]) for v in sel_v]
    den = ex[0] + ex[1] + ex[2] + ex[3]
    hot = [(lane == idx) for idx in sel_e]
    onehot = jnp.zeros((tm, LANE), F32)
    for hk in hot:
        onehot = onehot + jnp.where(hk, 1.0, 0.0)
    in_tile = _dot(tri_ref[...], onehot.astype(BF16))
    earlier = cnt_ref[...]
    before = in_tile + earlier
    tile_cnt = jnp.floor((jnp.sum(onehot, axis=0, keepdims=True) + (RUN_ALIGN - 1.0)) * (1.0 / RUN_ALIGN)) * RUN_ALIGN
    tile_start = _dot_f32(jnp.broadcast_to(tile_cnt, (8, LANE)), upper_ref[...])[0:1]
    local = tile_start + in_tile
    te = jnp.zeros((tm, LANE), jnp.int32)
    tw = jnp.zeros((tm, LANE), F32)
    rk = jnp.zeros((tm, LANE), jnp.int32)
    lo = jnp.zeros((tm, LANE), jnp.int32)
    for kk in range(TOP_K):
        rank = jnp.sum(jnp.where(hot[kk], before, 0.0), axis=-1, keepdims=True).astype(jnp.int32)
        lrow = jnp.sum(jnp.where(hot[kk], local, 0.0), axis=-1, keepdims=True).astype(jnp.int32)
        te = jnp.where(lane == kk, sel_e[kk], te)
        tw = jnp.where(lane == kk, ex[kk] / den, tw)
        rk = jnp.where(lane == kk, rank, rk)
        lo = jnp.where(lane == kk, lrow, lo)
    te_ref[...] = te
    tw_ref[...] = tw
    rk_ref[...] = rk
    lo_ref[...] = lo
    tab_ref[0, 0:1, :] = tile_start.astype(jnp.int32)
    tab_ref[0, 1:2, :] = tile_cnt.astype(jnp.int32)
    tab_ref[0, 2:3, :] = earlier.astype(jnp.int32)
    tab_ref[0, 3:8, :] = jnp.zeros((5, LANE), jnp.int32)
    cnt_ref[...] = earlier + tile_cnt


def _outproj_router_call(x, oa, ob, oc, wo, layer, g1, nw, sc, sh, wr, br, seq):
    t, d = x.shape
    tm = min(ROUTE_TILE, seq)
    per_b = seq // tm
    tri = np.tril(np.ones((tm, tm), np.float32), -1)
    upper = np.triu(np.ones((LANE, LANE), np.float32), 1)
    row = lambda w: pl.BlockSpec((tm, w), lambda i: (i, 0))
    const = lambda shape: pl.BlockSpec(shape, lambda i: (0,) * len(shape))
    perb = pl.BlockSpec((1, 1, d), lambda i: (i // per_b, 0, 0))
    n_tiles = t // tm
    return pl.pallas_call(
        functools.partial(_outproj_router_kernel, tm=tm),
        grid=(n_tiles,),
        in_specs=[row(d), row(oa.shape[1]), row(ob.shape[1]), row(oc.shape[1]),
                  pl.BlockSpec((None,) + wo.shape[1:], lambda i: (layer, 0, 0)), perb,
                  const((1, d)), perb, perb, const(wr.shape), const(br.shape), const(tri.shape),
                  const(upper.shape)],
        out_specs=[row(d), row(d), row(LANE), row(LANE), row(LANE), row(LANE),
                   pl.BlockSpec((1, 8, LANE), lambda i: (i, 0, 0)), const((1, LANE))],
        out_shape=[jax.ShapeDtypeStruct((t, d), F32), jax.ShapeDtypeStruct((t, d), F32),
                   jax.ShapeDtypeStruct((t, LANE), jnp.int32), jax.ShapeDtypeStruct((t, LANE), F32),
                   jax.ShapeDtypeStruct((t, LANE), jnp.int32), jax.ShapeDtypeStruct((t, LANE), jnp.int32),
                   jax.ShapeDtypeStruct((n_tiles, 8, LANE), jnp.int32), jax.ShapeDtypeStruct((1, LANE), F32)],
        compiler_params=_cparams(("arbitrary",)),
        name="outproj_norm2_router",
    )(x, oa, ob, oc, wo, g1, nw.reshape(1, d), sc, sh, wr, br, jnp.asarray(tri, BF16), jnp.asarray(upper))


def _dispatch_kernel(dest_ref, h_ref, buf_in_ref, buf_ref, sem, *, tm):
    del buf_in_ref

    def issue(r, carry):
        for kk in range(TOP_K):
            pltpu.make_async_copy(h_ref.at[pl.ds(r, 1)],
                                  buf_ref.at[pl.ds(dest_ref[r * TOP_K + kk], 1)], sem).start()
        return carry

    lax.fori_loop(0, tm, issue, 0, unroll=8)
    for _ in range(TOP_K):
        pltpu.make_async_copy(h_ref, buf_ref.at[pl.ds(0, tm)], sem).wait()


def _dispatch_call(h, dest_flat, buf):
    t, d = h.shape
    n_slots = buf.shape[0]
    tm = min(256, t)
    return pl.pallas_call(
        functools.partial(_dispatch_kernel, tm=tm),
        grid=(t // tm,),
        in_specs=[pl.BlockSpec((tm * TOP_K,), lambda i: (i,), memory_space=pltpu.SMEM),
                  pl.BlockSpec((tm, d), lambda i: (i, 0)),
                  pl.BlockSpec(memory_space=pl.ANY)],
        out_specs=pl.BlockSpec(memory_space=pl.ANY),
        out_shape=jax.ShapeDtypeStruct((n_slots, d), F32),
        scratch_shapes=[pltpu.SemaphoreType.DMA(())],
        input_output_aliases={2: 0},
        compiler_params=_cparams(("arbitrary",)),
        name="moe_dispatch",
    )(dest_flat, h, buf)


def _expert_kernel(be_ref, nu_ref, x_ref, wgu_ref, bgu_ref, wdn_ref, bdn_ref, o_ref, wgu_bf, wdn_bf, *, d_ff):
    i = pl.program_id(0)
    live = i < nu_ref[0]

    @pl.when(jnp.logical_and(live, jnp.logical_or(i == 0, be_ref[i] != be_ref[jnp.maximum(i - 1, 0)])))
    def _():
        wgu_bf[...] = wgu_ref[0, 0].astype(BF16)
        wdn_bf[...] = wdn_ref[0, 0].astype(BF16)

    @pl.when(live)
    def _():
        gu = _dot(x_ref[...].astype(BF16), wgu_bf[...]) + bgu_ref[0, 0]
        gate = jnp.minimum(gu[:, :d_ff], SWIGLU_LIMIT)
        up = jnp.clip(gu[:, d_ff:], -SWIGLU_LIMIT, SWIGLU_LIMIT)
        act = (up + 1.0) * gate * _sigmoid(SWIGLU_ALPHA * gate)
        o_ref[...] = _dot(act.astype(BF16), wdn_bf[...]) + bdn_ref[0, 0]

    @pl.when(jnp.logical_not(live))
    def _():
        o_ref[...] = jnp.zeros_like(o_ref)


def _expert_call(buf, block_e, n_used, wgu, bgu, wdn, bdn, layer):
    n_slots, d = buf.shape
    n_blocks = n_slots // MOE_BLOCK
    depth, ne, _, f2 = wgu.shape
    d_ff = f2 // 2
    blk = lambda i, be, nu: (jnp.minimum(i, nu[0] - 1), 0)
    exp4 = lambda i, be, nu: (layer, be[jnp.minimum(i, nu[0] - 1)], 0, 0)
    grid_spec = pltpu.PrefetchScalarGridSpec(
        num_scalar_prefetch=2,
        grid=(n_blocks,),
        in_specs=[pl.BlockSpec((MOE_BLOCK, d), blk),
                  pl.BlockSpec((1, 1, d, f2), exp4),
                  pl.BlockSpec((1, 1, 1, f2), exp4),
                  pl.BlockSpec((1, 1, d_ff, d), exp4),
                  pl.BlockSpec((1, 1, 1, d), exp4)],
        out_specs=pl.BlockSpec((MOE_BLOCK, d), lambda i, be, nu: (i, 0)),
        scratch_shapes=[pltpu.VMEM((d, f2), BF16), pltpu.VMEM((d_ff, d), BF16)],
    )
    return pl.pallas_call(
        functools.partial(_expert_kernel, d_ff=d_ff),
        grid_spec=grid_spec,
        out_shape=jax.ShapeDtypeStruct((n_slots, d), F32),
        compiler_params=_cparams(("arbitrary",)),
        name="moe_experts",
    )(block_e, n_used, buf, wgu, bgu.reshape(depth, ne, 1, f2), wdn, bdn.reshape(depth, ne, 1, d))


def _combine_kernel(tab_ref, src_ref, yb_ref, lo_ref, w_ref, x_ref, g2_ref, o_ref, rs_ref, sem, *, tm):
    i = pl.program_id(0)
    n = pl.num_programs(0)
    rows = rs_ref.shape[1]
    sizes = [s for s in (1 << p for p in range(tm.bit_length() - 1, -1, -1)) if s >= RUN_ALIGN]

    def run_copies(tile, slot, act):
        def per_expert(e, carry):
            first = tab_ref[tile, 0, e]
            length = tab_ref[tile, 1, e]
            src = src_ref[tile, e]
            done = jnp.int32(0)
            for size in sizes:
                part = length & size

                @pl.when(part != 0)
                def _():
                    cp = pltpu.make_async_copy(
                        yb_ref.at[pl.ds(pl.multiple_of(src + done, RUN_ALIGN), size)],
                        rs_ref.at[slot, pl.ds(pl.multiple_of(first + done, RUN_ALIGN), size)], sem.at[slot])
                    act(cp)
                done = done + part
            return carry

        lax.fori_loop(0, N_EXPERTS, per_expert, 0)

    @pl.when(i == 0)
    def _():
        rs_ref[...] = jnp.zeros_like(rs_ref)
        run_copies(0, 0, lambda cp: cp.start())

    @pl.when(i + 1 < n)
    def _():
        run_copies(i + 1, (i + 1) % 2, lambda cp: cp.start())

    slot = i % 2
    run_copies(i, slot, lambda cp: cp.wait())
    r = rs_ref[slot].astype(BF16)
    col = lax.broadcasted_iota(jnp.int32, (tm, rows), 1)
    w = w_ref[...]
    lo = lo_ref[...]
    pw = jnp.zeros((tm, rows), F32)
    for kk in range(TOP_K):
        pw = jnp.where(col == lo[:, kk:kk + 1], w[:, kk:kk + 1], pw)
    pw_hi, pw_lo = _split2(pw)
    y = _dot(pw_hi, r) + _dot(pw_lo, r)
    o_ref[...] = x_ref[...] + g2_ref[0] * y


def _combine_call(yb, tab, src, lo, tw, x, g2, seq):
    t, d = x.shape
    tm = min(ROUTE_TILE, seq)
    per_b = seq // tm
    grid_spec = pltpu.PrefetchScalarGridSpec(
        num_scalar_prefetch=2,
        grid=(t // tm,),
        in_specs=[pl.BlockSpec(memory_space=pl.ANY),
                  pl.BlockSpec((tm, LANE), lambda i, tb, sr: (i, 0)),
                  pl.BlockSpec((tm, LANE), lambda i, tb, sr: (i, 0)),
                  pl.BlockSpec((tm, d), lambda i, tb, sr: (i, 0)),
                  pl.BlockSpec((1, 1, d), lambda i, tb, sr: (i // per_b, 0, 0))],
        out_specs=pl.BlockSpec((tm, d), lambda i, tb, sr: (i, 0)),
        scratch_shapes=[pltpu.VMEM((2, TOP_K * tm + N_EXPERTS * RUN_ALIGN, d), F32),
                        pltpu.SemaphoreType.DMA((2,))],
    )
    return pl.pallas_call(
        functools.partial(_combine_kernel, tm=tm),
        grid_spec=grid_spec,
        out_shape=jax.ShapeDtypeStruct((t, d), F32),
        compiler_params=_cparams(("arbitrary",)),
        name="moe_combine",
    )(tab, src, yb, lo, tw, x, g2)


def _final_norm_kernel(x_ref, w_ref, o_ref):
    x = x_ref[...]
    o_ref[...] = x * lax.rsqrt(jnp.mean(x * x, axis=-1, keepdims=True) + EPS) * w_ref[...]


def _final_norm_call(x, w):
    t, d = x.shape
    tm = min(512, t)
    return pl.pallas_call(
        _final_norm_kernel,
        grid=(t // tm,),
        in_specs=[pl.BlockSpec((tm, d), lambda i: (i, 0)), pl.BlockSpec((1, d), lambda i: (0, 0))],
        out_specs=pl.BlockSpec((tm, d), lambda i: (i, 0)),
        out_shape=jax.ShapeDtypeStruct((t, d), F32),
        compiler_params=_cparams(("parallel",)),
        name="final_norm",
    )(x, w.reshape(1, d))


def _split_heads(w, hd):
    return w.reshape(w.shape[:-1] + (N_HEADS, hd))


def _pad_last(w, lo, hi):
    return jnp.pad(w, [(0, 0)] * (w.ndim - 1) + [(lo, hi)])


def _tile_cols(w, hd):
    return _pad_last(_split_heads(w, hd), 0, LANE - hd).reshape(w.shape[:-1] + (HEAD_W,))


def _pair_cols(w, hd):
    return _pad_last(_split_heads(w, hd), 0, HALF - hd).reshape(w.shape[:-1] + (PAIR_W,))


def _query_cols(w, hd):
    w = _split_heads(w, hd)
    tiles = [_pad_last(w[..., h, :], (h % 2) * HALF, LANE - (h % 2) * HALF - hd) for h in range(N_HEADS)]
    return jnp.concatenate(tiles, axis=-1)


def _tile_rows(w, hd):
    return jnp.swapaxes(_tile_cols(jnp.swapaxes(w, -1, -2), hd), -1, -2)


IN_PARTS = (("gq", GLA_DK, _query_cols), ("gk", GLA_DK, _pair_cols), ("gv", GLA_DV, _tile_cols),
            ("gg", GLA_DV, _tile_cols), ("ga", None, None),
            ("hq", HG_DK, _query_cols), ("hf", HG_DK, _pair_cols), ("hi", HG_DV, _pair_cols),
            ("hg", HG_DV, _pair_cols),
            ("rq", RET_DK, _query_cols), ("rk", RET_DK, _pair_cols), ("rv", RET_DV, _tile_cols),
            ("rg", RET_DV, _tile_cols))
F32_PARTS = ("ga", "hf")


def _layout_w_in(w_in):
    cols, off = [], 0
    for _, hd, layout in IN_PARTS:
        if layout is None:
            part = _pad_last(w_in[..., off:off + GLA_RANK], 0, LANE - GLA_RANK)
            off += GLA_RANK
        else:
            part = layout(w_in[..., off:off + N_HEADS * hd], hd)
            off += N_HEADS * hd
        cols.append(part)
    return jnp.concatenate(cols, axis=-1).astype(BF16), tuple(int(p.shape[-1]) for p in cols)


def kernel(x, c, positions, w_mod, b_mod, norm1_w, w_in, gla_wa2, gla_ba2, hg_lb, gla_norm_w, hg_norm_w,
           w_out, norm2_w, w_r, b_r, w_gu, b_gu, w_dn, b_dn, final_norm_w):
    nb, seq, d = x.shape
    t = nb * seq
    depth = w_mod.shape[0]
    dtypes = tuple(F32 if name in F32_PARTS else BF16 for name, _, _ in IN_PARTS)

    mod = _mod_call(c, w_mod, b_mod)
    cos, sin = _rope_call(positions)
    lb_all = _lb_call(hg_lb)

    n_assign = t * TOP_K
    n_run_pad = N_EXPERTS * (RUN_ALIGN - 1) * (t // min(ROUTE_TILE, seq))
    n_blocks = (n_assign + n_run_pad + MOE_BLOCK - 1) // MOE_BLOCK + N_EXPERTS
    n_slots = n_blocks * MOE_BLOCK

    w_cat, widths = _layout_w_in(w_in)
    n_gla, n_hg = N_HEADS * GLA_DV, N_HEADS * HG_DV
    wo_p = jnp.concatenate([_tile_rows(w_out[:, :n_gla], GLA_DV), w_out[:, n_gla:n_gla + n_hg],
                            _tile_rows(w_out[:, n_gla + n_hg:], RET_DV)], axis=1).astype(BF16)
    wa2p = jnp.pad(_pair_cols(gla_wa2, GLA_DK), ((0, 0), (0, LANE - GLA_RANK), (0, 0)))
    ba2p = _pair_cols(gla_ba2, GLA_DK)
    gnw = _tile_cols(gla_norm_w, GLA_DV)
    wr_p = _pad_last(w_r, 0, LANE - N_EXPERTS)
    br_p = jnp.pad(b_r, ((0, 0), (0, LANE - N_EXPERTS)), constant_values=NEG_BIG)
    expert_ids = jnp.arange(N_EXPERTS, dtype=jnp.int32)

    buf = jnp.zeros((n_slots, d), F32)
    xf = x.reshape(t, d)
    for layer in range(depth):
        sh1, sc1, g1, sh2, sc2, g2 = [m.reshape(nb, 1, d) for m in jnp.split(mod[layer], 6, axis=-1)]

        parts = _inproj_call(xf, norm1_w[layer], sc1, sh1, w_cat, layer, widths, dtypes, seq)
        gq, gk, gv, gg, ga, hq, hf, hi, hg, rq, rk, rv, rg = parts
        o_gla = _gla_call(gq, gk, gv, gg, ga, wa2p[layer], ba2p[layer:layer + 1], gnw[layer:layer + 1], nb, seq)
        o_hg = _hg_call(hq, hf, hi, hg, lb_all[layer:layer + 1], hg_norm_w[layer:layer + 1], nb, seq)
        o_ret = _ret_call(rq, rk, rv, rg, cos, sin, nb, seq)
        xf, h2, te, tw, rk_, lo, tab, cnt = _outproj_router_call(
            xf, o_gla, o_hg, o_ret, wo_p, layer, g1, norm2_w[layer], sc2, sh2, wr_p[layer],
            br_p[layer:layer + 1], seq)

        counts = cnt[0, :N_EXPERTS].astype(jnp.int32)
        padded = (counts + MOE_BLOCK - 1) // MOE_BLOCK * MOE_BLOCK
        pends = jnp.cumsum(padded)
        pstarts = pends - padded
        top_e = te[:, :TOP_K]
        start_of = jnp.sum(jnp.where(top_e[:, :, None] == expert_ids, pstarts, 0), axis=-1)
        dest = (start_of + rk_[:, :TOP_K]).reshape(-1).astype(jnp.int32)
        block_start = jnp.arange(n_blocks, dtype=jnp.int32) * MOE_BLOCK
        block_e = jnp.minimum(jnp.sum((pends[None, :] <= block_start[:, None]).astype(jnp.int32), axis=1),
                              N_EXPERTS - 1)
        n_used = (pends[-1:] // MOE_BLOCK).astype(jnp.int32)
        run_src = tab[:, 2, :] + jnp.pad(pstarts, (0, LANE - N_EXPERTS))[None, :]

        buf = _dispatch_call(h2, dest, buf)
        yb = _expert_call(buf, block_e, n_used, w_gu, b_gu, w_dn, b_dn, layer)
        xf = _combine_call(yb, tab, run_src, lo, tw, xf, g2, seq)

    return _final_norm_call(xf, final_norm_w).reshape(nb, seq, d)
```

```python
import functools

import numpy as np
import jax
import jax.numpy as jnp
from jax import lax
from jax.experimental import pallas as pl
from jax.experimental.pallas import tpu as pltpu

F32 = jnp.float32
BF16 = jnp.bfloat16

N_HEADS = 4
GLA_DK, GLA_DV, GLA_RANK, GLA_TAU = 48, 96, 16, 16.0
HG_DK, HG_DV = 64, 64
RET_DK, RET_DV = 48, 96
ROPE_BASE = 10000.0
N_EXPERTS, TOP_K = 32, 4
SWIGLU_LIMIT, SWIGLU_ALPHA = 7.0, 1.702
MOE_BLOCK = 512
ROUTE_TILE = 512
RUN_ALIGN = 8
EPS = 1e-6

LANE = 128
HALF = LANE // 2
N_PAIRS = N_HEADS // 2
HEAD_W = N_HEADS * LANE
PAIR_W = N_PAIRS * LANE
VMEM_LIMIT = 56 * 1024 * 1024
NEG_BIG = -1e30


def _cparams(sem):
    return pltpu.CompilerParams(dimension_semantics=sem, vmem_limit_bytes=VMEM_LIMIT)


def _dot(a, b):
    return jnp.dot(a, b, preferred_element_type=F32)


def _dot_nt(a, b):
    return lax.dot_general(a, b, (((1,), (1,)), ((), ())), preferred_element_type=F32)


def _dot_tn(a, b):
    return lax.dot_general(a, b, (((0,), (0,)), ((), ())), preferred_element_type=F32)


def _split2(a):
    hi = a.astype(BF16)
    return hi, (a - hi.astype(F32)).astype(BF16)


def _dot_f32(a, b):
    a_hi, a_lo = _split2(a)
    b_hi, b_lo = _split2(b)
    return _dot(a_hi, b_hi) + _dot(a_hi, b_lo) + _dot(a_lo, b_hi)


def _sigmoid(x):
    return 1.0 / (1.0 + jnp.exp(-x))


def _log_sigmoid(x):
    return jnp.minimum(x, 0.0) - jnp.log1p(jnp.exp(-jnp.abs(x)))


def _mod_kernel(c_ref, w_ref, b_ref, o_ref):
    c = c_ref[...]
    o_ref[0] = _dot_f32(c * _sigmoid(c), w_ref[0]) + b_ref[0]


def _mod_call(c, w_mod, b_mod):
    depth, d, d6 = w_mod.shape
    nb = c.shape[0]
    rows = 8
    c_pad = jnp.zeros((rows, d), F32).at[:nb].set(c)
    out = pl.pallas_call(
        _mod_kernel,
        grid=(depth, d6 // d),
        in_specs=[
            pl.BlockSpec((rows, d), lambda l, j: (0, 0)),
            pl.BlockSpec((1, d, d), lambda l, j: (l, 0, j)),
            pl.BlockSpec((1, 1, d), lambda l, j: (l, 0, j)),
        ],
        out_specs=pl.BlockSpec((1, rows, d), lambda l, j: (l, 0, j)),
        out_shape=jax.ShapeDtypeStruct((depth, rows, d6), F32),
        compiler_params=_cparams(("parallel", "parallel")),
        name="mod",
    )(c_pad, w_mod, b_mod.reshape(depth, 1, d6))
    return out[:, :nb]


def _rope_kernel(pos_ref, freq_ref, cos_ref, sin_ref):
    ang = pos_ref[0] * freq_ref[...]
    cos_ref[0] = jnp.cos(ang)
    sin_ref[0] = jnp.sin(ang)


def _rope_call(positions):
    nb, l = positions.shape
    half = RET_DK // 2
    inv = (ROPE_BASE ** (-np.arange(half, dtype=np.float32) / half)).astype(np.float32)
    freq = np.zeros((1, LANE), np.float32)
    for base in (0, HALF):
        freq[0, base:base + half] = inv
        freq[0, base + half:base + 2 * half] = inv
    tl = min(l, 512)
    pos = positions.astype(F32).reshape(nb, l, 1)
    shp = jax.ShapeDtypeStruct((nb, l, LANE), F32)
    return pl.pallas_call(
        _rope_kernel,
        grid=(nb, l // tl),
        in_specs=[pl.BlockSpec((1, tl, 1), lambda b, i: (b, i, 0)),
                  pl.BlockSpec((1, LANE), lambda b, i: (0, 0))],
        out_specs=[pl.BlockSpec((1, tl, LANE), lambda b, i: (b, i, 0))] * 2,
        out_shape=[shp, shp],
        compiler_params=_cparams(("parallel", "parallel")),
        name="rope_tables",
    )(pos, jnp.asarray(freq))


def _lb_kernel(p_ref, o_ref):
    p = p_ref[...]
    depth = p.shape[0]
    m = jnp.max(p, axis=0, keepdims=True)
    e = jnp.exp(p - m)
    sm = e / jnp.sum(e, axis=0, keepdims=True)
    acc = jnp.zeros_like(sm[0:1])
    for i in range(depth):
        if i > 0:
            acc = acc + sm[i:i + 1]
        o_ref[i:i + 1, :] = acc


def _lb_call(hg_lb):
    return pl.pallas_call(
        _lb_kernel,
        out_shape=jax.ShapeDtypeStruct(hg_lb.shape, F32),
        name="hg_lower_bounds",
    )(hg_lb.astype(F32))


def _inproj_kernel(x_ref, nw_ref, sc_ref, sh_ref, w_ref, *out_refs, widths):
    x = x_ref[...]
    y = x * lax.rsqrt(jnp.mean(x * x, axis=-1, keepdims=True) + EPS) * nw_ref[...]
    hb = (y * (1.0 + sc_ref[0]) + sh_ref[0]).astype(BF16)
    off = 0
    for o_ref, wd in zip(out_refs, widths):
        o_ref[...] = _dot(hb, w_ref[:, off:off + wd]).astype(o_ref.dtype)
        off += wd


def _inproj_call(x, norm_w, sc, sh, w_cat, layer, widths, dtypes, seq):
    t, d = x.shape
    tm = min(512, seq)
    per_b = seq // tm
    return pl.pallas_call(
        functools.partial(_inproj_kernel, widths=widths),
        grid=(t // tm,),
        in_specs=[
            pl.BlockSpec((tm, d), lambda i: (i, 0)),
            pl.BlockSpec((1, d), lambda i: (0, 0)),
            pl.BlockSpec((1, 1, d), lambda i: (i // per_b, 0, 0)),
            pl.BlockSpec((1, 1, d), lambda i: (i // per_b, 0, 0)),
            pl.BlockSpec((None,) + w_cat.shape[1:], lambda i: (layer, 0, 0)),
        ],
        out_specs=[pl.BlockSpec((tm, wd), lambda i: (i, 0)) for wd in widths],
        out_shape=[jax.ShapeDtypeStruct((t, wd), dt) for wd, dt in zip(widths, dtypes)],
        compiler_params=_cparams(("parallel",)),
        name="norm1_inproj",
    )(x, norm_w.reshape(1, d), sc, sh, w_cat)


def _level_tables(c):
    nl = int(np.log2(c))
    assert 1 << nl == c
    idx = np.arange(c)
    mats, masks = [], [np.eye(c, dtype=np.float32)]
    t = idx[None, :]
    i = idx[:, None]
    for lvl in range(nl):
        h = c >> (lvl + 1)
        blk, pos = idx // (2 * h), idx % (2 * h)
        m = (blk * 2 * h + h - 1)[:, None]
        right = (pos >= h)[:, None]
        a = np.where(right, (t > m) & (t <= i), (t > i) & (t <= m))
        mats.append(a.astype(np.float32))
        same = blk[:, None] == blk[None, :]
        masks.append((same & right & (pos < h)[None, :]).astype(np.float32))
    mats.append((t <= i).astype(np.float32))
    mats.append((t > i).astype(np.float32))
    return np.concatenate(mats, 0), np.stack(masks, 0), nl


def _pair_decays(g, mall_ref):
    gs = _dot(mall_ref[...], jnp.concatenate(_split2(g), axis=1))
    return jnp.exp(gs[:, :LANE] + gs[:, LANE:])


def _pair_keys(k, e, c, nl):
    levels = [(k * e[lvl * c:(lvl + 1) * c]).astype(BF16) for lvl in range(nl)]
    return k.astype(BF16), levels, (k * e[(nl + 1) * c:(nl + 2) * c]).astype(BF16)


def _gated_head(q, kb, klv, e, v, st, masks_ref, c, nl):
    scores = masks_ref[0] * _dot_nt(q.astype(BF16), kb)
    for lvl in range(nl):
        scores = scores + masks_ref[lvl + 1] * _dot_nt((q * e[lvl * c:(lvl + 1) * c]).astype(BF16), klv[lvl])
    q_in = (q * e[nl * c:(nl + 1) * c]).astype(BF16)
    return _dot(scores.astype(BF16), v) + _dot_nt(q_in, st.astype(BF16))


def _gla_kernel(q_ref, k_ref, v_ref, gg_ref, ga_ref, wa2_ref, ba2_ref, nw_ref, mall_ref, masks_ref,
                o_ref, st_ref, *, c, nl):
    @pl.when(pl.program_id(0) == 0)
    def _():
        st_ref[...] = jnp.zeros_like(st_ref)

    for b in range(q_ref.shape[0]):
        log_alpha = _log_sigmoid(_dot_f32(ga_ref[b], wa2_ref[...]) + ba2_ref[...]) * (1.0 / GLA_TAU)
        for pr in range(N_PAIRS):
            psl = slice(pr * LANE, (pr + 1) * LANE)
            e = _pair_decays(log_alpha[:, psl], mall_ref)
            kb, klv, k_end = _pair_keys(k_ref[b, :, psl].astype(F32), e, c, nl)
            dec = e[(nl + 1) * c - 1:(nl + 1) * c, :]
            for hd in (2 * pr, 2 * pr + 1):
                sl = slice(hd * LANE, (hd + 1) * LANE)
                q = q_ref[b, :, sl].astype(F32) * (GLA_DK ** -0.5)
                v = v_ref[b, :, sl]
                st = st_ref[b, hd]
                o = _gated_head(q, kb, klv, e, v, st, masks_ref, c, nl)
                st_ref[b, hd] = dec * st + _dot_tn(v, k_end)
                y = o * lax.rsqrt(jnp.sum(o * o, axis=-1, keepdims=True) * (1.0 / GLA_DV) + EPS) * nw_ref[:, sl]
                gate = gg_ref[b, :, sl].astype(F32)
                o_ref[b, :, sl] = (y * gate * _sigmoid(gate)).astype(o_ref.dtype)


def _hg_kernel(q_ref, f_ref, v_ref, og_ref, lb_ref, nw_ref, mall_ref, masks_ref,
               o_ref, st_ref, *, c, nl):
    @pl.when(pl.program_id(0) == 0)
    def _():
        st_ref[...] = jnp.zeros_like(st_ref)

    low = lax.broadcasted_iota(jnp.int32, (c, LANE), 1) < HALF
    for b in range(q_ref.shape[0]):
        for pr in range(N_PAIRS):
            psl = slice(pr * LANE, (pr + 1) * LANE)
            lb = lb_ref[:, psl]
            hf = f_ref[b, :, psl]
            la = jnp.log(lb)
            lc = jnp.log1p(-lb) + _log_sigmoid(hf)
            log_f = jnp.maximum(la, lc) + jnp.log1p(jnp.exp(-jnp.abs(la - lc)))
            e = _pair_decays(log_f, mall_ref)
            kb, klv, k_end = _pair_keys((1.0 - lb) * _sigmoid(-hf), e, c, nl)
            v = v_ref[b, :, psl]
            st = st_ref[b, pr]
            o_even = _gated_head(q_ref[b, :, (2 * pr) * LANE:(2 * pr + 1) * LANE].astype(F32),
                                 kb, klv, e, v, st, masks_ref, c, nl)
            o_odd = _gated_head(q_ref[b, :, (2 * pr + 1) * LANE:(2 * pr + 2) * LANE].astype(F32),
                                kb, klv, e, v, st, masks_ref, c, nl)
            o = jnp.where(low, o_even, o_odd)
            st_ref[b, pr] = e[(nl + 1) * c - 1:(nl + 1) * c, :] * st + _dot_tn(v, k_end)
            sq = o * o
            ms = jnp.where(low, jnp.sum(jnp.where(low, sq, 0.0), axis=-1, keepdims=True),
                           jnp.sum(jnp.where(low, 0.0, sq), axis=-1, keepdims=True)) * (1.0 / HG_DV)
            y = o * lax.rsqrt(ms + EPS) * nw_ref[:, psl]
            o_ref[b, :, psl] = (y * _sigmoid(og_ref[b, :, psl].astype(F32))).astype(o_ref.dtype)


def _tok_spec(nb, c, width):
    return pl.BlockSpec((nb, c, width), lambda i: (0, i, 0))


def _const_spec(shape):
    nd = len(shape)
    return pl.BlockSpec(shape, lambda i: (0,) * nd)


def _seq_view(a, nb, seq):
    return a.reshape(nb, seq, a.shape[-1])


def _gla_call(gq, gk, gv, gg, ga, wa2p, ba2p, nwp, nb, seq):
    c = min(128, seq)
    n = seq // c
    mall, masks, nl = _level_tables(c)
    sv = lambda a: _seq_view(a, nb, seq)
    return pl.pallas_call(
        functools.partial(_gla_kernel, c=c, nl=nl),
        grid=(n,),
        in_specs=[_tok_spec(nb, c, HEAD_W), _tok_spec(nb, c, PAIR_W), _tok_spec(nb, c, HEAD_W),
                  _tok_spec(nb, c, HEAD_W), _tok_spec(nb, c, LANE),
                  _const_spec(wa2p.shape), _const_spec(ba2p.shape), _const_spec(nwp.shape),
                  _const_spec(mall.shape), _const_spec(masks.shape)],
        out_specs=_tok_spec(nb, c, HEAD_W),
        out_shape=jax.ShapeDtypeStruct((nb, seq, HEAD_W), BF16),
        scratch_shapes=[pltpu.VMEM((nb, N_HEADS, LANE, LANE), F32)],
        compiler_params=_cparams(("arbitrary",)),
        name="gla_recurrence",
    )(sv(gq), sv(gk), sv(gv), sv(gg), sv(ga), wa2p, ba2p, nwp,
      jnp.asarray(mall, BF16), jnp.asarray(masks)).reshape(nb * seq, HEAD_W)


def _hg_call(hq, hf, hi, hg, lbp, nwp, nb, seq):
    c = min(128, seq)
    n = seq // c
    mall, masks, nl = _level_tables(c)
    sv = lambda a: _seq_view(a, nb, seq)
    return pl.pallas_call(
        functools.partial(_hg_kernel, c=c, nl=nl),
        grid=(n,),
        in_specs=[_tok_spec(nb, c, HEAD_W), _tok_spec(nb, c, PAIR_W), _tok_spec(nb, c, PAIR_W),
                  _tok_spec(nb, c, PAIR_W),
                  _const_spec(lbp.shape), _const_spec(nwp.shape),
                  _const_spec(mall.shape), _const_spec(masks.shape)],
        out_specs=_tok_spec(nb, c, PAIR_W),
        out_shape=jax.ShapeDtypeStruct((nb, seq, PAIR_W), BF16),
        scratch_shapes=[pltpu.VMEM((nb, N_PAIRS, LANE, LANE), F32)],
        compiler_params=_cparams(("arbitrary",)),
        name="hgrn2_recurrence",
    )(sv(hq), sv(hf), sv(hi), sv(hg), lbp, nwp,
      jnp.asarray(mall, BF16), jnp.asarray(masks)).reshape(nb * seq, PAIR_W)


def _ret_tables(c):
    hs = np.arange(N_HEADS, dtype=np.float64)
    log_gamma = np.log(1.0 - np.exp2(-5.0 - hs))
    idx = np.arange(c, dtype=np.float64)
    rel = idx[:, None] - idx[None, :]
    dmat = np.where(rel >= 0, np.exp(log_gamma[:, None, None] * np.maximum(rel, 0.0)), 0.0)
    qdec = np.exp(log_gamma[:, None] * (idx + 1.0))
    kdec = np.exp(log_gamma[:, None] * (c - 1.0 - idx))
    cdec = np.exp(log_gamma * c)
    qfull = np.broadcast_to(qdec[:, :, None], (N_HEADS, c, LANE)).astype(np.float32)
    kpair = np.zeros((N_PAIRS, c, LANE), np.float32)
    for hd in range(N_HEADS):
        kpair[hd // 2, :, (hd % 2) * HALF:(hd % 2 + 1) * HALF] = kdec[hd][:, None]
    return dmat.astype(np.float32), qfull, kpair, [float(np.float32(v)) for v in cdec]


def _ret_kernel(q_ref, k_ref, v_ref, og_ref, cos_ref, sin_ref, dmat_ref, qdec_ref, kdec_ref,
                o_ref, st_ref, *, c, cdec):
    @pl.when(pl.program_id(0) == 0)
    def _():
        st_ref[...] = jnp.zeros_like(st_ref)

    half = RET_DK // 2
    lane = lax.broadcasted_iota(jnp.int32, (c, LANE), 1)
    first = (lane & (HALF - 1)) < half

    for b in range(q_ref.shape[0]):
        cos = cos_ref[b]
        sin = sin_ref[b]

        def rotary(t):
            rot = jnp.where(first, -pltpu.roll(t, LANE - half, 1), pltpu.roll(t, half, 1))
            return t * cos + rot * sin

        for pr in range(N_PAIRS):
            psl = slice(pr * LANE, (pr + 1) * LANE)
            k = rotary(k_ref[b, :, psl].astype(F32))
            kb = k.astype(BF16)
            k_end = (k * kdec_ref[pr]).astype(BF16)
            for hd in (2 * pr, 2 * pr + 1):
                sl = slice(hd * LANE, (hd + 1) * LANE)
                q = rotary(q_ref[b, :, sl].astype(F32)) * (RET_DK ** -0.5)
                v = v_ref[b, :, sl]
                st = st_ref[b, hd]
                scores = _dot_nt(q.astype(BF16), kb) * dmat_ref[hd]
                o = _dot(scores.astype(BF16), v) + _dot_nt((q * qdec_ref[hd]).astype(BF16), st.astype(BF16))
                st_ref[b, hd] = cdec[hd] * st + _dot_tn(v, k_end)
                mu = jnp.sum(o, axis=-1, keepdims=True) * (1.0 / RET_DV)
                dlt = jnp.where(lane < RET_DV, o - mu, 0.0)
                var = jnp.sum(dlt * dlt, axis=-1, keepdims=True) * (1.0 / RET_DV)
                gate = og_ref[b, :, sl].astype(F32)
                o_ref[b, :, sl] = (dlt * lax.rsqrt(var + EPS) * gate * _sigmoid(gate)).astype(o_ref.dtype)


def _ret_call(rq, rk, rv, rg, cos, sin, nb, seq):
    c = min(128, seq)
    n = seq // c
    dmat, qdec, kdec, cdec = _ret_tables(c)
    sv = lambda a: _seq_view(a, nb, seq)
    return pl.pallas_call(
        functools.partial(_ret_kernel, c=c, cdec=cdec),
        grid=(n,),
        in_specs=[_tok_spec(nb, c, HEAD_W), _tok_spec(nb, c, PAIR_W), _tok_spec(nb, c, HEAD_W),
                  _tok_spec(nb, c, HEAD_W), _tok_spec(nb, c, LANE), _tok_spec(nb, c, LANE),
                  _const_spec(dmat.shape), _const_spec(qdec.shape), _const_spec(kdec.shape)],
        out_specs=_tok_spec(nb, c, HEAD_W),
        out_shape=jax.ShapeDtypeStruct((nb, seq, HEAD_W), BF16),
        scratch_shapes=[pltpu.VMEM((nb, N_HEADS, LANE, LANE), F32)],
        compiler_params=_cparams(("arbitrary",)),
        name="retention",
    )(sv(rq), sv(rk), sv(rv), sv(rg), cos, sin,
      jnp.asarray(dmat), jnp.asarray(qdec), jnp.asarray(kdec)).reshape(nb * seq, HEAD_W)


def _outproj_router_kernel(x_ref, oa_ref, ob_ref, oc_ref, wo_ref, g1_ref, nw_ref, sc_ref, sh_ref,
                           wr_ref, br_ref, tri_ref, upper_ref,
                           xo_ref, h_ref, tw_ref, lo_ref, tab_ref, cnt_ref, *, tm):
    @pl.when(pl.program_id(0) == 0)
    def _():
        cnt_ref[...] = jnp.zeros_like(cnt_ref)

    wa, wb = oa_ref.shape[1], ob_ref.shape[1]
    mix = (_dot(oa_ref[...], wo_ref[0:wa, :]) + _dot(ob_ref[...], wo_ref[wa:wa + wb, :])
           + _dot(oc_ref[...], wo_ref[wa + wb:, :]))
    x = x_ref[...] + g1_ref[0] * mix
    xo_ref[...] = x
    y = x * lax.rsqrt(jnp.mean(x * x, axis=-1, keepdims=True) + EPS) * nw_ref[...]
    h = y * (1.0 + sc_ref[0]) + sh_ref[0]
    h_ref[...] = h

    lg = _dot_f32(h, wr_ref[...]) + br_ref[...]
    lane = lax.broadcasted_iota(jnp.int32, (tm, LANE), 1)
    sel_e, sel_v = [], []
    for _ in range(TOP_K):
        m = jnp.max(lg, axis=-1, keepdims=True)
        idx = jnp.min(jnp.where(lg == m, lane, LANE), axis=-1, keepdims=True)
        sel_e.append(idx)
        sel_v.append(m)
        lg = jnp.where(lane == idx, -jnp.inf, lg)
    ex = [jnp.exp(v - sel_v[0]) for v in sel_v]
    den = ex[0] + ex[1] + ex[2] + ex[3]
    hot = [(lane == idx) for idx in sel_e]
    onehot = jnp.zeros((tm, LANE), F32)
    for hk in hot:
        onehot = onehot + jnp.where(hk, 1.0, 0.0)
    in_tile = _dot(tri_ref[...], onehot.astype(BF16))
    earlier = cnt_ref[...]
    tile_cnt = jnp.floor((jnp.sum(onehot, axis=0, keepdims=True) + (RUN_ALIGN - 1.0)) * (1.0 / RUN_ALIGN)) * RUN_ALIGN
    tile_start = _dot_f32(jnp.broadcast_to(tile_cnt, (8, LANE)), upper_ref[...])[0:1]
    local = tile_start + in_tile
    tw = jnp.zeros((tm, LANE), F32)
    lo = jnp.zeros((tm, LANE), jnp.int32)
    for kk in range(TOP_K):
        lrow = jnp.sum(jnp.where(hot[kk], local, 0.0), axis=-1, keepdims=True).astype(jnp.int32)
        tw = jnp.where(lane == kk, ex[kk] / den, tw)
        lo = jnp.where(lane == kk, lrow, lo)
    tw_ref[...] = tw
    lo_ref[...] = lo
    tab_ref[0, 0:1, :] = tile_start.astype(jnp.int32)
    tab_ref[0, 1:2, :] = tile_cnt.astype(jnp.int32)
    tab_ref[0, 2:3, :] = earlier.astype(jnp.int32)
    tab_ref[0, 3:8, :] = jnp.zeros((5, LANE), jnp.int32)
    cnt_ref[...] = earlier + tile_cnt


def _outproj_router_call(x, oa, ob, oc, wo, layer, g1, nw, sc, sh, wr, br, seq):
    t, d = x.shape
    tm = min(ROUTE_TILE, seq)
    per_b = seq // tm
    tri = np.tril(np.ones((tm, tm), np.float32), -1)
    upper = np.triu(np.ones((LANE, LANE), np.float32), 1)
    row = lambda w: pl.BlockSpec((tm, w), lambda i: (i, 0))
    const = lambda shape: pl.BlockSpec(shape, lambda i: (0,) * len(shape))
    perb = pl.BlockSpec((1, 1, d), lambda i: (i // per_b, 0, 0))
    n_tiles = t // tm
    return pl.pallas_call(
        functools.partial(_outproj_router_kernel, tm=tm),
        grid=(n_tiles,),
        in_specs=[row(d), row(oa.shape[1]), row(ob.shape[1]), row(oc.shape[1]),
                  pl.BlockSpec((None,) + wo.shape[1:], lambda i: (layer, 0, 0)), perb,
                  const((1, d)), perb, perb, const(wr.shape), const(br.shape), const(tri.shape),
                  const(upper.shape)],
        out_specs=[row(d), row(d), row(LANE), row(LANE),
                   pl.BlockSpec((1, 8, LANE), lambda i: (i, 0, 0)), const((1, LANE))],
        out_shape=[jax.ShapeDtypeStruct((t, d), F32), jax.ShapeDtypeStruct((t, d), F32),
                   jax.ShapeDtypeStruct((t, LANE), F32), jax.ShapeDtypeStruct((t, LANE), jnp.int32),
                   jax.ShapeDtypeStruct((n_tiles, 8, LANE), jnp.int32), jax.ShapeDtypeStruct((1, LANE), F32)],
        compiler_params=_cparams(("arbitrary",)),
        name="outproj_norm2_router",
    )(x, oa, ob, oc, wo, g1, nw.reshape(1, d), sc, sh, wr, br, jnp.asarray(tri, BF16), jnp.asarray(upper))


def _run_copies(tab_ref, at_ref, tile, tm, make_copy, act):
    sizes = [s for s in (1 << p for p in range(tm.bit_length() - 1, -1, -1)) if s >= RUN_ALIGN]

    def per_expert(e, carry):
        first = tab_ref[tile, 0, e]
        length = tab_ref[tile, 1, e]
        at = at_ref[tile, e]
        done = jnp.int32(0)
        for size in sizes:
            part = length & size

            @pl.when(part != 0)
            def _():
                act(make_copy(pl.multiple_of(first + done, RUN_ALIGN), pl.multiple_of(at + done, RUN_ALIGN), size))
            done = done + part
        return carry

    lax.fori_loop(0, N_EXPERTS, per_expert, 0)


def _local_rows(tm):
    return TOP_K * tm + N_EXPERTS * RUN_ALIGN


def _dispatch_kernel(tab_ref, at_ref, h_ref, lo_ref, buf_in_ref, buf_ref, st_ref, sem, *, tm, n_tiles):
    del buf_in_ref
    i = pl.program_id(0)
    slot = i % 2

    def copies(tile, s, act):
        _run_copies(tab_ref, at_ref, tile, tm,
                    lambda row, at, size: pltpu.make_async_copy(st_ref.at[s, pl.ds(row, size)],
                                                                buf_ref.at[pl.ds(at, size)], sem.at[s]), act)

    @pl.when(i >= 2)
    def _():
        copies(i - 2, slot, lambda cp: cp.wait())

    col = lax.broadcasted_iota(jnp.int32, (tm, st_ref.shape[1]), 1)
    lo = lo_ref[...]
    place = jnp.zeros(col.shape, F32)
    for kk in range(TOP_K):
        place = jnp.where(col == lo[:, kk:kk + 1], 1.0, place)
    st_ref[slot] = _dot_tn(place.astype(BF16), h_ref[...].astype(BF16))
    copies(i, slot, lambda cp: cp.start())

    @pl.when(i == n_tiles - 1)
    def _():
        if n_tiles >= 2:
            copies(i - 1, 1 - slot, lambda cp: cp.wait())
        copies(i, slot, lambda cp: cp.wait())


def _dispatch_call(h, tab, run_at, lo, buf):
    t, d = h.shape
    n_slots = buf.shape[0]
    tm = min(ROUTE_TILE, t)
    n_tiles = t // tm
    grid_spec = pltpu.PrefetchScalarGridSpec(
        num_scalar_prefetch=2,
        grid=(n_tiles,),
        in_specs=[pl.BlockSpec((tm, d), lambda i, tb, sr: (i, 0)),
                  pl.BlockSpec((tm, LANE), lambda i, tb, sr: (i, 0)),
                  pl.BlockSpec(memory_space=pl.ANY)],
        out_specs=pl.BlockSpec(memory_space=pl.ANY),
        scratch_shapes=[pltpu.VMEM((2, _local_rows(tm), d), F32), pltpu.SemaphoreType.DMA((2,))],
    )
    return pl.pallas_call(
        functools.partial(_dispatch_kernel, tm=tm, n_tiles=n_tiles),
        grid_spec=grid_spec,
        out_shape=jax.ShapeDtypeStruct((n_slots, d), F32),
        input_output_aliases={4: 0},
        compiler_params=_cparams(("arbitrary",)),
        name="moe_dispatch",
    )(tab, run_at, h, lo, buf)


def _expert_kernel(be_ref, nu_ref, x_ref, wgu_ref, bgu_ref, wdn_ref, bdn_ref, o_ref, wgu_bf, wdn_bf, *, d_ff):
    i = pl.program_id(0)
    live = i < nu_ref[0]

    @pl.when(jnp.logical_and(live, jnp.logical_or(i == 0, be_ref[i] != be_ref[jnp.maximum(i - 1, 0)])))
    def _():
        wgu_bf[...] = wgu_ref[0, 0].astype(BF16)
        wdn_bf[...] = wdn_ref[0, 0].astype(BF16)

    @pl.when(live)
    def _():
        gu = _dot(x_ref[...].astype(BF16), wgu_bf[...]) + bgu_ref[0, 0]
        gate = jnp.minimum(gu[:, :d_ff], SWIGLU_LIMIT)
        up = jnp.clip(gu[:, d_ff:], -SWIGLU_LIMIT, SWIGLU_LIMIT)
        act = (up + 1.0) * gate * _sigmoid(SWIGLU_ALPHA * gate)
        o_ref[...] = _dot(act.astype(BF16), wdn_bf[...]) + bdn_ref[0, 0]

    @pl.when(jnp.logical_not(live))
    def _():
        o_ref[...] = jnp.zeros_like(o_ref)


def _expert_call(buf, block_e, n_used, wgu, bgu, wdn, bdn, layer):
    n_slots, d = buf.shape
    n_blocks = n_slots // MOE_BLOCK
    depth, ne, _, f2 = wgu.shape
    d_ff = f2 // 2
    blk = lambda i, be, nu: (jnp.minimum(i, nu[0] - 1), 0)
    exp4 = lambda i, be, nu: (layer, be[jnp.minimum(i, nu[0] - 1)], 0, 0)
    grid_spec = pltpu.PrefetchScalarGridSpec(
        num_scalar_prefetch=2,
        grid=(n_blocks,),
        in_specs=[pl.BlockSpec((MOE_BLOCK, d), blk),
                  pl.BlockSpec((1, 1, d, f2), exp4),
                  pl.BlockSpec((1, 1, 1, f2), exp4),
                  pl.BlockSpec((1, 1, d_ff, d), exp4),
                  pl.BlockSpec((1, 1, 1, d), exp4)],
        out_specs=pl.BlockSpec((MOE_BLOCK, d), lambda i, be, nu: (i, 0)),
        scratch_shapes=[pltpu.VMEM((d, f2), BF16), pltpu.VMEM((d_ff, d), BF16)],
    )
    return pl.pallas_call(
        functools.partial(_expert_kernel, d_ff=d_ff),
        grid_spec=grid_spec,
        out_shape=jax.ShapeDtypeStruct((n_slots, d), F32),
        compiler_params=_cparams(("arbitrary",)),
        name="moe_experts",
    )(block_e, n_used, buf, wgu, bgu.reshape(depth, ne, 1, f2), wdn, bdn.reshape(depth, ne, 1, d))


def _combine_kernel(tab_ref, at_ref, yb_ref, lo_ref, w_ref, x_ref, g2_ref, o_ref, rs_ref, sem, *, tm):
    i = pl.program_id(0)
    n = pl.num_programs(0)
    rows = rs_ref.shape[1]

    def run_copies(tile, s, act):
        _run_copies(tab_ref, at_ref, tile, tm,
                    lambda row, at, size: pltpu.make_async_copy(yb_ref.at[pl.ds(at, size)],
                                                                rs_ref.at[s, pl.ds(row, size)], sem.at[s]), act)

    @pl.when(i == 0)
    def _():
        rs_ref[...] = jnp.zeros_like(rs_ref)
        run_copies(0, 0, lambda cp: cp.start())

    @pl.when(i + 1 < n)
    def _():
        run_copies(i + 1, (i + 1) % 2, lambda cp: cp.start())

    slot = i % 2
    run_copies(i, slot, lambda cp: cp.wait())
    r = rs_ref[slot].astype(BF16)
    col = lax.broadcasted_iota(jnp.int32, (tm, rows), 1)
    w = w_ref[...]
    lo = lo_ref[...]
    pw = jnp.zeros((tm, rows), F32)
    for kk in range(TOP_K):
        pw = jnp.where(col == lo[:, kk:kk + 1], w[:, kk:kk + 1], pw)
    pw_hi, pw_lo = _split2(pw)
    y = _dot(pw_hi, r) + _dot(pw_lo, r)
    o_ref[...] = x_ref[...] + g2_ref[0] * y


def _combine_call(yb, tab, src, lo, tw, x, g2, seq):
    t, d = x.shape
    tm = min(ROUTE_TILE, seq)
    per_b = seq // tm
    grid_spec = pltpu.PrefetchScalarGridSpec(
        num_scalar_prefetch=2,
        grid=(t // tm,),
        in_specs=[pl.BlockSpec(memory_space=pl.ANY),
                  pl.BlockSpec((tm, LANE), lambda i, tb, sr: (i, 0)),
                  pl.BlockSpec((tm, LANE), lambda i, tb, sr: (i, 0)),
                  pl.BlockSpec((tm, d), lambda i, tb, sr: (i, 0)),
                  pl.BlockSpec((1, 1, d), lambda i, tb, sr: (i // per_b, 0, 0))],
        out_specs=pl.BlockSpec((tm, d), lambda i, tb, sr: (i, 0)),
        scratch_shapes=[pltpu.VMEM((2, _local_rows(tm), d), F32), pltpu.SemaphoreType.DMA((2,))],
    )
    return pl.pallas_call(
        functools.partial(_combine_kernel, tm=tm),
        grid_spec=grid_spec,
        out_shape=jax.ShapeDtypeStruct((t, d), F32),
        compiler_params=_cparams(("arbitrary",)),
        name="moe_combine",
    )(tab, src, yb, lo, tw, x, g2)


def _final_norm_kernel(x_ref, w_ref, o_ref):
    x = x_ref[...]
    o_ref[...] = x * lax.rsqrt(jnp.mean(x * x, axis=-1, keepdims=True) + EPS) * w_ref[...]


def _final_norm_call(x, w):
    t, d = x.shape
    tm = min(512, t)
    return pl.pallas_call(
        _final_norm_kernel,
        grid=(t // tm,),
        in_specs=[pl.BlockSpec((tm, d), lambda i: (i, 0)), pl.BlockSpec((1, d), lambda i: (0, 0))],
        out_specs=pl.BlockSpec((tm, d), lambda i: (i, 0)),
        out_shape=jax.ShapeDtypeStruct((t, d), F32),
        compiler_params=_cparams(("parallel",)),
        name="final_norm",
    )(x, w.reshape(1, d))


def _split_heads(w, hd):
    return w.reshape(w.shape[:-1] + (N_HEADS, hd))


def _pad_last(w, lo, hi):
    return jnp.pad(w, [(0, 0)] * (w.ndim - 1) + [(lo, hi)])


def _tile_cols(w, hd):
    return _pad_last(_split_heads(w, hd), 0, LANE - hd).reshape(w.shape[:-1] + (HEAD_W,))


def _pair_cols(w, hd):
    return _pad_last(_split_heads(w, hd), 0, HALF - hd).reshape(w.shape[:-1] + (PAIR_W,))


def _query_cols(w, hd):
    w = _split_heads(w, hd)
    tiles = [_pad_last(w[..., h, :], (h % 2) * HALF, LANE - (h % 2) * HALF - hd) for h in range(N_HEADS)]
    return jnp.concatenate(tiles, axis=-1)


def _tile_rows(w, hd):
    return jnp.swapaxes(_tile_cols(jnp.swapaxes(w, -1, -2), hd), -1, -2)


IN_PARTS = (("gq", GLA_DK, _query_cols), ("gk", GLA_DK, _pair_cols), ("gv", GLA_DV, _tile_cols),
            ("gg", GLA_DV, _tile_cols), ("ga", None, None),
            ("hq", HG_DK, _query_cols), ("hf", HG_DK, _pair_cols), ("hi", HG_DV, _pair_cols),
            ("hg", HG_DV, _pair_cols),
            ("rq", RET_DK, _query_cols), ("rk", RET_DK, _pair_cols), ("rv", RET_DV, _tile_cols),
            ("rg", RET_DV, _tile_cols))
F32_PARTS = ("ga", "hf")


def _layout_w_in(w_in):
    cols, off = [], 0
    for _, hd, layout in IN_PARTS:
        if layout is None:
            part = _pad_last(w_in[..., off:off + GLA_RANK], 0, LANE - GLA_RANK)
            off += GLA_RANK
        else:
            part = layout(w_in[..., off:off + N_HEADS * hd], hd)
            off += N_HEADS * hd
        cols.append(part)
    return jnp.concatenate(cols, axis=-1).astype(BF16), tuple(int(p.shape[-1]) for p in cols)


def kernel(x, c, positions, w_mod, b_mod, norm1_w, w_in, gla_wa2, gla_ba2, hg_lb, gla_norm_w, hg_norm_w,
           w_out, norm2_w, w_r, b_r, w_gu, b_gu, w_dn, b_dn, final_norm_w):
    nb, seq, d = x.shape
    t = nb * seq
    depth = w_mod.shape[0]
    dtypes = tuple(F32 if name in F32_PARTS else BF16 for name, _, _ in IN_PARTS)

    mod = _mod_call(c, w_mod, b_mod)
    cos, sin = _rope_call(positions)
    lb_all = _lb_call(hg_lb)

    n_assign = t * TOP_K
    n_run_pad = N_EXPERTS * (RUN_ALIGN - 1) * (t // min(ROUTE_TILE, seq))
    n_blocks = (n_assign + n_run_pad + MOE_BLOCK - 1) // MOE_BLOCK + N_EXPERTS
    n_slots = n_blocks * MOE_BLOCK

    w_cat, widths = _layout_w_in(w_in)
    n_gla, n_hg = N_HEADS * GLA_DV, N_HEADS * HG_DV
    wo_p = jnp.concatenate([_tile_rows(w_out[:, :n_gla], GLA_DV), w_out[:, n_gla:n_gla + n_hg],
                            _tile_rows(w_out[:, n_gla + n_hg:], RET_DV)], axis=1).astype(BF16)
    wa2p = jnp.pad(_pair_cols(gla_wa2, GLA_DK), ((0, 0), (0, LANE - GLA_RANK), (0, 0)))
    ba2p = _pair_cols(gla_ba2, GLA_DK)
    gnw = _tile_cols(gla_norm_w, GLA_DV)
    wr_p = _pad_last(w_r, 0, LANE - N_EXPERTS)
    br_p = jnp.pad(b_r, ((0, 0), (0, LANE - N_EXPERTS)), constant_values=NEG_BIG)

    buf = jnp.zeros((n_slots, d), F32)
    xf = x.reshape(t, d)
    for layer in range(depth):
        sh1, sc1, g1, sh2, sc2, g2 = [m.reshape(nb, 1, d) for m in jnp.split(mod[layer], 6, axis=-1)]

        parts = _inproj_call(xf, norm1_w[layer], sc1, sh1, w_cat, layer, widths, dtypes, seq)
        gq, gk, gv, gg, ga, hq, hf, hi, hg, rq, rk, rv, rg = parts
        o_gla = _gla_call(gq, gk, gv, gg, ga, wa2p[layer], ba2p[layer:layer + 1], gnw[layer:layer + 1], nb, seq)
        o_hg = _hg_call(hq, hf, hi, hg, lb_all[layer:layer + 1], hg_norm_w[layer:layer + 1], nb, seq)
        o_ret = _ret_call(rq, rk, rv, rg, cos, sin, nb, seq)
        xf, h2, tw, lo, tab, cnt = _outproj_router_call(
            xf, o_gla, o_hg, o_ret, wo_p, layer, g1, norm2_w[layer], sc2, sh2, wr_p[layer],
            br_p[layer:layer + 1], seq)

        counts = cnt[0, :N_EXPERTS].astype(jnp.int32)
        padded = (counts + MOE_BLOCK - 1) // MOE_BLOCK * MOE_BLOCK
        pends = jnp.cumsum(padded)
        pstarts = pends - padded
        block_start = jnp.arange(n_blocks, dtype=jnp.int32) * MOE_BLOCK
        block_e = jnp.minimum(jnp.sum((pends[None, :] <= block_start[:, None]).astype(jnp.int32), axis=1),
                              N_EXPERTS - 1)
        n_used = (pends[-1:] // MOE_BLOCK).astype(jnp.int32)
        run_at = tab[:, 2, :] + jnp.pad(pstarts, (0, LANE - N_EXPERTS))[None, :]

        buf = _dispatch_call(h2, tab, run_at, lo, buf)
        yb = _expert_call(buf, block_e, n_used, w_gu, b_gu, w_dn, b_dn, layer)
        xf = _combine_call(yb, tab, run_at, lo, tw, xf, g2, seq)

    return _final_norm_call(xf, final_norm_w).reshape(nb, seq, d)
```

```python
import functools

import numpy as np
import jax
import jax.numpy as jnp
from jax import lax
from jax.experimental import pallas as pl
from jax.experimental.pallas import tpu as pltpu

F32 = jnp.float32
BF16 = jnp.bfloat16

N_HEADS = 4
GLA_DK, GLA_DV, GLA_RANK, GLA_TAU = 48, 96, 16, 16.0
HG_DK, HG_DV = 64, 64
RET_DK, RET_DV = 48, 96
ROPE_BASE = 10000.0
N_EXPERTS, TOP_K = 32, 4
SWIGLU_LIMIT, SWIGLU_ALPHA = 7.0, 1.702
MOE_BLOCK = 512
ROUTE_TILE = 512
RUN_ALIGN = 8
EPS = 1e-6

LANE = 128
HALF = LANE // 2
N_PAIRS = N_HEADS // 2
HEAD_W = N_HEADS * LANE
PAIR_W = N_PAIRS * LANE
VMEM_LIMIT = 56 * 1024 * 1024
NEG_BIG = -1e30


def _cparams(sem):
    return pltpu.CompilerParams(dimension_semantics=sem, vmem_limit_bytes=VMEM_LIMIT)


def _dot(a, b):
    return jnp.dot(a, b, preferred_element_type=F32)


def _dot_nt(a, b):
    return lax.dot_general(a, b, (((1,), (1,)), ((), ())), preferred_element_type=F32)


def _dot_tn(a, b):
    return lax.dot_general(a, b, (((0,), (0,)), ((), ())), preferred_element_type=F32)


def _split2(a):
    hi = a.astype(BF16)
    return hi, (a - hi.astype(F32)).astype(BF16)


def _dot_f32(a, b):
    a_hi, a_lo = _split2(a)
    b_hi, b_lo = _split2(b)
    return _dot(a_hi, b_hi) + _dot(a_hi, b_lo) + _dot(a_lo, b_hi)


def _sigmoid(x):
    return 1.0 / (1.0 + jnp.exp(-x))


def _log_sigmoid(x):
    return jnp.minimum(x, 0.0) - jnp.log1p(jnp.exp(-jnp.abs(x)))


def _mod_kernel(c_ref, w_ref, b_ref, o_ref):
    c = c_ref[...]
    o_ref[0] = _dot_f32(c * _sigmoid(c), w_ref[0]) + b_ref[0]


def _mod_call(c, w_mod, b_mod):
    depth, d, d6 = w_mod.shape
    nb = c.shape[0]
    rows = 8
    c_pad = jnp.zeros((rows, d), F32).at[:nb].set(c)
    out = pl.pallas_call(
        _mod_kernel,
        grid=(depth, d6 // d),
        in_specs=[
            pl.BlockSpec((rows, d), lambda l, j: (0, 0)),
            pl.BlockSpec((1, d, d), lambda l, j: (l, 0, j)),
            pl.BlockSpec((1, 1, d), lambda l, j: (l, 0, j)),
        ],
        out_specs=pl.BlockSpec((1, rows, d), lambda l, j: (l, 0, j)),
        out_shape=jax.ShapeDtypeStruct((depth, rows, d6), F32),
        compiler_params=_cparams(("parallel", "parallel")),
        name="mod",
    )(c_pad, w_mod, b_mod.reshape(depth, 1, d6))
    return out[:, :nb]


def _rope_kernel(pos_ref, freq_ref, cos_ref, sin_ref):
    ang = pos_ref[0] * freq_ref[...]
    cos_ref[0] = jnp.cos(ang)
    sin_ref[0] = jnp.sin(ang)


def _rope_call(positions):
    nb, l = positions.shape
    half = RET_DK // 2
    inv = (ROPE_BASE ** (-np.arange(half, dtype=np.float32) / half)).astype(np.float32)
    freq = np.zeros((1, LANE), np.float32)
    for base in (0, HALF):
        freq[0, base:base + half] = inv
        freq[0, base + half:base + 2 * half] = inv
    tl = min(l, 512)
    pos = positions.astype(F32).reshape(nb, l, 1)
    shp = jax.ShapeDtypeStruct((nb, l, LANE), F32)
    return pl.pallas_call(
        _rope_kernel,
        grid=(nb, l // tl),
        in_specs=[pl.BlockSpec((1, tl, 1), lambda b, i: (b, i, 0)),
                  pl.BlockSpec((1, LANE), lambda b, i: (0, 0))],
        out_specs=[pl.BlockSpec((1, tl, LANE), lambda b, i: (b, i, 0))] * 2,
        out_shape=[shp, shp],
        compiler_params=_cparams(("parallel", "parallel")),
        name="rope_tables",
    )(pos, jnp.asarray(freq))


def _lb_kernel(p_ref, o_ref):
    p = p_ref[...]
    depth = p.shape[0]
    m = jnp.max(p, axis=0, keepdims=True)
    e = jnp.exp(p - m)
    sm = e / jnp.sum(e, axis=0, keepdims=True)
    acc = jnp.zeros_like(sm[0:1])
    for i in range(depth):
        if i > 0:
            acc = acc + sm[i:i + 1]
        o_ref[i:i + 1, :] = acc


def _lb_call(hg_lb):
    return pl.pallas_call(
        _lb_kernel,
        out_shape=jax.ShapeDtypeStruct(hg_lb.shape, F32),
        name="hg_lower_bounds",
    )(hg_lb.astype(F32))


def _inproj_kernel(x_ref, nw_ref, sc_ref, sh_ref, w_ref, *out_refs, widths):
    x = x_ref[...]
    y = x * lax.rsqrt(jnp.mean(x * x, axis=-1, keepdims=True) + EPS) * nw_ref[...]
    hb = (y * (1.0 + sc_ref[0]) + sh_ref[0]).astype(BF16)
    off = 0
    for o_ref, wd in zip(out_refs, widths):
        o_ref[...] = _dot(hb, w_ref[:, off:off + wd]).astype(o_ref.dtype)
        off += wd


def _inproj_call(x, norm_w, sc, sh, w_cat, layer, widths, dtypes, seq):
    t, d = x.shape
    tm = min(512, seq)
    per_b = seq // tm
    return pl.pallas_call(
        functools.partial(_inproj_kernel, widths=widths),
        grid=(t // tm,),
        in_specs=[
            pl.BlockSpec((tm, d), lambda i: (i, 0)),
            pl.BlockSpec((1, d), lambda i: (0, 0)),
            pl.BlockSpec((1, 1, d), lambda i: (i // per_b, 0, 0)),
            pl.BlockSpec((1, 1, d), lambda i: (i // per_b, 0, 0)),
            pl.BlockSpec((None,) + w_cat.shape[1:], lambda i: (layer, 0, 0)),
        ],
        out_specs=[pl.BlockSpec((tm, wd), lambda i: (i, 0)) for wd in widths],
        out_shape=[jax.ShapeDtypeStruct((t, wd), dt) for wd, dt in zip(widths, dtypes)],
        compiler_params=_cparams(("parallel",)),
        name="norm1_inproj",
    )(x, norm_w.reshape(1, d), sc, sh, w_cat)


def _level_tables(c):
    nl = int(np.log2(c))
    assert 1 << nl == c
    idx = np.arange(c)
    mats, masks = [], [np.eye(c, dtype=np.float32)]
    t = idx[None, :]
    i = idx[:, None]
    for lvl in range(nl):
        h = c >> (lvl + 1)
        blk, pos = idx // (2 * h), idx % (2 * h)
        m = (blk * 2 * h + h - 1)[:, None]
        right = (pos >= h)[:, None]
        a = np.where(right, (t > m) & (t <= i), (t > i) & (t <= m))
        mats.append(a.astype(np.float32))
        same = blk[:, None] == blk[None, :]
        masks.append((same & right & (pos < h)[None, :]).astype(np.float32))
    mats.append((t <= i).astype(np.float32))
    mats.append((t > i).astype(np.float32))
    return np.concatenate(mats, 0), np.stack(masks, 0), nl


def _pair_decays(g, mall_ref):
    gs = _dot(mall_ref[...], jnp.concatenate(_split2(g), axis=1))
    return jnp.exp(gs[:, :LANE] + gs[:, LANE:])


def _pair_keys(k, e, c, nl):
    levels = [(k * e[lvl * c:(lvl + 1) * c]).astype(BF16) for lvl in range(nl)]
    return k.astype(BF16), levels, (k * e[(nl + 1) * c:(nl + 2) * c]).astype(BF16)


def _gated_head(q, kb, klv, e, v, st, masks_ref, c, nl):
    scores = masks_ref[0] * _dot_nt(q.astype(BF16), kb)
    for lvl in range(nl):
        scores = scores + masks_ref[lvl + 1] * _dot_nt((q * e[lvl * c:(lvl + 1) * c]).astype(BF16), klv[lvl])
    q_in = (q * e[nl * c:(nl + 1) * c]).astype(BF16)
    return _dot(scores.astype(BF16), v) + _dot_nt(q_in, st.astype(BF16))


def _gla_kernel(q_ref, k_ref, v_ref, gg_ref, ga_ref, wa2_ref, ba2_ref, nw_ref, mall_ref, masks_ref,
                o_ref, st_ref, *, c, nl):
    @pl.when(pl.program_id(0) == 0)
    def _():
        st_ref[...] = jnp.zeros_like(st_ref)

    for b in range(q_ref.shape[0]):
        log_alpha = _log_sigmoid(_dot_f32(ga_ref[b], wa2_ref[...]) + ba2_ref[...]) * (1.0 / GLA_TAU)
        for pr in range(N_PAIRS):
            psl = slice(pr * LANE, (pr + 1) * LANE)
            e = _pair_decays(log_alpha[:, psl], mall_ref)
            kb, klv, k_end = _pair_keys(k_ref[b, :, psl].astype(F32), e, c, nl)
            dec = e[(nl + 1) * c - 1:(nl + 1) * c, :]
            for hd in (2 * pr, 2 * pr + 1):
                sl = slice(hd * LANE, (hd + 1) * LANE)
                q = q_ref[b, :, sl].astype(F32) * (GLA_DK ** -0.5)
                v = v_ref[b, :, sl]
                st = st_ref[b, hd]
                o = _gated_head(q, kb, klv, e, v, st, masks_ref, c, nl)
                st_ref[b, hd] = dec * st + _dot_tn(v, k_end)
                y = o * lax.rsqrt(jnp.sum(o * o, axis=-1, keepdims=True) * (1.0 / GLA_DV) + EPS) * nw_ref[:, sl]
                gate = gg_ref[b, :, sl].astype(F32)
                o_ref[b, :, sl] = (y * gate * _sigmoid(gate)).astype(o_ref.dtype)


def _hg_kernel(q_ref, f_ref, v_ref, og_ref, lb_ref, nw_ref, mall_ref, masks_ref,
               o_ref, st_ref, *, c, nl):
    @pl.when(pl.program_id(0) == 0)
    def _():
        st_ref[...] = jnp.zeros_like(st_ref)

    low = lax.broadcasted_iota(jnp.int32, (c, LANE), 1) < HALF
    for b in range(q_ref.shape[0]):
        for pr in range(N_PAIRS):
            psl = slice(pr * LANE, (pr + 1) * LANE)
            lb = lb_ref[:, psl]
            hf = f_ref[b, :, psl]
            la = jnp.log(lb)
            lc = jnp.log1p(-lb) + _log_sigmoid(hf)
            log_f = jnp.maximum(la, lc) + jnp.log1p(jnp.exp(-jnp.abs(la - lc)))
            e = _pair_decays(log_f, mall_ref)
            kb, klv, k_end = _pair_keys((1.0 - lb) * _sigmoid(-hf), e, c, nl)
            v = v_ref[b, :, psl]
            st = st_ref[b, pr]
            o_even = _gated_head(q_ref[b, :, (2 * pr) * LANE:(2 * pr + 1) * LANE].astype(F32),
                                 kb, klv, e, v, st, masks_ref, c, nl)
            o_odd = _gated_head(q_ref[b, :, (2 * pr + 1) * LANE:(2 * pr + 2) * LANE].astype(F32),
                                kb, klv, e, v, st, masks_ref, c, nl)
            o = jnp.where(low, o_even, o_odd)
            st_ref[b, pr] = e[(nl + 1) * c - 1:(nl + 1) * c, :] * st + _dot_tn(v, k_end)
            sq = o * o
            ms = jnp.where(low, jnp.sum(jnp.where(low, sq, 0.0), axis=-1, keepdims=True),
                           jnp.sum(jnp.where(low, 0.0, sq), axis=-1, keepdims=True)) * (1.0 / HG_DV)
            y = o * lax.rsqrt(ms + EPS) * nw_ref[:, psl]
            o_ref[b, :, psl] = (y * _sigmoid(og_ref[b, :, psl].astype(F32))).astype(o_ref.dtype)


def _tok_spec(nb, c, width):
    return pl.BlockSpec((nb, c, width), lambda i: (0, i, 0))


def _const_spec(shape):
    nd = len(shape)
    return pl.BlockSpec(shape, lambda i: (0,) * nd)


def _seq_view(a, nb, seq):
    return a.reshape(nb, seq, a.shape[-1])


def _gla_call(gq, gk, gv, gg, ga, wa2p, ba2p, nwp, nb, seq):
    c = min(128, seq)
    n = seq // c
    mall, masks, nl = _level_tables(c)
    sv = lambda a: _seq_view(a, nb, seq)
    return pl.pallas_call(
        functools.partial(_gla_kernel, c=c, nl=nl),
        grid=(n,),
        in_specs=[_tok_spec(nb, c, HEAD_W), _tok_spec(nb, c, PAIR_W), _tok_spec(nb, c, HEAD_W),
                  _tok_spec(nb, c, HEAD_W), _tok_spec(nb, c, LANE),
                  _const_spec(wa2p.shape), _const_spec(ba2p.shape), _const_spec(nwp.shape),
                  _const_spec(mall.shape), _const_spec(masks.shape)],
        out_specs=_tok_spec(nb, c, HEAD_W),
        out_shape=jax.ShapeDtypeStruct((nb, seq, HEAD_W), BF16),
        scratch_shapes=[pltpu.VMEM((nb, N_HEADS, LANE, LANE), F32)],
        compiler_params=_cparams(("arbitrary",)),
        name="gla_recurrence",
    )(sv(gq), sv(gk), sv(gv), sv(gg), sv(ga), wa2p, ba2p, nwp,
      jnp.asarray(mall, BF16), jnp.asarray(masks)).reshape(nb * seq, HEAD_W)


def _hg_call(hq, hf, hi, hg, lbp, nwp, nb, seq):
    c = min(128, seq)
    n = seq // c
    mall, masks, nl = _level_tables(c)
    sv = lambda a: _seq_view(a, nb, seq)
    return pl.pallas_call(
        functools.partial(_hg_kernel, c=c, nl=nl),
        grid=(n,),
        in_specs=[_tok_spec(nb, c, HEAD_W), _tok_spec(nb, c, PAIR_W), _tok_spec(nb, c, PAIR_W),
                  _tok_spec(nb, c, PAIR_W),
                  _const_spec(lbp.shape), _const_spec(nwp.shape),
                  _const_spec(mall.shape), _const_spec(masks.shape)],
        out_specs=_tok_spec(nb, c, PAIR_W),
        out_shape=jax.ShapeDtypeStruct((nb, seq, PAIR_W), BF16),
        scratch_shapes=[pltpu.VMEM((nb, N_PAIRS, LANE, LANE), F32)],
        compiler_params=_cparams(("arbitrary",)),
        name="hgrn2_recurrence",
    )(sv(hq), sv(hf), sv(hi), sv(hg), lbp, nwp,
      jnp.asarray(mall, BF16), jnp.asarray(masks)).reshape(nb * seq, PAIR_W)


def _ret_tables(c):
    hs = np.arange(N_HEADS, dtype=np.float64)
    log_gamma = np.log(1.0 - np.exp2(-5.0 - hs))
    idx = np.arange(c, dtype=np.float64)
    rel = idx[:, None] - idx[None, :]
    dmat = np.where(rel >= 0, np.exp(log_gamma[:, None, None] * np.maximum(rel, 0.0)), 0.0)
    qdec = np.exp(log_gamma[:, None] * (idx + 1.0))
    kdec = np.exp(log_gamma[:, None] * (c - 1.0 - idx))
    cdec = np.exp(log_gamma * c)
    qfull = np.broadcast_to(qdec[:, :, None], (N_HEADS, c, LANE)).astype(np.float32)
    kpair = np.zeros((N_PAIRS, c, LANE), np.float32)
    for hd in range(N_HEADS):
        kpair[hd // 2, :, (hd % 2) * HALF:(hd % 2 + 1) * HALF] = kdec[hd][:, None]
    return dmat.astype(np.float32), qfull, kpair, [float(np.float32(v)) for v in cdec]


def _ret_kernel(q_ref, k_ref, v_ref, og_ref, cos_ref, sin_ref, dmat_ref, qdec_ref, kdec_ref,
                o_ref, st_ref, *, c, cdec):
    @pl.when(pl.program_id(0) == 0)
    def _():
        st_ref[...] = jnp.zeros_like(st_ref)

    half = RET_DK // 2
    lane = lax.broadcasted_iota(jnp.int32, (c, LANE), 1)
    first = (lane & (HALF - 1)) < half

    for b in range(q_ref.shape[0]):
        cos = cos_ref[b]
        sin = sin_ref[b]

        def rotary(t):
            rot = jnp.where(first, -pltpu.roll(t, LANE - half, 1), pltpu.roll(t, half, 1))
            return t * cos + rot * sin

        for pr in range(N_PAIRS):
            psl = slice(pr * LANE, (pr + 1) * LANE)
            k = rotary(k_ref[b, :, psl].astype(F32))
            kb = k.astype(BF16)
            k_end = (k * kdec_ref[pr]).astype(BF16)
            for hd in (2 * pr, 2 * pr + 1):
                sl = slice(hd * LANE, (hd + 1) * LANE)
                q = rotary(q_ref[b, :, sl].astype(F32)) * (RET_DK ** -0.5)
                v = v_ref[b, :, sl]
                st = st_ref[b, hd]
                scores = _dot_nt(q.astype(BF16), kb) * dmat_ref[hd]
                o = _dot(scores.astype(BF16), v) + _dot_nt((q * qdec_ref[hd]).astype(BF16), st.astype(BF16))
                st_ref[b, hd] = cdec[hd] * st + _dot_tn(v, k_end)
                mu = jnp.sum(o, axis=-1, keepdims=True) * (1.0 / RET_DV)
                dlt = jnp.where(lane < RET_DV, o - mu, 0.0)
                var = jnp.sum(dlt * dlt, axis=-1, keepdims=True) * (1.0 / RET_DV)
                gate = og_ref[b, :, sl].astype(F32)
                o_ref[b, :, sl] = (dlt * lax.rsqrt(var + EPS) * gate * _sigmoid(gate)).astype(o_ref.dtype)


def _ret_call(rq, rk, rv, rg, cos, sin, nb, seq):
    c = min(128, seq)
    n = seq // c
    dmat, qdec, kdec, cdec = _ret_tables(c)
    sv = lambda a: _seq_view(a, nb, seq)
    return pl.pallas_call(
        functools.partial(_ret_kernel, c=c, cdec=cdec),
        grid=(n,),
        in_specs=[_tok_spec(nb, c, HEAD_W), _tok_spec(nb, c, PAIR_W), _tok_spec(nb, c, HEAD_W),
                  _tok_spec(nb, c, HEAD_W), _tok_spec(nb, c, LANE), _tok_spec(nb, c, LANE),
                  _const_spec(dmat.shape), _const_spec(qdec.shape), _const_spec(kdec.shape)],
        out_specs=_tok_spec(nb, c, HEAD_W),
        out_shape=jax.ShapeDtypeStruct((nb, seq, HEAD_W), BF16),
        scratch_shapes=[pltpu.VMEM((nb, N_HEADS, LANE, LANE), F32)],
        compiler_params=_cparams(("arbitrary",)),
        name="retention",
    )(sv(rq), sv(rk), sv(rv), sv(rg), cos, sin,
      jnp.asarray(dmat), jnp.asarray(qdec), jnp.asarray(kdec)).reshape(nb * seq, HEAD_W)


def _outproj_router_kernel(x_ref, oa_ref, ob_ref, oc_ref, wo_ref, g1_ref, nw_ref, sc_ref, sh_ref,
                           wr_ref, br_ref, tri_ref, upper_ref,
                           xo_ref, h_ref, tw_ref, lo_ref, tab_ref, cnt_ref, *, tm):
    @pl.when(pl.program_id(0) == 0)
    def _():
        cnt_ref[...] = jnp.zeros_like(cnt_ref)

    wa, wb = oa_ref.shape[1], ob_ref.shape[1]
    mix = (_dot(oa_ref[...], wo_ref[0:wa, :]) + _dot(ob_ref[...], wo_ref[wa:wa + wb, :])
           + _dot(oc_ref[...], wo_ref[wa + wb:, :]))
    x = x_ref[...] + g1_ref[0] * mix
    xo_ref[...] = x
    y = x * lax.rsqrt(jnp.mean(x * x, axis=-1, keepdims=True) + EPS) * nw_ref[...]
    h = y * (1.0 + sc_ref[0]) + sh_ref[0]
    h_ref[...] = h.astype(h_ref.dtype)

    lg = _dot_f32(h, wr_ref[...]) + br_ref[...]
    lane = lax.broadcasted_iota(jnp.int32, (tm, LANE), 1)
    sel_e, sel_v = [], []
    for _ in range(TOP_K):
        m = jnp.max(lg, axis=-1, keepdims=True)
        idx = jnp.min(jnp.where(lg == m, lane, LANE), axis=-1, keepdims=True)
        sel_e.append(idx)
        sel_v.append(m)
        lg = jnp.where(lane == idx, -jnp.inf, lg)
    ex = [jnp.exp(v - sel_v[0]) for v in sel_v]
    den = ex[0] + ex[1] + ex[2] + ex[3]
    hot = [(lane == idx) for idx in sel_e]
    onehot = jnp.zeros((tm, LANE), F32)
    for hk in hot:
        onehot = onehot + jnp.where(hk, 1.0, 0.0)
    in_tile = _dot(tri_ref[...], onehot.astype(BF16))
    earlier = cnt_ref[...]
    tile_cnt = jnp.floor((jnp.sum(onehot, axis=0, keepdims=True) + (RUN_ALIGN - 1.0)) * (1.0 / RUN_ALIGN)) * RUN_ALIGN
    tile_start = _dot_f32(jnp.broadcast_to(tile_cnt, (8, LANE)), upper_ref[...])[0:1]
    local = tile_start + in_tile
    tw = jnp.zeros((tm, LANE), F32)
    lo = jnp.zeros((tm, LANE), jnp.int32)
    for kk in range(TOP_K):
        lrow = jnp.sum(jnp.where(hot[kk], local, 0.0), axis=-1, keepdims=True).astype(jnp.int32)
        tw = jnp.where(lane == kk, ex[kk] / den, tw)
        lo = jnp.where(lane == kk, lrow, lo)
    tw_ref[...] = tw
    lo_ref[...] = lo
    tab_ref[0, 0:1, :] = tile_start.astype(jnp.int32)
    tab_ref[0, 1:2, :] = tile_cnt.astype(jnp.int32)
    tab_ref[0, 2:3, :] = earlier.astype(jnp.int32)
    tab_ref[0, 3:8, :] = jnp.zeros((5, LANE), jnp.int32)
    cnt_ref[...] = earlier + tile_cnt


def _outproj_router_call(x, oa, ob, oc, wo, layer, g1, nw, sc, sh, wr, br, seq):
    t, d = x.shape
    tm = min(ROUTE_TILE, seq)
    per_b = seq // tm
    tri = np.tril(np.ones((tm, tm), np.float32), -1)
    upper = np.triu(np.ones((LANE, LANE), np.float32), 1)
    row = lambda w: pl.BlockSpec((tm, w), lambda i: (i, 0))
    const = lambda shape: pl.BlockSpec(shape, lambda i: (0,) * len(shape))
    perb = pl.BlockSpec((1, 1, d), lambda i: (i // per_b, 0, 0))
    n_tiles = t // tm
    return pl.pallas_call(
        functools.partial(_outproj_router_kernel, tm=tm),
        grid=(n_tiles,),
        in_specs=[row(d), row(oa.shape[1]), row(ob.shape[1]), row(oc.shape[1]),
                  pl.BlockSpec((None,) + wo.shape[1:], lambda i: (layer, 0, 0)), perb,
                  const((1, d)), perb, perb, const(wr.shape), const(br.shape), const(tri.shape),
                  const(upper.shape)],
        out_specs=[row(d), row(d), row(LANE), row(LANE),
                   pl.BlockSpec((1, 8, LANE), lambda i: (i, 0, 0)), const((1, LANE))],
        out_shape=[jax.ShapeDtypeStruct((t, d), F32), jax.ShapeDtypeStruct((t, d), BF16),
                   jax.ShapeDtypeStruct((t, LANE), F32), jax.ShapeDtypeStruct((t, LANE), jnp.int32),
                   jax.ShapeDtypeStruct((n_tiles, 8, LANE), jnp.int32), jax.ShapeDtypeStruct((1, LANE), F32)],
        compiler_params=_cparams(("arbitrary",)),
        name="outproj_norm2_router",
    )(x, oa, ob, oc, wo, g1, nw.reshape(1, d), sc, sh, wr, br, jnp.asarray(tri, BF16), jnp.asarray(upper))


def _run_copies(tab_ref, at_ref, tile, tm, make_copy, act):
    sizes = [s for s in (1 << p for p in range(tm.bit_length() - 1, -1, -1)) if s >= RUN_ALIGN]

    def per_expert(e, carry):
        first = tab_ref[tile, 0, e]
        length = tab_ref[tile, 1, e]
        at = at_ref[tile, e]
        done = jnp.int32(0)
        for size in sizes:
            part = length & size

            @pl.when(part != 0)
            def _():
                act(make_copy(pl.multiple_of(first + done, RUN_ALIGN), pl.multiple_of(at + done, RUN_ALIGN), size))
            done = done + part
        return carry

    lax.fori_loop(0, N_EXPERTS, per_expert, 0)


def _local_rows(tm):
    return TOP_K * tm + N_EXPERTS * RUN_ALIGN


def _dispatch_kernel(tab_ref, at_ref, h_ref, lo_ref, buf_in_ref, buf_ref, st_ref, sem, *, tm, n_tiles):
    del buf_in_ref
    i = pl.program_id(0)
    slot = i % 2

    def copies(tile, s, act):
        _run_copies(tab_ref, at_ref, tile, tm,
                    lambda row, at, size: pltpu.make_async_copy(st_ref.at[s, pl.ds(row, size)],
                                                                buf_ref.at[pl.ds(at, size)], sem.at[s]), act)

    @pl.when(i >= 2)
    def _():
        copies(i - 2, slot, lambda cp: cp.wait())

    col = lax.broadcasted_iota(jnp.int32, (tm, st_ref.shape[1]), 1)
    lo = lo_ref[...]
    place = jnp.zeros(col.shape, F32)
    for kk in range(TOP_K):
        place = jnp.where(col == lo[:, kk:kk + 1], 1.0, place)
    st_ref[slot] = _dot_tn(place.astype(BF16), h_ref[...].astype(BF16))
    copies(i, slot, lambda cp: cp.start())

    @pl.when(i == n_tiles - 1)
    def _():
        if n_tiles >= 2:
            copies(i - 1, 1 - slot, lambda cp: cp.wait())
        copies(i, slot, lambda cp: cp.wait())


def _dispatch_call(h, tab, run_at, lo, buf):
    t, d = h.shape
    n_slots = buf.shape[0]
    tm = min(ROUTE_TILE, t)
    n_tiles = t // tm
    grid_spec = pltpu.PrefetchScalarGridSpec(
        num_scalar_prefetch=2,
        grid=(n_tiles,),
        in_specs=[pl.BlockSpec((tm, d), lambda i, tb, sr: (i, 0)),
                  pl.BlockSpec((tm, LANE), lambda i, tb, sr: (i, 0)),
                  pl.BlockSpec(memory_space=pl.ANY)],
        out_specs=pl.BlockSpec(memory_space=pl.ANY),
        scratch_shapes=[pltpu.VMEM((2, _local_rows(tm), d), F32), pltpu.SemaphoreType.DMA((2,))],
    )
    return pl.pallas_call(
        functools.partial(_dispatch_kernel, tm=tm, n_tiles=n_tiles),
        grid_spec=grid_spec,
        out_shape=jax.ShapeDtypeStruct((n_slots, d), F32),
        input_output_aliases={4: 0},
        compiler_params=_cparams(("arbitrary",)),
        name="moe_dispatch",
    )(tab, run_at, h, lo, buf)


def _expert_kernel(be_ref, nx_ref, nu_ref, x_ref, wgu_ref, bgu_ref, wdn_ref, bdn_ref, o_ref,
                   wgu_f32, wdn_f32, wgu_bf, wdn_bf, sem, *, d_ff, layer):
    i = pl.program_id(0)
    live = i < nu_ref[0]
    e = be_ref[i]
    e_next = nx_ref[i]

    def weight_copies(expert):
        return (pltpu.make_async_copy(wgu_ref.at[layer, expert], wgu_f32, sem.at[0]),
                pltpu.make_async_copy(wdn_ref.at[layer, expert], wdn_f32, sem.at[1]))

    @pl.when(i == 0)
    def _():
        for cp in weight_copies(e):
            cp.start()

    @pl.when(jnp.logical_and(live, jnp.logical_or(i == 0, e != be_ref[jnp.maximum(i - 1, 0)])))
    def _():
        for cp in weight_copies(e):
            cp.wait()
        wgu_bf[...] = wgu_f32[...].astype(BF16)
        wdn_bf[...] = wdn_f32[...].astype(BF16)

        @pl.when(e_next != e)
        def _():
            for cp in weight_copies(e_next):
                cp.start()

    @pl.when(live)
    def _():
        gu = _dot(x_ref[...].astype(BF16), wgu_bf[...]) + bgu_ref[0, 0]
        gate = jnp.minimum(gu[:, :d_ff], SWIGLU_LIMIT)
        up = jnp.clip(gu[:, d_ff:], -SWIGLU_LIMIT, SWIGLU_LIMIT)
        act = (up + 1.0) * gate * _sigmoid(SWIGLU_ALPHA * gate)
        o_ref[...] = _dot(act.astype(BF16), wdn_bf[...]) + bdn_ref[0, 0]

    @pl.when(jnp.logical_not(live))
    def _():
        o_ref[...] = jnp.zeros_like(o_ref)


def _expert_call(buf, block_e, next_e, n_used, wgu, bgu, wdn, bdn, layer):
    n_slots, d = buf.shape
    n_blocks = n_slots // MOE_BLOCK
    depth, ne, _, f2 = wgu.shape
    d_ff = f2 // 2
    blk = lambda i, be, nx, nu: (jnp.minimum(i, nu[0] - 1), 0)
    exp4 = lambda i, be, nx, nu: (layer, be[jnp.minimum(i, nu[0] - 1)], 0, 0)
    grid_spec = pltpu.PrefetchScalarGridSpec(
        num_scalar_prefetch=3,
        grid=(n_blocks,),
        in_specs=[pl.BlockSpec((MOE_BLOCK, d), blk),
                  pl.BlockSpec(memory_space=pl.ANY),
                  pl.BlockSpec((1, 1, 1, f2), exp4),
                  pl.BlockSpec(memory_space=pl.ANY),
                  pl.BlockSpec((1, 1, 1, d), exp4)],
        out_specs=pl.BlockSpec((MOE_BLOCK, d), lambda i, be, nx, nu: (i, 0)),
        scratch_shapes=[pltpu.VMEM((d, f2), F32), pltpu.VMEM((d_ff, d), F32),
                        pltpu.VMEM((d, f2), BF16), pltpu.VMEM((d_ff, d), BF16),
                        pltpu.SemaphoreType.DMA((2,))],
    )
    return pl.pallas_call(
        functools.partial(_expert_kernel, d_ff=d_ff, layer=layer),
        grid_spec=grid_spec,
        out_shape=jax.ShapeDtypeStruct((n_slots, d), F32),
        compiler_params=_cparams(("arbitrary",)),
        name="moe_experts",
    )(block_e, next_e, n_used, buf, wgu, bgu.reshape(depth, ne, 1, f2), wdn, bdn.reshape(depth, ne, 1, d))


def _combine_kernel(tab_ref, at_ref, yb_ref, lo_ref, w_ref, x_ref, g2_ref, o_ref, rs_ref, sem, *, tm):
    i = pl.program_id(0)
    n = pl.num_programs(0)
    rows = rs_ref.shape[1]

    def run_copies(tile, s, act):
        _run_copies(tab_ref, at_ref, tile, tm,
                    lambda row, at, size: pltpu.make_async_copy(yb_ref.at[pl.ds(at, size)],
                                                                rs_ref.at[s, pl.ds(row, size)], sem.at[s]), act)

    @pl.when(i == 0)
    def _():
        rs_ref[...] = jnp.zeros_like(rs_ref)
        run_copies(0, 0, lambda cp: cp.start())

    @pl.when(i + 1 < n)
    def _():
        run_copies(i + 1, (i + 1) % 2, lambda cp: cp.start())

    slot = i % 2
    run_copies(i, slot, lambda cp: cp.wait())
    r = rs_ref[slot].astype(BF16)
    col = lax.broadcasted_iota(jnp.int32, (tm, rows), 1)
    w = w_ref[...]
    lo = lo_ref[...]
    pw = jnp.zeros((tm, rows), F32)
    for kk in range(TOP_K):
        pw = jnp.where(col == lo[:, kk:kk + 1], w[:, kk:kk + 1], pw)
    y = _dot(pw.astype(BF16), r)
    o_ref[...] = x_ref[...] + g2_ref[0] * y


def _combine_call(yb, tab, src, lo, tw, x, g2, seq):
    t, d = x.shape
    tm = min(ROUTE_TILE, seq)
    per_b = seq // tm
    grid_spec = pltpu.PrefetchScalarGridSpec(
        num_scalar_prefetch=2,
        grid=(t // tm,),
        in_specs=[pl.BlockSpec(memory_space=pl.ANY),
                  pl.BlockSpec((tm, LANE), lambda i, tb, sr: (i, 0)),
                  pl.BlockSpec((tm, LANE), lambda i, tb, sr: (i, 0)),
                  pl.BlockSpec((tm, d), lambda i, tb, sr: (i, 0)),
                  pl.BlockSpec((1, 1, d), lambda i, tb, sr: (i // per_b, 0, 0))],
        out_specs=pl.BlockSpec((tm, d), lambda i, tb, sr: (i, 0)),
        scratch_shapes=[pltpu.VMEM((2, _local_rows(tm), d), F32), pltpu.SemaphoreType.DMA((2,))],
    )
    return pl.pallas_call(
        functools.partial(_combine_kernel, tm=tm),
        grid_spec=grid_spec,
        out_shape=jax.ShapeDtypeStruct((t, d), F32),
        compiler_params=_cparams(("arbitrary",)),
        name="moe_combine",
    )(tab, src, yb, lo, tw, x, g2)


def _final_norm_kernel(x_ref, w_ref, o_ref):
    x = x_ref[...]
    o_ref[...] = x * lax.rsqrt(jnp.mean(x * x, axis=-1, keepdims=True) + EPS) * w_ref[...]


def _final_norm_call(x, w):
    t, d = x.shape
    tm = min(512, t)
    return pl.pallas_call(
        _final_norm_kernel,
        grid=(t // tm,),
        in_specs=[pl.BlockSpec((tm, d), lambda i: (i, 0)), pl.BlockSpec((1, d), lambda i: (0, 0))],
        out_specs=pl.BlockSpec((tm, d), lambda i: (i, 0)),
        out_shape=jax.ShapeDtypeStruct((t, d), F32),
        compiler_params=_cparams(("parallel",)),
        name="final_norm",
    )(x, w.reshape(1, d))


def _split_heads(w, hd):
    return w.reshape(w.shape[:-1] + (N_HEADS, hd))


def _pad_last(w, lo, hi):
    return jnp.pad(w, [(0, 0)] * (w.ndim - 1) + [(lo, hi)])


def _tile_cols(w, hd):
    return _pad_last(_split_heads(w, hd), 0, LANE - hd).reshape(w.shape[:-1] + (HEAD_W,))


def _pair_cols(w, hd):
    return _pad_last(_split_heads(w, hd), 0, HALF - hd).reshape(w.shape[:-1] + (PAIR_W,))


def _query_cols(w, hd):
    w = _split_heads(w, hd)
    tiles = [_pad_last(w[..., h, :], (h % 2) * HALF, LANE - (h % 2) * HALF - hd) for h in range(N_HEADS)]
    return jnp.concatenate(tiles, axis=-1)


def _tile_rows(w, hd):
    return jnp.swapaxes(_tile_cols(jnp.swapaxes(w, -1, -2), hd), -1, -2)


IN_PARTS = (("gq", GLA_DK, _query_cols), ("gk", GLA_DK, _pair_cols), ("gv", GLA_DV, _tile_cols),
            ("gg", GLA_DV, _tile_cols), ("ga", None, None),
            ("hq", HG_DK, _query_cols), ("hf", HG_DK, _pair_cols), ("hi", HG_DV, _pair_cols),
            ("hg", HG_DV, _pair_cols),
            ("rq", RET_DK, _query_cols), ("rk", RET_DK, _pair_cols), ("rv", RET_DV, _tile_cols),
            ("rg", RET_DV, _tile_cols))
F32_PARTS = ("ga", "hf")


def _layout_w_in(w_in):
    cols, off = [], 0
    for _, hd, layout in IN_PARTS:
        if layout is None:
            part = _pad_last(w_in[..., off:off + GLA_RANK], 0, LANE - GLA_RANK)
            off += GLA_RANK
        else:
            part = layout(w_in[..., off:off + N_HEADS * hd], hd)
            off += N_HEADS * hd
        cols.append(part)
    return jnp.concatenate(cols, axis=-1).astype(BF16), tuple(int(p.shape[-1]) for p in cols)


def kernel(x, c, positions, w_mod, b_mod, norm1_w, w_in, gla_wa2, gla_ba2, hg_lb, gla_norm_w, hg_norm_w,
           w_out, norm2_w, w_r, b_r, w_gu, b_gu, w_dn, b_dn, final_norm_w):
    nb, seq, d = x.shape
    t = nb * seq
    depth = w_mod.shape[0]
    dtypes = tuple(F32 if name in F32_PARTS else BF16 for name, _, _ in IN_PARTS)

    mod = _mod_call(c, w_mod, b_mod)
    cos, sin = _rope_call(positions)
    lb_all = _lb_call(hg_lb)

    n_assign = t * TOP_K
    n_run_pad = N_EXPERTS * (RUN_ALIGN - 1) * (t // min(ROUTE_TILE, seq))
    n_blocks = (n_assign + n_run_pad + MOE_BLOCK - 1) // MOE_BLOCK + N_EXPERTS
    n_slots = n_blocks * MOE_BLOCK

    w_cat, widths = _layout_w_in(w_in)
    n_gla, n_hg = N_HEADS * GLA_DV, N_HEADS * HG_DV
    wo_p = jnp.concatenate([_tile_rows(w_out[:, :n_gla], GLA_DV), w_out[:, n_gla:n_gla + n_hg],
                            _tile_rows(w_out[:, n_gla + n_hg:], RET_DV)], axis=1).astype(BF16)
    wa2p = jnp.pad(_pair_cols(gla_wa2, GLA_DK), ((0, 0), (0, LANE - GLA_RANK), (0, 0)))
    ba2p = _pair_cols(gla_ba2, GLA_DK)
    gnw = _tile_cols(gla_norm_w, GLA_DV)
    wr_p = _pad_last(w_r, 0, LANE - N_EXPERTS)
    br_p = jnp.pad(b_r, ((0, 0), (0, LANE - N_EXPERTS)), constant_values=NEG_BIG)

    buf = jnp.zeros((n_slots, d), F32)
    xf = x.reshape(t, d)
    for layer in range(depth):
        sh1, sc1, g1, sh2, sc2, g2 = [m.reshape(nb, 1, d) for m in jnp.split(mod[layer], 6, axis=-1)]

        parts = _inproj_call(xf, norm1_w[layer], sc1, sh1, w_cat, layer, widths, dtypes, seq)
        gq, gk, gv, gg, ga, hq, hf, hi, hg, rq, rk, rv, rg = parts
        o_gla = _gla_call(gq, gk, gv, gg, ga, wa2p[layer], ba2p[layer:layer + 1], gnw[layer:layer + 1], nb, seq)
        o_hg = _hg_call(hq, hf, hi, hg, lb_all[layer:layer + 1], hg_norm_w[layer:layer + 1], nb, seq)
        o_ret = _ret_call(rq, rk, rv, rg, cos, sin, nb, seq)
        xf, h2, tw, lo, tab, cnt = _outproj_router_call(
            xf, o_gla, o_hg, o_ret, wo_p, layer, g1, norm2_w[layer], sc2, sh2, wr_p[layer],
            br_p[layer:layer + 1], seq)

        counts = cnt[0, :N_EXPERTS].astype(jnp.int32)
        padded = (counts + MOE_BLOCK - 1) // MOE_BLOCK * MOE_BLOCK
        pends = jnp.cumsum(padded)
        pstarts = pends - padded
        block_start = jnp.arange(n_blocks, dtype=jnp.int32) * MOE_BLOCK
        block_e = jnp.minimum(jnp.sum((pends[None, :] <= block_start[:, None]).astype(jnp.int32), axis=1),
                              N_EXPERTS - 1)
        n_used = (pends[-1:] // MOE_BLOCK).astype(jnp.int32)
        run_at = tab[:, 2, :] + jnp.pad(pstarts, (0, LANE - N_EXPERTS))[None, :]

        ids = jnp.arange(N_EXPERTS, dtype=jnp.int32)
        later = jnp.where((ids[None, :] > ids[:, None]) & (padded[None, :] > 0), ids[None, :], N_EXPERTS)
        follower = jnp.min(later, axis=1)
        follower = jnp.where(follower == N_EXPERTS, ids, follower)
        next_e = jnp.sum(jnp.where(block_e[:, None] == ids[None, :], follower[None, :], 0), axis=1)

        buf = _dispatch_call(h2, tab, run_at, lo, buf)
        yb = _expert_call(buf, block_e, next_e, n_used, w_gu, b_gu, w_dn, b_dn, layer)
        xf = _combine_call(yb, tab, run_at, lo, tw, xf, g2, seq)

    return _final_norm_call(xf, final_norm_w).reshape(nb, seq, d)
```

```python
import functools

import numpy as np
import jax
import jax.numpy as jnp
from jax import lax
from jax.experimental import pallas as pl
from jax.experimental.pallas import tpu as pltpu

F32 = jnp.float32
BF16 = jnp.bfloat16

N_HEADS = 4
GLA_DK, GLA_DV, GLA_RANK, GLA_TAU = 48, 96, 16, 16.0
HG_DK, HG_DV = 64, 64
RET_DK, RET_DV = 48, 96
ROPE_BASE = 10000.0
N_EXPERTS, TOP_K = 32, 4
SWIGLU_LIMIT, SWIGLU_ALPHA = 7.0, 1.702
MOE_BLOCK = 512
ROUTE_TILE = 512
RUN_ALIGN = 8
EPS = 1e-6

LANE = 128
HALF = LANE // 2
N_PAIRS = N_HEADS // 2
HEAD_W = N_HEADS * LANE
PAIR_W = N_PAIRS * LANE
VMEM_LIMIT = 56 * 1024 * 1024
NEG_BIG = -1e30


def _cparams(sem):
    return pltpu.CompilerParams(dimension_semantics=sem, vmem_limit_bytes=VMEM_LIMIT)


def _dot(a, b):
    return jnp.dot(a, b, preferred_element_type=F32)


def _dot_nt(a, b):
    return lax.dot_general(a, b, (((1,), (1,)), ((), ())), preferred_element_type=F32)


def _dot_tn(a, b):
    return lax.dot_general(a, b, (((0,), (0,)), ((), ())), preferred_element_type=F32)


def _split2(a):
    hi = a.astype(BF16)
    return hi, (a - hi.astype(F32)).astype(BF16)


def _dot_f32(a, b):
    a_hi, a_lo = _split2(a)
    b_hi, b_lo = _split2(b)
    return _dot(a_hi, b_hi) + _dot(a_hi, b_lo) + _dot(a_lo, b_hi)


def _sigmoid(x):
    return 1.0 / (1.0 + jnp.exp(-x))


def _log_sigmoid(x):
    return jnp.minimum(x, 0.0) - jnp.log1p(jnp.exp(-jnp.abs(x)))


def _mod_kernel(c_ref, w_ref, b_ref, o_ref):
    c = c_ref[...]
    o_ref[0] = _dot_f32(c * _sigmoid(c), w_ref[0]) + b_ref[0]


def _mod_call(c, w_mod, b_mod):
    depth, d, d6 = w_mod.shape
    nb = c.shape[0]
    rows = 8
    c_pad = jnp.zeros((rows, d), F32).at[:nb].set(c)
    out = pl.pallas_call(
        _mod_kernel,
        grid=(depth, d6 // d),
        in_specs=[
            pl.BlockSpec((rows, d), lambda l, j: (0, 0)),
            pl.BlockSpec((1, d, d), lambda l, j: (l, 0, j)),
            pl.BlockSpec((1, 1, d), lambda l, j: (l, 0, j)),
        ],
        out_specs=pl.BlockSpec((1, rows, d), lambda l, j: (l, 0, j)),
        out_shape=jax.ShapeDtypeStruct((depth, rows, d6), F32),
        compiler_params=_cparams(("parallel", "parallel")),
        name="mod",
    )(c_pad, w_mod, b_mod.reshape(depth, 1, d6))
    return out[:, :nb]


def _rope_kernel(pos_ref, freq_ref, cos_ref, sin_ref):
    ang = pos_ref[0] * freq_ref[...]
    cos_ref[0] = jnp.cos(ang)
    sin_ref[0] = jnp.sin(ang)


def _rope_call(positions):
    nb, l = positions.shape
    half = RET_DK // 2
    inv = (ROPE_BASE ** (-np.arange(half, dtype=np.float32) / half)).astype(np.float32)
    freq = np.zeros((1, LANE), np.float32)
    for base in (0, HALF):
        freq[0, base:base + half] = inv
        freq[0, base + half:base + 2 * half] = inv
    tl = min(l, 512)
    pos = positions.astype(F32).reshape(nb, l, 1)
    shp = jax.ShapeDtypeStruct((nb, l, LANE), F32)
    return pl.pallas_call(
        _rope_kernel,
        grid=(nb, l // tl),
        in_specs=[pl.BlockSpec((1, tl, 1), lambda b, i: (b, i, 0)),
                  pl.BlockSpec((1, LANE), lambda b, i: (0, 0))],
        out_specs=[pl.BlockSpec((1, tl, LANE), lambda b, i: (b, i, 0))] * 2,
        out_shape=[shp, shp],
        compiler_params=_cparams(("parallel", "parallel")),
        name="rope_tables",
    )(pos, jnp.asarray(freq))


def _lb_kernel(p_ref, o_ref):
    p = p_ref[...]
    depth = p.shape[0]
    m = jnp.max(p, axis=0, keepdims=True)
    e = jnp.exp(p - m)
    sm = e / jnp.sum(e, axis=0, keepdims=True)
    acc = jnp.zeros_like(sm[0:1])
    for i in range(depth):
        if i > 0:
            acc = acc + sm[i:i + 1]
        o_ref[i:i + 1, :] = acc


def _lb_call(hg_lb):
    return pl.pallas_call(
        _lb_kernel,
        out_shape=jax.ShapeDtypeStruct(hg_lb.shape, F32),
        name="hg_lower_bounds",
    )(hg_lb.astype(F32))


def _inproj_kernel(x_ref, nw_ref, sc_ref, sh_ref, w_ref, *out_refs, widths):
    x = x_ref[...]
    y = x * lax.rsqrt(jnp.mean(x * x, axis=-1, keepdims=True) + EPS) * nw_ref[...]
    hb = (y * (1.0 + sc_ref[0]) + sh_ref[0]).astype(BF16)
    off = 0
    for o_ref, wd in zip(out_refs, widths):
        o_ref[...] = _dot(hb, w_ref[:, off:off + wd]).astype(o_ref.dtype)
        off += wd


def _inproj_call(x, norm_w, sc, sh, w_cat, layer, widths, dtypes, seq):
    t, d = x.shape
    tm = min(512, seq)
    per_b = seq // tm
    return pl.pallas_call(
        functools.partial(_inproj_kernel, widths=widths),
        grid=(t // tm,),
        in_specs=[
            pl.BlockSpec((tm, d), lambda i: (i, 0)),
            pl.BlockSpec((1, d), lambda i: (0, 0)),
            pl.BlockSpec((1, 1, d), lambda i: (i // per_b, 0, 0)),
            pl.BlockSpec((1, 1, d), lambda i: (i // per_b, 0, 0)),
            pl.BlockSpec((None,) + w_cat.shape[1:], lambda i: (layer, 0, 0)),
        ],
        out_specs=[pl.BlockSpec((tm, wd), lambda i: (i, 0)) for wd in widths],
        out_shape=[jax.ShapeDtypeStruct((t, wd), dt) for wd, dt in zip(widths, dtypes)],
        compiler_params=_cparams(("parallel",)),
        name="norm1_inproj",
    )(x, norm_w.reshape(1, d), sc, sh, w_cat)


def _level_tables(c):
    nl = int(np.log2(c))
    assert 1 << nl == c
    idx = np.arange(c)
    mats, masks = [], [np.eye(c, dtype=np.float32)]
    t = idx[None, :]
    i = idx[:, None]
    for lvl in range(nl):
        h = c >> (lvl + 1)
        blk, pos = idx // (2 * h), idx % (2 * h)
        m = (blk * 2 * h + h - 1)[:, None]
        right = (pos >= h)[:, None]
        a = np.where(right, (t > m) & (t <= i), (t > i) & (t <= m))
        mats.append(a.astype(np.float32))
        same = blk[:, None] == blk[None, :]
        masks.append((same & right & (pos < h)[None, :]).astype(np.float32))
    mats.append((t <= i).astype(np.float32))
    mats.append((t > i).astype(np.float32))
    return np.concatenate(mats, 0), np.stack(masks, 0), nl


def _pair_decays(g, mall_ref):
    gs = _dot(mall_ref[...], jnp.concatenate(_split2(g), axis=1))
    return jnp.exp(gs[:, :LANE] + gs[:, LANE:])


def _pair_keys(k, e, c, nl):
    levels = [(k * e[lvl * c:(lvl + 1) * c]).astype(BF16) for lvl in range(nl)]
    return k.astype(BF16), levels, (k * e[(nl + 1) * c:(nl + 2) * c]).astype(BF16)


def _gated_head(q, kb, klv, e, v, st, masks_ref, c, nl):
    scores = masks_ref[0] * _dot_nt(q.astype(BF16), kb)
    for lvl in range(nl):
        scores = scores + masks_ref[lvl + 1] * _dot_nt((q * e[lvl * c:(lvl + 1) * c]).astype(BF16), klv[lvl])
    q_in = (q * e[nl * c:(nl + 1) * c]).astype(BF16)
    return _dot(scores.astype(BF16), v) + _dot_nt(q_in, st.astype(BF16))


def _gla_kernel(q_ref, k_ref, v_ref, gg_ref, ga_ref, wa2_ref, ba2_ref, nw_ref, mall_ref, masks_ref,
                o_ref, st_ref, *, c, nl):
    @pl.when(pl.program_id(0) == 0)
    def _():
        st_ref[...] = jnp.zeros_like(st_ref)

    for b in range(q_ref.shape[0]):
        log_alpha = _log_sigmoid(_dot_f32(ga_ref[b], wa2_ref[...]) + ba2_ref[...]) * (1.0 / GLA_TAU)
        for pr in range(N_PAIRS):
            psl = slice(pr * LANE, (pr + 1) * LANE)
            e = _pair_decays(log_alpha[:, psl], mall_ref)
            kb, klv, k_end = _pair_keys(k_ref[b, :, psl].astype(F32), e, c, nl)
            dec = e[(nl + 1) * c - 1:(nl + 1) * c, :]
            for hd in (2 * pr, 2 * pr + 1):
                sl = slice(hd * LANE, (hd + 1) * LANE)
                q = q_ref[b, :, sl].astype(F32) * (GLA_DK ** -0.5)
                v = v_ref[b, :, sl]
                st = st_ref[b, hd]
                o = _gated_head(q, kb, klv, e, v, st, masks_ref, c, nl)
                st_ref[b, hd] = dec * st + _dot_tn(v, k_end)
                y = o * lax.rsqrt(jnp.sum(o * o, axis=-1, keepdims=True) * (1.0 / GLA_DV) + EPS) * nw_ref[:, sl]
                gate = gg_ref[b, :, sl].astype(F32)
                o_ref[b, :, sl] = (y * gate * _sigmoid(gate)).astype(o_ref.dtype)


def _hg_kernel(q_ref, f_ref, v_ref, og_ref, lb_ref, nw_ref, mall_ref, masks_ref,
               o_ref, st_ref, *, c, nl):
    @pl.when(pl.program_id(0) == 0)
    def _():
        st_ref[...] = jnp.zeros_like(st_ref)

    low = lax.broadcasted_iota(jnp.int32, (c, LANE), 1) < HALF
    for b in range(q_ref.shape[0]):
        for pr in range(N_PAIRS):
            psl = slice(pr * LANE, (pr + 1) * LANE)
            lb = lb_ref[:, psl]
            hf = f_ref[b, :, psl]
            la = jnp.log(lb)
            lc = jnp.log1p(-lb) + _log_sigmoid(hf)
            log_f = jnp.maximum(la, lc) + jnp.log1p(jnp.exp(-jnp.abs(la - lc)))
            e = _pair_decays(log_f, mall_ref)
            kb, klv, k_end = _pair_keys((1.0 - lb) * _sigmoid(-hf), e, c, nl)
            v = v_ref[b, :, psl]
            st = st_ref[b, pr]
            o_even = _gated_head(q_ref[b, :, (2 * pr) * LANE:(2 * pr + 1) * LANE].astype(F32),
                                 kb, klv, e, v, st, masks_ref, c, nl)
            o_odd = _gated_head(q_ref[b, :, (2 * pr + 1) * LANE:(2 * pr + 2) * LANE].astype(F32),
                                kb, klv, e, v, st, masks_ref, c, nl)
            o = jnp.where(low, o_even, o_odd)
            st_ref[b, pr] = e[(nl + 1) * c - 1:(nl + 1) * c, :] * st + _dot_tn(v, k_end)
            sq = o * o
            ms = jnp.where(low, jnp.sum(jnp.where(low, sq, 0.0), axis=-1, keepdims=True),
                           jnp.sum(jnp.where(low, 0.0, sq), axis=-1, keepdims=True)) * (1.0 / HG_DV)
            y = o * lax.rsqrt(ms + EPS) * nw_ref[:, psl]
            o_ref[b, :, psl] = (y * _sigmoid(og_ref[b, :, psl].astype(F32))).astype(o_ref.dtype)


def _tok_spec(nb, c, width):
    return pl.BlockSpec((nb, c, width), lambda i: (0, i, 0))


def _const_spec(shape):
    nd = len(shape)
    return pl.BlockSpec(shape, lambda i: (0,) * nd)


def _seq_view(a, nb, seq):
    return a.reshape(nb, seq, a.shape[-1])


def _gla_call(gq, gk, gv, gg, ga, wa2p, ba2p, nwp, nb, seq):
    c = min(128, seq)
    n = seq // c
    mall, masks, nl = _level_tables(c)
    sv = lambda a: _seq_view(a, nb, seq)
    return pl.pallas_call(
        functools.partial(_gla_kernel, c=c, nl=nl),
        grid=(n,),
        in_specs=[_tok_spec(nb, c, HEAD_W), _tok_spec(nb, c, PAIR_W), _tok_spec(nb, c, HEAD_W),
                  _tok_spec(nb, c, HEAD_W), _tok_spec(nb, c, LANE),
                  _const_spec(wa2p.shape), _const_spec(ba2p.shape), _const_spec(nwp.shape),
                  _const_spec(mall.shape), _const_spec(masks.shape)],
        out_specs=_tok_spec(nb, c, HEAD_W),
        out_shape=jax.ShapeDtypeStruct((nb, seq, HEAD_W), BF16),
        scratch_shapes=[pltpu.VMEM((nb, N_HEADS, LANE, LANE), F32)],
        compiler_params=_cparams(("arbitrary",)),
        name="gla_recurrence",
    )(sv(gq), sv(gk), sv(gv), sv(gg), sv(ga), wa2p, ba2p, nwp,
      jnp.asarray(mall, BF16), jnp.asarray(masks)).reshape(nb * seq, HEAD_W)


def _hg_call(hq, hf, hi, hg, lbp, nwp, nb, seq):
    c = min(128, seq)
    n = seq // c
    mall, masks, nl = _level_tables(c)
    sv = lambda a: _seq_view(a, nb, seq)
    return pl.pallas_call(
        functools.partial(_hg_kernel, c=c, nl=nl),
        grid=(n,),
        in_specs=[_tok_spec(nb, c, HEAD_W), _tok_spec(nb, c, PAIR_W), _tok_spec(nb, c, PAIR_W),
                  _tok_spec(nb, c, PAIR_W),
                  _const_spec(lbp.shape), _const_spec(nwp.shape),
                  _const_spec(mall.shape), _const_spec(masks.shape)],
        out_specs=_tok_spec(nb, c, PAIR_W),
        out_shape=jax.ShapeDtypeStruct((nb, seq, PAIR_W), BF16),
        scratch_shapes=[pltpu.VMEM((nb, N_PAIRS, LANE, LANE), F32)],
        compiler_params=_cparams(("arbitrary",)),
        name="hgrn2_recurrence",
    )(sv(hq), sv(hf), sv(hi), sv(hg), lbp, nwp,
      jnp.asarray(mall, BF16), jnp.asarray(masks)).reshape(nb * seq, PAIR_W)


def _ret_tables(c):
    hs = np.arange(N_HEADS, dtype=np.float64)
    log_gamma = np.log(1.0 - np.exp2(-5.0 - hs))
    idx = np.arange(c, dtype=np.float64)
    rel = idx[:, None] - idx[None, :]
    dmat = np.where(rel >= 0, np.exp(log_gamma[:, None, None] * np.maximum(rel, 0.0)), 0.0)
    qdec = np.exp(log_gamma[:, None] * (idx + 1.0))
    kdec = np.exp(log_gamma[:, None] * (c - 1.0 - idx))
    cdec = np.exp(log_gamma * c)
    qfull = np.broadcast_to(qdec[:, :, None], (N_HEADS, c, LANE)).astype(np.float32)
    kpair = np.zeros((N_PAIRS, c, LANE), np.float32)
    for hd in range(N_HEADS):
        kpair[hd // 2, :, (hd % 2) * HALF:(hd % 2 + 1) * HALF] = kdec[hd][:, None]
    return dmat.astype(np.float32), qfull, kpair, [float(np.float32(v)) for v in cdec]


def _ret_kernel(q_ref, k_ref, v_ref, og_ref, cos_ref, sin_ref, dmat_ref, qdec_ref, kdec_ref,
                o_ref, st_ref, *, c, cdec):
    @pl.when(pl.program_id(0) == 0)
    def _():
        st_ref[...] = jnp.zeros_like(st_ref)

    half = RET_DK // 2
    lane = lax.broadcasted_iota(jnp.int32, (c, LANE), 1)
    first = (lane & (HALF - 1)) < half

    for b in range(q_ref.shape[0]):
        cos = cos_ref[b]
        sin = sin_ref[b]

        def rotary(t):
            rot = jnp.where(first, -pltpu.roll(t, LANE - half, 1), pltpu.roll(t, half, 1))
            return t * cos + rot * sin

        for pr in range(N_PAIRS):
            psl = slice(pr * LANE, (pr + 1) * LANE)
            k = rotary(k_ref[b, :, psl].astype(F32))
            kb = k.astype(BF16)
            k_end = (k * kdec_ref[pr]).astype(BF16)
            for hd in (2 * pr, 2 * pr + 1):
                sl = slice(hd * LANE, (hd + 1) * LANE)
                q = rotary(q_ref[b, :, sl].astype(F32)) * (RET_DK ** -0.5)
                v = v_ref[b, :, sl]
                st = st_ref[b, hd]
                scores = _dot_nt(q.astype(BF16), kb) * dmat_ref[hd]
                o = _dot(scores.astype(BF16), v) + _dot_nt((q * qdec_ref[hd]).astype(BF16), st.astype(BF16))
                st_ref[b, hd] = cdec[hd] * st + _dot_tn(v, k_end)
                mu = jnp.sum(o, axis=-1, keepdims=True) * (1.0 / RET_DV)
                dlt = jnp.where(lane < RET_DV, o - mu, 0.0)
                var = jnp.sum(dlt * dlt, axis=-1, keepdims=True) * (1.0 / RET_DV)
                gate = og_ref[b, :, sl].astype(F32)
                o_ref[b, :, sl] = (dlt * lax.rsqrt(var + EPS) * gate * _sigmoid(gate)).astype(o_ref.dtype)


def _ret_call(rq, rk, rv, rg, cos, sin, nb, seq):
    c = min(128, seq)
    n = seq // c
    dmat, qdec, kdec, cdec = _ret_tables(c)
    sv = lambda a: _seq_view(a, nb, seq)
    return pl.pallas_call(
        functools.partial(_ret_kernel, c=c, cdec=cdec),
        grid=(n,),
        in_specs=[_tok_spec(nb, c, HEAD_W), _tok_spec(nb, c, PAIR_W), _tok_spec(nb, c, HEAD_W),
                  _tok_spec(nb, c, HEAD_W), _tok_spec(nb, c, LANE), _tok_spec(nb, c, LANE),
                  _const_spec(dmat.shape), _const_spec(qdec.shape), _const_spec(kdec.shape)],
        out_specs=_tok_spec(nb, c, HEAD_W),
        out_shape=jax.ShapeDtypeStruct((nb, seq, HEAD_W), BF16),
        scratch_shapes=[pltpu.VMEM((nb, N_HEADS, LANE, LANE), F32)],
        compiler_params=_cparams(("arbitrary",)),
        name="retention",
    )(sv(rq), sv(rk), sv(rv), sv(rg), cos, sin,
      jnp.asarray(dmat), jnp.asarray(qdec), jnp.asarray(kdec)).reshape(nb * seq, HEAD_W)


def _outproj_router_kernel(x_ref, oa_ref, ob_ref, oc_ref, wo_ref, g1_ref, nw_ref, sc_ref, sh_ref,
                           wr_ref, br_ref, tri_ref, upper_ref,
                           xo_ref, h_ref, tw_ref, lo_ref, tab_ref, cnt_ref, *, tm):
    @pl.when(pl.program_id(0) == 0)
    def _():
        cnt_ref[...] = jnp.zeros_like(cnt_ref)

    wa, wb = oa_ref.shape[1], ob_ref.shape[1]
    mix = (_dot(oa_ref[...], wo_ref[0:wa, :]) + _dot(ob_ref[...], wo_ref[wa:wa + wb, :])
           + _dot(oc_ref[...], wo_ref[wa + wb:, :]))
    x = x_ref[...] + g1_ref[0] * mix
    xo_ref[...] = x
    y = x * lax.rsqrt(jnp.mean(x * x, axis=-1, keepdims=True) + EPS) * nw_ref[...]
    h = y * (1.0 + sc_ref[0]) + sh_ref[0]
    h_ref[...] = h.astype(h_ref.dtype)

    lg = _dot_f32(h, wr_ref[...]) + br_ref[...]
    lane = lax.broadcasted_iota(jnp.int32, (tm, LANE), 1)
    sel_e, sel_v = [], []
    for _ in range(TOP_K):
        m = jnp.max(lg, axis=-1, keepdims=True)
        idx = jnp.min(jnp.where(lg == m, lane, LANE), axis=-1, keepdims=True)
        sel_e.append(idx)
        sel_v.append(m)
        lg = jnp.where(lane == idx, -jnp.inf, lg)
    ex = [jnp.exp(v - sel_v[0]) for v in sel_v]
    den = ex[0] + ex[1] + ex[2] + ex[3]
    hot = [(lane == idx) for idx in sel_e]
    onehot = jnp.zeros((tm, LANE), F32)
    for hk in hot:
        onehot = onehot + jnp.where(hk, 1.0, 0.0)
    in_tile = _dot(tri_ref[...], onehot.astype(BF16))
    earlier = cnt_ref[...]
    tile_cnt = jnp.floor((jnp.sum(onehot, axis=0, keepdims=True) + (RUN_ALIGN - 1.0)) * (1.0 / RUN_ALIGN)) * RUN_ALIGN
    tile_start = _dot_f32(jnp.broadcast_to(tile_cnt, (8, LANE)), upper_ref[...])[0:1]
    local = tile_start + in_tile
    tw = jnp.zeros((tm, LANE), F32)
    lo = jnp.zeros((tm, LANE), jnp.int32)
    for kk in range(TOP_K):
        lrow = jnp.sum(jnp.where(hot[kk], local, 0.0), axis=-1, keepdims=True).astype(jnp.int32)
        tw = jnp.where(lane == kk, ex[kk] / den, tw)
        lo = jnp.where(lane == kk, lrow, lo)
    tw_ref[...] = tw
    lo_ref[...] = lo
    tab_ref[0, 0:1, :] = tile_start.astype(jnp.int32)
    tab_ref[0, 1:2, :] = tile_cnt.astype(jnp.int32)
    tab_ref[0, 2:3, :] = earlier.astype(jnp.int32)
    tab_ref[0, 3:8, :] = jnp.zeros((5, LANE), jnp.int32)
    cnt_ref[...] = earlier + tile_cnt


def _outproj_router_call(x, oa, ob, oc, wo, layer, g1, nw, sc, sh, wr, br, seq):
    t, d = x.shape
    tm = min(ROUTE_TILE, seq)
    per_b = seq // tm
    tri = np.tril(np.ones((tm, tm), np.float32), -1)
    upper = np.triu(np.ones((LANE, LANE), np.float32), 1)
    row = lambda w: pl.BlockSpec((tm, w), lambda i: (i, 0))
    const = lambda shape: pl.BlockSpec(shape, lambda i: (0,) * len(shape))
    perb = pl.BlockSpec((1, 1, d), lambda i: (i // per_b, 0, 0))
    n_tiles = t // tm
    return pl.pallas_call(
        functools.partial(_outproj_router_kernel, tm=tm),
        grid=(n_tiles,),
        in_specs=[row(d), row(oa.shape[1]), row(ob.shape[1]), row(oc.shape[1]),
                  pl.BlockSpec((None,) + wo.shape[1:], lambda i: (layer, 0, 0)), perb,
                  const((1, d)), perb, perb, const(wr.shape), const(br.shape), const(tri.shape),
                  const(upper.shape)],
        out_specs=[row(d), row(d), row(LANE), row(LANE),
                   pl.BlockSpec((1, 8, LANE), lambda i: (i, 0, 0)), const((1, LANE))],
        out_shape=[jax.ShapeDtypeStruct((t, d), F32), jax.ShapeDtypeStruct((t, d), BF16),
                   jax.ShapeDtypeStruct((t, LANE), F32), jax.ShapeDtypeStruct((t, LANE), jnp.int32),
                   jax.ShapeDtypeStruct((n_tiles, 8, LANE), jnp.int32), jax.ShapeDtypeStruct((1, LANE), F32)],
        compiler_params=_cparams(("arbitrary",)),
        name="outproj_norm2_router",
    )(x, oa, ob, oc, wo, g1, nw.reshape(1, d), sc, sh, wr, br, jnp.asarray(tri, BF16), jnp.asarray(upper))


def _run_copies(tab_ref, at_ref, tile, tm, make_copy, act):
    sizes = [s for s in (1 << p for p in range(tm.bit_length() - 1, -1, -1)) if s >= RUN_ALIGN]

    def per_expert(e, carry):
        first = tab_ref[tile, 0, e]
        length = tab_ref[tile, 1, e]
        at = at_ref[tile, e]
        done = jnp.int32(0)
        for size in sizes:
            part = length & size

            @pl.when(part != 0)
            def _():
                act(make_copy(pl.multiple_of(first + done, RUN_ALIGN), pl.multiple_of(at + done, RUN_ALIGN), size))
            done = done + part
        return carry

    lax.fori_loop(0, N_EXPERTS, per_expert, 0)


def _local_rows(tm):
    return TOP_K * tm + N_EXPERTS * RUN_ALIGN


def _dispatch_kernel(tab_ref, at_ref, h_ref, lo_ref, buf_in_ref, buf_ref, st_ref, sem, *, tm, n_tiles):
    del buf_in_ref
    i = pl.program_id(0)
    slot = i % 2

    def copies(tile, s, act):
        _run_copies(tab_ref, at_ref, tile, tm,
                    lambda row, at, size: pltpu.make_async_copy(st_ref.at[s, pl.ds(row, size)],
                                                                buf_ref.at[pl.ds(at, size)], sem.at[s]), act)

    @pl.when(i >= 2)
    def _():
        copies(i - 2, slot, lambda cp: cp.wait())

    col = lax.broadcasted_iota(jnp.int32, (tm, st_ref.shape[1]), 1)
    lo = lo_ref[...]
    place = jnp.zeros(col.shape, F32)
    for kk in range(TOP_K):
        place = jnp.where(col == lo[:, kk:kk + 1], 1.0, place)
    st_ref[slot] = _dot_tn(place.astype(BF16), h_ref[...].astype(BF16))
    copies(i, slot, lambda cp: cp.start())

    @pl.when(i == n_tiles - 1)
    def _():
        if n_tiles >= 2:
            copies(i - 1, 1 - slot, lambda cp: cp.wait())
        copies(i, slot, lambda cp: cp.wait())


def _dispatch_call(h, tab, run_at, lo, buf):
    t, d = h.shape
    n_slots = buf.shape[0]
    tm = min(ROUTE_TILE, t)
    n_tiles = t // tm
    grid_spec = pltpu.PrefetchScalarGridSpec(
        num_scalar_prefetch=2,
        grid=(n_tiles,),
        in_specs=[pl.BlockSpec((tm, d), lambda i, tb, sr: (i, 0)),
                  pl.BlockSpec((tm, LANE), lambda i, tb, sr: (i, 0)),
                  pl.BlockSpec(memory_space=pl.ANY)],
        out_specs=pl.BlockSpec(memory_space=pl.ANY),
        scratch_shapes=[pltpu.VMEM((2, _local_rows(tm), d), F32), pltpu.SemaphoreType.DMA((2,))],
    )
    return pl.pallas_call(
        functools.partial(_dispatch_kernel, tm=tm, n_tiles=n_tiles),
        grid_spec=grid_spec,
        out_shape=jax.ShapeDtypeStruct((n_slots, d), F32),
        input_output_aliases={4: 0},
        compiler_params=_cparams(("arbitrary",)),
        name="moe_dispatch",
    )(tab, run_at, h, lo, buf)


def _expert_kernel(be_ref, nx_ref, used_ref, nu_ref, x_ref, wgu_ref, bgu_ref, wdn_ref, bdn_ref, o_ref,
                   wgu_f32, wdn_f32, wgu_bf, wdn_bf, sem, *, d_ff, layer):
    i = pl.program_id(0)
    live = i < nu_ref[0]
    e = be_ref[i]
    e_next = nx_ref[i]

    def weight_copies(expert):
        return (pltpu.make_async_copy(wgu_ref.at[layer, expert], wgu_f32, sem.at[0]),
                pltpu.make_async_copy(wdn_ref.at[layer, expert], wdn_f32, sem.at[1]))

    @pl.when(i == 0)
    def _():
        for cp in weight_copies(e):
            cp.start()

    @pl.when(jnp.logical_and(live, jnp.logical_or(i == 0, e != be_ref[jnp.maximum(i - 1, 0)])))
    def _():
        for cp in weight_copies(e):
            cp.wait()
        wgu_bf[...] = wgu_f32[...].astype(BF16)
        wdn_bf[...] = wdn_f32[...].astype(BF16)

        @pl.when(e_next != e)
        def _():
            for cp in weight_copies(e_next):
                cp.start()

    def ffn(rows):
        gu = _dot(x_ref[0:rows, :].astype(BF16), wgu_bf[...]) + bgu_ref[0, 0]
        gate = jnp.minimum(gu[:, :d_ff], SWIGLU_LIMIT)
        up = jnp.clip(gu[:, d_ff:], -SWIGLU_LIMIT, SWIGLU_LIMIT)
        act = (up + 1.0) * gate * _sigmoid(SWIGLU_ALPHA * gate)
        o_ref[0:rows, :] = _dot(act.astype(BF16), wdn_bf[...]) + bdn_ref[0, 0]
        if rows < MOE_BLOCK:
            o_ref[rows:, :] = jnp.zeros((MOE_BLOCK - rows, o_ref.shape[1]), o_ref.dtype)

    used = used_ref[i]
    prefixes = [MOE_BLOCK // 4, MOE_BLOCK // 2, MOE_BLOCK]
    for lo_rows, rows in zip([0] + prefixes[:-1], prefixes):
        pl.when(jnp.logical_and(live, jnp.logical_and(used > lo_rows, used <= rows)))(
            functools.partial(ffn, rows))

    @pl.when(jnp.logical_not(live))
    def _():
        o_ref[...] = jnp.zeros_like(o_ref)


def _expert_call(buf, block_e, next_e, block_used, n_used, wgu, bgu, wdn, bdn, layer):
    n_slots, d = buf.shape
    n_blocks = n_slots // MOE_BLOCK
    depth, ne, _, f2 = wgu.shape
    d_ff = f2 // 2
    blk = lambda i, be, nx, us, nu: (jnp.minimum(i, nu[0] - 1), 0)
    exp4 = lambda i, be, nx, us, nu: (layer, be[jnp.minimum(i, nu[0] - 1)], 0, 0)
    grid_spec = pltpu.PrefetchScalarGridSpec(
        num_scalar_prefetch=4,
        grid=(n_blocks,),
        in_specs=[pl.BlockSpec((MOE_BLOCK, d), blk),
                  pl.BlockSpec(memory_space=pl.ANY),
                  pl.BlockSpec((1, 1, 1, f2), exp4),
                  pl.BlockSpec(memory_space=pl.ANY),
                  pl.BlockSpec((1, 1, 1, d), exp4)],
        out_specs=pl.BlockSpec((MOE_BLOCK, d), lambda i, be, nx, us, nu: (i, 0)),
        scratch_shapes=[pltpu.VMEM((d, f2), F32), pltpu.VMEM((d_ff, d), F32),
                        pltpu.VMEM((d, f2), BF16), pltpu.VMEM((d_ff, d), BF16),
                        pltpu.SemaphoreType.DMA((2,))],
    )
    return pl.pallas_call(
        functools.partial(_expert_kernel, d_ff=d_ff, layer=layer),
        grid_spec=grid_spec,
        out_shape=jax.ShapeDtypeStruct((n_slots, d), F32),
        compiler_params=_cparams(("arbitrary",)),
        name="moe_experts",
    )(block_e, next_e, block_used, n_used, buf, wgu, bgu.reshape(depth, ne, 1, f2), wdn, bdn.reshape(depth, ne, 1, d))


def _combine_kernel(tab_ref, at_ref, yb_ref, lo_ref, w_ref, x_ref, g2_ref, o_ref, rs_ref, sem, *, tm):
    i = pl.program_id(0)
    n = pl.num_programs(0)
    rows = rs_ref.shape[1]

    def run_copies(tile, s, act):
        _run_copies(tab_ref, at_ref, tile, tm,
                    lambda row, at, size: pltpu.make_async_copy(yb_ref.at[pl.ds(at, size)],
                                                                rs_ref.at[s, pl.ds(row, size)], sem.at[s]), act)

    @pl.when(i == 0)
    def _():
        rs_ref[...] = jnp.zeros_like(rs_ref)
        run_copies(0, 0, lambda cp: cp.start())

    @pl.when(i + 1 < n)
    def _():
        run_copies(i + 1, (i + 1) % 2, lambda cp: cp.start())

    slot = i % 2
    run_copies(i, slot, lambda cp: cp.wait())
    r = rs_ref[slot].astype(BF16)
    col = lax.broadcasted_iota(jnp.int32, (tm, rows), 1)
    w = w_ref[...]
    lo = lo_ref[...]
    pw = jnp.zeros((tm, rows), F32)
    for kk in range(TOP_K):
        pw = jnp.where(col == lo[:, kk:kk + 1], w[:, kk:kk + 1], pw)
    y = _dot(pw.astype(BF16), r)
    o_ref[...] = x_ref[...] + g2_ref[0] * y


def _combine_call(yb, tab, src, lo, tw, x, g2, seq):
    t, d = x.shape
    tm = min(ROUTE_TILE, seq)
    per_b = seq // tm
    grid_spec = pltpu.PrefetchScalarGridSpec(
        num_scalar_prefetch=2,
        grid=(t // tm,),
        in_specs=[pl.BlockSpec(memory_space=pl.ANY),
                  pl.BlockSpec((tm, LANE), lambda i, tb, sr: (i, 0)),
                  pl.BlockSpec((tm, LANE), lambda i, tb, sr: (i, 0)),
                  pl.BlockSpec((tm, d), lambda i, tb, sr: (i, 0)),
                  pl.BlockSpec((1, 1, d), lambda i, tb, sr: (i // per_b, 0, 0))],
        out_specs=pl.BlockSpec((tm, d), lambda i, tb, sr: (i, 0)),
        scratch_shapes=[pltpu.VMEM((2, _local_rows(tm), d), F32), pltpu.SemaphoreType.DMA((2,))],
    )
    return pl.pallas_call(
        functools.partial(_combine_kernel, tm=tm),
        grid_spec=grid_spec,
        out_shape=jax.ShapeDtypeStruct((t, d), F32),
        compiler_params=_cparams(("arbitrary",)),
        name="moe_combine",
    )(tab, src, yb, lo, tw, x, g2)


def _final_norm_kernel(x_ref, w_ref, o_ref):
    x = x_ref[...]
    o_ref[...] = x * lax.rsqrt(jnp.mean(x * x, axis=-1, keepdims=True) + EPS) * w_ref[...]


def _final_norm_call(x, w):
    t, d = x.shape
    tm = min(512, t)
    return pl.pallas_call(
        _final_norm_kernel,
        grid=(t // tm,),
        in_specs=[pl.BlockSpec((tm, d), lambda i: (i, 0)), pl.BlockSpec((1, d), lambda i: (0, 0))],
        out_specs=pl.BlockSpec((tm, d), lambda i: (i, 0)),
        out_shape=jax.ShapeDtypeStruct((t, d), F32),
        compiler_params=_cparams(("parallel",)),
        name="final_norm",
    )(x, w.reshape(1, d))


def _split_heads(w, hd):
    return w.reshape(w.shape[:-1] + (N_HEADS, hd))


def _pad_last(w, lo, hi):
    return jnp.pad(w, [(0, 0)] * (w.ndim - 1) + [(lo, hi)])


def _tile_cols(w, hd):
    return _pad_last(_split_heads(w, hd), 0, LANE - hd).reshape(w.shape[:-1] + (HEAD_W,))


def _pair_cols(w, hd):
    return _pad_last(_split_heads(w, hd), 0, HALF - hd).reshape(w.shape[:-1] + (PAIR_W,))


def _query_cols(w, hd):
    w = _split_heads(w, hd)
    tiles = [_pad_last(w[..., h, :], (h % 2) * HALF, LANE - (h % 2) * HALF - hd) for h in range(N_HEADS)]
    return jnp.concatenate(tiles, axis=-1)


def _tile_rows(w, hd):
    return jnp.swapaxes(_tile_cols(jnp.swapaxes(w, -1, -2), hd), -1, -2)


IN_PARTS = (("gq", GLA_DK, _query_cols), ("gk", GLA_DK, _pair_cols), ("gv", GLA_DV, _tile_cols),
            ("gg", GLA_DV, _tile_cols), ("ga", None, None),
            ("hq", HG_DK, _query_cols), ("hf", HG_DK, _pair_cols), ("hi", HG_DV, _pair_cols),
            ("hg", HG_DV, _pair_cols),
            ("rq", RET_DK, _query_cols), ("rk", RET_DK, _pair_cols), ("rv", RET_DV, _tile_cols),
            ("rg", RET_DV, _tile_cols))
F32_PARTS = ("ga", "hf")


def _layout_w_in(w_in):
    cols, off = [], 0
    for _, hd, layout in IN_PARTS:
        if layout is None:
            part = _pad_last(w_in[..., off:off + GLA_RANK], 0, LANE - GLA_RANK)
            off += GLA_RANK
        else:
            part = layout(w_in[..., off:off + N_HEADS * hd], hd)
            off += N_HEADS * hd
        cols.append(part)
    return jnp.concatenate(cols, axis=-1).astype(BF16), tuple(int(p.shape[-1]) for p in cols)


def kernel(x, c, positions, w_mod, b_mod, norm1_w, w_in, gla_wa2, gla_ba2, hg_lb, gla_norm_w, hg_norm_w,
           w_out, norm2_w, w_r, b_r, w_gu, b_gu, w_dn, b_dn, final_norm_w):
    nb, seq, d = x.shape
    t = nb * seq
    depth = w_mod.shape[0]
    dtypes = tuple(F32 if name in F32_PARTS else BF16 for name, _, _ in IN_PARTS)

    mod = _mod_call(c, w_mod, b_mod)
    cos, sin = _rope_call(positions)
    lb_all = _lb_call(hg_lb)

    n_assign = t * TOP_K
    n_run_pad = N_EXPERTS * (RUN_ALIGN - 1) * (t // min(ROUTE_TILE, seq))
    n_blocks = (n_assign + n_run_pad + MOE_BLOCK - 1) // MOE_BLOCK + N_EXPERTS
    n_slots = n_blocks * MOE_BLOCK

    w_cat, widths = _layout_w_in(w_in)
    n_gla, n_hg = N_HEADS * GLA_DV, N_HEADS * HG_DV
    wo_p = jnp.concatenate([_tile_rows(w_out[:, :n_gla], GLA_DV), w_out[:, n_gla:n_gla + n_hg],
                            _tile_rows(w_out[:, n_gla + n_hg:], RET_DV)], axis=1).astype(BF16)
    wa2p = jnp.pad(_pair_cols(gla_wa2, GLA_DK), ((0, 0), (0, LANE - GLA_RANK), (0, 0)))
    ba2p = _pair_cols(gla_ba2, GLA_DK)
    gnw = _tile_cols(gla_norm_w, GLA_DV)
    wr_p = _pad_last(w_r, 0, LANE - N_EXPERTS)
    br_p = jnp.pad(b_r, ((0, 0), (0, LANE - N_EXPERTS)), constant_values=NEG_BIG)

    buf = jnp.zeros((n_slots, d), F32)
    xf = x.reshape(t, d)
    for layer in range(depth):
        sh1, sc1, g1, sh2, sc2, g2 = [m.reshape(nb, 1, d) for m in jnp.split(mod[layer], 6, axis=-1)]

        parts = _inproj_call(xf, norm1_w[layer], sc1, sh1, w_cat, layer, widths, dtypes, seq)
        gq, gk, gv, gg, ga, hq, hf, hi, hg, rq, rk, rv, rg = parts
        o_gla = _gla_call(gq, gk, gv, gg, ga, wa2p[layer], ba2p[layer:layer + 1], gnw[layer:layer + 1], nb, seq)
        o_hg = _hg_call(hq, hf, hi, hg, lb_all[layer:layer + 1], hg_norm_w[layer:layer + 1], nb, seq)
        o_ret = _ret_call(rq, rk, rv, rg, cos, sin, nb, seq)
        xf, h2, tw, lo, tab, cnt = _outproj_router_call(
            xf, o_gla, o_hg, o_ret, wo_p, layer, g1, norm2_w[layer], sc2, sh2, wr_p[layer],
            br_p[layer:layer + 1], seq)

        counts = cnt[0, :N_EXPERTS].astype(jnp.int32)
        padded = (counts + MOE_BLOCK - 1) // MOE_BLOCK * MOE_BLOCK
        pends = jnp.cumsum(padded)
        pstarts = pends - padded
        block_start = jnp.arange(n_blocks, dtype=jnp.int32) * MOE_BLOCK
        block_e = jnp.minimum(jnp.sum((pends[None, :] <= block_start[:, None]).astype(jnp.int32), axis=1),
                              N_EXPERTS - 1)
        n_used = (pends[-1:] // MOE_BLOCK).astype(jnp.int32)
        run_at = tab[:, 2, :] + jnp.pad(pstarts, (0, LANE - N_EXPERTS))[None, :]

        ids = jnp.arange(N_EXPERTS, dtype=jnp.int32)
        later = jnp.where((ids[None, :] > ids[:, None]) & (padded[None, :] > 0), ids[None, :], N_EXPERTS)
        follower = jnp.min(later, axis=1)
        follower = jnp.where(follower == N_EXPERTS, ids, follower)
        of_block = block_e[:, None] == ids[None, :]
        next_e = jnp.sum(jnp.where(of_block, follower[None, :], 0), axis=1)
        run_end = jnp.sum(jnp.where(of_block, (pstarts + counts)[None, :], 0), axis=1)
        block_used = jnp.clip(run_end - block_start, 0, MOE_BLOCK).astype(jnp.int32)

        buf = _dispatch_call(h2, tab, run_at, lo, buf)
        yb = _expert_call(buf, block_e, next_e, block_used, n_used, w_gu, b_gu, w_dn, b_dn, layer)
        xf = _combine_call(yb, tab, run_at, lo, tw, xf, g2, seq)

    return _final_norm_call(xf, final_norm_w).reshape(nb, seq, d)
```

```python
import functools

import numpy as np
import jax
import jax.numpy as jnp
from jax import lax
from jax.experimental import pallas as pl
from jax.experimental.pallas import tpu as pltpu

F32 = jnp.float32
BF16 = jnp.bfloat16

N_HEADS = 4
GLA_DK, GLA_DV, GLA_RANK, GLA_TAU = 48, 96, 16, 16.0
HG_DK, HG_DV = 64, 64
RET_DK, RET_DV = 48, 96
ROPE_BASE = 10000.0
N_EXPERTS, TOP_K = 32, 4
SWIGLU_LIMIT, SWIGLU_ALPHA = 7.0, 1.702
MOE_BLOCK = 512
ROUTE_TILE = 512
RUN_ALIGN = 8
EPS = 1e-6

LANE = 128
HALF = LANE // 2
N_PAIRS = N_HEADS // 2
HEAD_W = N_HEADS * LANE
PAIR_W = N_PAIRS * LANE
VMEM_LIMIT = 56 * 1024 * 1024
NEG_BIG = -1e30


def _cparams(sem):
    return pltpu.CompilerParams(dimension_semantics=sem, vmem_limit_bytes=VMEM_LIMIT)


def _dot(a, b):
    return jnp.dot(a, b, preferred_element_type=F32)


def _dot_nt(a, b):
    return lax.dot_general(a, b, (((1,), (1,)), ((), ())), preferred_element_type=F32)


def _dot_tn(a, b):
    return lax.dot_general(a, b, (((0,), (0,)), ((), ())), preferred_element_type=F32)


def _split2(a):
    hi = a.astype(BF16)
    return hi, (a - hi.astype(F32)).astype(BF16)


def _dot_f32(a, b):
    a_hi, a_lo = _split2(a)
    b_hi, b_lo = _split2(b)
    return _dot(a_hi, b_hi) + _dot(a_hi, b_lo) + _dot(a_lo, b_hi)


def _sigmoid(x):
    return 1.0 / (1.0 + jnp.exp(-x))


def _log_sigmoid(x):
    return jnp.minimum(x, 0.0) - jnp.log1p(jnp.exp(-jnp.abs(x)))


def _mod_kernel(c_ref, w_ref, b_ref, o_ref):
    c = c_ref[...]
    o_ref[0] = _dot_f32(c * _sigmoid(c), w_ref[0]) + b_ref[0]


def _mod_call(c, w_mod, b_mod):
    depth, d, d6 = w_mod.shape
    nb = c.shape[0]
    rows = 8
    c_pad = jnp.zeros((rows, d), F32).at[:nb].set(c)
    out = pl.pallas_call(
        _mod_kernel,
        grid=(depth, d6 // d),
        in_specs=[
            pl.BlockSpec((rows, d), lambda l, j: (0, 0)),
            pl.BlockSpec((1, d, d), lambda l, j: (l, 0, j)),
            pl.BlockSpec((1, 1, d), lambda l, j: (l, 0, j)),
        ],
        out_specs=pl.BlockSpec((1, rows, d), lambda l, j: (l, 0, j)),
        out_shape=jax.ShapeDtypeStruct((depth, rows, d6), F32),
        compiler_params=_cparams(("parallel", "parallel")),
        name="mod",
    )(c_pad, w_mod, b_mod.reshape(depth, 1, d6))
    return out[:, :nb]


def _rope_kernel(pos_ref, freq_ref, cos_ref, sin_ref):
    ang = pos_ref[0] * freq_ref[...]
    cos_ref[0] = jnp.cos(ang)
    sin_ref[0] = jnp.sin(ang)


def _rope_call(positions):
    nb, l = positions.shape
    half = RET_DK // 2
    inv = (ROPE_BASE ** (-np.arange(half, dtype=np.float32) / half)).astype(np.float32)
    freq = np.zeros((1, LANE), np.float32)
    for base in (0, HALF):
        freq[0, base:base + half] = inv
        freq[0, base + half:base + 2 * half] = inv
    tl = min(l, 512)
    pos = positions.astype(F32).reshape(nb, l, 1)
    shp = jax.ShapeDtypeStruct((nb, l, LANE), F32)
    return pl.pallas_call(
        _rope_kernel,
        grid=(nb, l // tl),
        in_specs=[pl.BlockSpec((1, tl, 1), lambda b, i: (b, i, 0)),
                  pl.BlockSpec((1, LANE), lambda b, i: (0, 0))],
        out_specs=[pl.BlockSpec((1, tl, LANE), lambda b, i: (b, i, 0))] * 2,
        out_shape=[shp, shp],
        compiler_params=_cparams(("parallel", "parallel")),
        name="rope_tables",
    )(pos, jnp.asarray(freq))


def _lb_kernel(p_ref, o_ref):
    p = p_ref[...]
    depth = p.shape[0]
    m = jnp.max(p, axis=0, keepdims=True)
    e = jnp.exp(p - m)
    sm = e / jnp.sum(e, axis=0, keepdims=True)
    acc = jnp.zeros_like(sm[0:1])
    for i in range(depth):
        if i > 0:
            acc = acc + sm[i:i + 1]
        o_ref[i:i + 1, :] = acc


def _lb_call(hg_lb):
    return pl.pallas_call(
        _lb_kernel,
        out_shape=jax.ShapeDtypeStruct(hg_lb.shape, F32),
        name="hg_lower_bounds",
    )(hg_lb.astype(F32))


def _inproj_kernel(x_ref, nw_ref, sc_ref, sh_ref, w_ref, *out_refs, widths):
    x = x_ref[...]
    y = x * lax.rsqrt(jnp.mean(x * x, axis=-1, keepdims=True) + EPS) * nw_ref[...]
    hb = (y * (1.0 + sc_ref[0]) + sh_ref[0]).astype(BF16)
    off = 0
    for o_ref, wd in zip(out_refs, widths):
        o_ref[...] = _dot(hb, w_ref[:, off:off + wd]).astype(o_ref.dtype)
        off += wd


def _inproj_call(x, norm_w, sc, sh, w_cat, layer, widths, dtypes, seq):
    t, d = x.shape
    tm = min(512, seq)
    per_b = seq // tm
    return pl.pallas_call(
        functools.partial(_inproj_kernel, widths=widths),
        grid=(t // tm,),
        in_specs=[
            pl.BlockSpec((tm, d), lambda i: (i, 0)),
            pl.BlockSpec((1, d), lambda i: (0, 0)),
            pl.BlockSpec((1, 1, d), lambda i: (i // per_b, 0, 0)),
            pl.BlockSpec((1, 1, d), lambda i: (i // per_b, 0, 0)),
            pl.BlockSpec((None,) + w_cat.shape[1:], lambda i: (layer, 0, 0)),
        ],
        out_specs=[pl.BlockSpec((tm, wd), lambda i: (i, 0)) for wd in widths],
        out_shape=[jax.ShapeDtypeStruct((t, wd), dt) for wd, dt in zip(widths, dtypes)],
        compiler_params=_cparams(("parallel",)),
        name="norm1_inproj",
    )(x, norm_w.reshape(1, d), sc, sh, w_cat)


def _level_tables(c):
    nl = int(np.log2(c))
    assert 1 << nl == c
    idx = np.arange(c)
    mats, masks = [], [np.eye(c, dtype=np.float32)]
    t = idx[None, :]
    i = idx[:, None]
    for lvl in range(nl):
        h = c >> (lvl + 1)
        blk, pos = idx // (2 * h), idx % (2 * h)
        m = (blk * 2 * h + h - 1)[:, None]
        right = (pos >= h)[:, None]
        a = np.where(right, (t > m) & (t <= i), (t > i) & (t <= m))
        mats.append(a.astype(np.float32))
        same = blk[:, None] == blk[None, :]
        masks.append((same & right & (pos < h)[None, :]).astype(np.float32))
    mats.append((t <= i).astype(np.float32))
    mats.append((t > i).astype(np.float32))
    return np.concatenate(mats, 0), np.stack(masks, 0), nl


def _pair_decays(g, mall_ref):
    gs = _dot(mall_ref[...], jnp.concatenate(_split2(g), axis=1))
    return jnp.exp(gs[:, :LANE] + gs[:, LANE:])


def _pair_keys(k, e, c, nl):
    levels = [(k * e[lvl * c:(lvl + 1) * c]).astype(BF16) for lvl in range(nl)]
    return k.astype(BF16), levels, (k * e[(nl + 1) * c:(nl + 2) * c]).astype(BF16)


def _gated_head(q, kb, klv, e, v, st, masks_ref, c, nl):
    scores = masks_ref[0] * _dot_nt(q.astype(BF16), kb)
    for lvl in range(nl):
        scores = scores + masks_ref[lvl + 1] * _dot_nt((q * e[lvl * c:(lvl + 1) * c]).astype(BF16), klv[lvl])
    q_in = (q * e[nl * c:(nl + 1) * c]).astype(BF16)
    return _dot(scores.astype(BF16), v) + _dot_nt(q_in, st.astype(BF16))


def _gla_kernel(q_ref, k_ref, v_ref, gg_ref, ga_ref, wa2_ref, ba2_ref, nw_ref, mall_ref, masks_ref,
                o_ref, st_ref, *, c, nl):
    @pl.when(pl.program_id(0) == 0)
    def _():
        st_ref[...] = jnp.zeros_like(st_ref)

    for b in range(q_ref.shape[0]):
        log_alpha = _log_sigmoid(_dot_f32(ga_ref[b], wa2_ref[...]) + ba2_ref[...]) * (1.0 / GLA_TAU)
        for pr in range(N_PAIRS):
            psl = slice(pr * LANE, (pr + 1) * LANE)
            e = _pair_decays(log_alpha[:, psl], mall_ref)
            kb, klv, k_end = _pair_keys(k_ref[b, :, psl].astype(F32), e, c, nl)
            dec = e[(nl + 1) * c - 1:(nl + 1) * c, :]
            for hd in (2 * pr, 2 * pr + 1):
                sl = slice(hd * LANE, (hd + 1) * LANE)
                q = q_ref[b, :, sl].astype(F32) * (GLA_DK ** -0.5)
                v = v_ref[b, :, sl]
                st = st_ref[b, hd]
                o = _gated_head(q, kb, klv, e, v, st, masks_ref, c, nl)
                st_ref[b, hd] = dec * st + _dot_tn(v, k_end)
                y = o * lax.rsqrt(jnp.sum(o * o, axis=-1, keepdims=True) * (1.0 / GLA_DV) + EPS) * nw_ref[:, sl]
                gate = gg_ref[b, :, sl].astype(F32)
                o_ref[b, :, sl] = (y * gate * _sigmoid(gate)).astype(o_ref.dtype)


def _hg_kernel(q_ref, f_ref, v_ref, og_ref, lb_ref, nw_ref, mall_ref, masks_ref,
               o_ref, st_ref, *, c, nl):
    @pl.when(pl.program_id(0) == 0)
    def _():
        st_ref[...] = jnp.zeros_like(st_ref)

    low = lax.broadcasted_iota(jnp.int32, (c, LANE), 1) < HALF
    for b in range(q_ref.shape[0]):
        for pr in range(N_PAIRS):
            psl = slice(pr * LANE, (pr + 1) * LANE)
            lb = lb_ref[:, psl]
            hf = f_ref[b, :, psl]
            la = jnp.log(lb)
            lc = jnp.log1p(-lb) + _log_sigmoid(hf)
            log_f = jnp.maximum(la, lc) + jnp.log1p(jnp.exp(-jnp.abs(la - lc)))
            e = _pair_decays(log_f, mall_ref)
            kb, klv, k_end = _pair_keys((1.0 - lb) * _sigmoid(-hf), e, c, nl)
            v = v_ref[b, :, psl]
            st = st_ref[b, pr]
            o_even = _gated_head(q_ref[b, :, (2 * pr) * LANE:(2 * pr + 1) * LANE].astype(F32),
                                 kb, klv, e, v, st, masks_ref, c, nl)
            o_odd = _gated_head(q_ref[b, :, (2 * pr + 1) * LANE:(2 * pr + 2) * LANE].astype(F32),
                                kb, klv, e, v, st, masks_ref, c, nl)
            o = jnp.where(low, o_even, o_odd)
            st_ref[b, pr] = e[(nl + 1) * c - 1:(nl + 1) * c, :] * st + _dot_tn(v, k_end)
            sq = o * o
            ms = jnp.where(low, jnp.sum(jnp.where(low, sq, 0.0), axis=-1, keepdims=True),
                           jnp.sum(jnp.where(low, 0.0, sq), axis=-1, keepdims=True)) * (1.0 / HG_DV)
            y = o * lax.rsqrt(ms + EPS) * nw_ref[:, psl]
            o_ref[b, :, psl] = (y * _sigmoid(og_ref[b, :, psl].astype(F32))).astype(o_ref.dtype)


def _tok_spec(nb, c, width):
    return pl.BlockSpec((nb, c, width), lambda i: (0, i, 0))


def _const_spec(shape):
    nd = len(shape)
    return pl.BlockSpec(shape, lambda i: (0,) * nd)


def _seq_view(a, nb, seq):
    return a.reshape(nb, seq, a.shape[-1])


def _gla_call(gq, gk, gv, gg, ga, wa2p, ba2p, nwp, nb, seq):
    c = min(128, seq)
    n = seq // c
    mall, masks, nl = _level_tables(c)
    sv = lambda a: _seq_view(a, nb, seq)
    return pl.pallas_call(
        functools.partial(_gla_kernel, c=c, nl=nl),
        grid=(n,),
        in_specs=[_tok_spec(nb, c, HEAD_W), _tok_spec(nb, c, PAIR_W), _tok_spec(nb, c, HEAD_W),
                  _tok_spec(nb, c, HEAD_W), _tok_spec(nb, c, LANE),
                  _const_spec(wa2p.shape), _const_spec(ba2p.shape), _const_spec(nwp.shape),
                  _const_spec(mall.shape), _const_spec(masks.shape)],
        out_specs=_tok_spec(nb, c, HEAD_W),
        out_shape=jax.ShapeDtypeStruct((nb, seq, HEAD_W), BF16),
        scratch_shapes=[pltpu.VMEM((nb, N_HEADS, LANE, LANE), F32)],
        compiler_params=_cparams(("arbitrary",)),
        name="gla_recurrence",
    )(sv(gq), sv(gk), sv(gv), sv(gg), sv(ga), wa2p, ba2p, nwp,
      jnp.asarray(mall, BF16), jnp.asarray(masks)).reshape(nb * seq, HEAD_W)


def _hg_call(hq, hf, hi, hg, lbp, nwp, nb, seq):
    c = min(128, seq)
    n = seq // c
    mall, masks, nl = _level_tables(c)
    sv = lambda a: _seq_view(a, nb, seq)
    return pl.pallas_call(
        functools.partial(_hg_kernel, c=c, nl=nl),
        grid=(n,),
        in_specs=[_tok_spec(nb, c, HEAD_W), _tok_spec(nb, c, PAIR_W), _tok_spec(nb, c, PAIR_W),
                  _tok_spec(nb, c, PAIR_W),
                  _const_spec(lbp.shape), _const_spec(nwp.shape),
                  _const_spec(mall.shape), _const_spec(masks.shape)],
        out_specs=_tok_spec(nb, c, PAIR_W),
        out_shape=jax.ShapeDtypeStruct((nb, seq, PAIR_W), BF16),
        scratch_shapes=[pltpu.VMEM((nb, N_PAIRS, LANE, LANE), F32)],
        compiler_params=_cparams(("arbitrary",)),
        name="hgrn2_recurrence",
    )(sv(hq), sv(hf), sv(hi), sv(hg), lbp, nwp,
      jnp.asarray(mall, BF16), jnp.asarray(masks)).reshape(nb * seq, PAIR_W)


def _ret_tables(c):
    hs = np.arange(N_HEADS, dtype=np.float64)
    log_gamma = np.log(1.0 - np.exp2(-5.0 - hs))
    idx = np.arange(c, dtype=np.float64)
    rel = idx[:, None] - idx[None, :]
    dmat = np.where(rel >= 0, np.exp(log_gamma[:, None, None] * np.maximum(rel, 0.0)), 0.0)
    qdec = np.exp(log_gamma[:, None] * (idx + 1.0))
    kdec = np.exp(log_gamma[:, None] * (c - 1.0 - idx))
    cdec = np.exp(log_gamma * c)
    qfull = np.broadcast_to(qdec[:, :, None], (N_HEADS, c, LANE)).astype(np.float32)
    kpair = np.zeros((N_PAIRS, c, LANE), np.float32)
    for hd in range(N_HEADS):
        kpair[hd // 2, :, (hd % 2) * HALF:(hd % 2 + 1) * HALF] = kdec[hd][:, None]
    return dmat.astype(np.float32), qfull, kpair, [float(np.float32(v)) for v in cdec]


def _ret_kernel(q_ref, k_ref, v_ref, og_ref, cos_ref, sin_ref, dmat_ref, qdec_ref, kdec_ref,
                o_ref, st_ref, *, c, cdec):
    @pl.when(pl.program_id(0) == 0)
    def _():
        st_ref[...] = jnp.zeros_like(st_ref)

    half = RET_DK // 2
    lane = lax.broadcasted_iota(jnp.int32, (c, LANE), 1)
    first = (lane & (HALF - 1)) < half

    for b in range(q_ref.shape[0]):
        cos = cos_ref[b]
        sin = sin_ref[b]

        def rotary(t):
            rot = jnp.where(first, -pltpu.roll(t, LANE - half, 1), pltpu.roll(t, half, 1))
            return t * cos + rot * sin

        for pr in range(N_PAIRS):
            psl = slice(pr * LANE, (pr + 1) * LANE)
            k = rotary(k_ref[b, :, psl].astype(F32))
            kb = k.astype(BF16)
            k_end = (k * kdec_ref[pr]).astype(BF16)
            for hd in (2 * pr, 2 * pr + 1):
                sl = slice(hd * LANE, (hd + 1) * LANE)
                q = rotary(q_ref[b, :, sl].astype(F32)) * (RET_DK ** -0.5)
                v = v_ref[b, :, sl]
                st = st_ref[b, hd]
                scores = _dot_nt(q.astype(BF16), kb) * dmat_ref[hd]
                o = _dot(scores.astype(BF16), v) + _dot_nt((q * qdec_ref[hd]).astype(BF16), st.astype(BF16))
                st_ref[b, hd] = cdec[hd] * st + _dot_tn(v, k_end)
                mu = jnp.sum(o, axis=-1, keepdims=True) * (1.0 / RET_DV)
                dlt = jnp.where(lane < RET_DV, o - mu, 0.0)
                var = jnp.sum(dlt * dlt, axis=-1, keepdims=True) * (1.0 / RET_DV)
                gate = og_ref[b, :, sl].astype(F32)
                o_ref[b, :, sl] = (dlt * lax.rsqrt(var + EPS) * gate * _sigmoid(gate)).astype(o_ref.dtype)


def _ret_call(rq, rk, rv, rg, cos, sin, nb, seq):
    c = min(128, seq)
    n = seq // c
    dmat, qdec, kdec, cdec = _ret_tables(c)
    sv = lambda a: _seq_view(a, nb, seq)
    return pl.pallas_call(
        functools.partial(_ret_kernel, c=c, cdec=cdec),
        grid=(n,),
        in_specs=[_tok_spec(nb, c, HEAD_W), _tok_spec(nb, c, PAIR_W), _tok_spec(nb, c, HEAD_W),
                  _tok_spec(nb, c, HEAD_W), _tok_spec(nb, c, LANE), _tok_spec(nb, c, LANE),
                  _const_spec(dmat.shape), _const_spec(qdec.shape), _const_spec(kdec.shape)],
        out_specs=_tok_spec(nb, c, HEAD_W),
        out_shape=jax.ShapeDtypeStruct((nb, seq, HEAD_W), BF16),
        scratch_shapes=[pltpu.VMEM((nb, N_HEADS, LANE, LANE), F32)],
        compiler_params=_cparams(("arbitrary",)),
        name="retention",
    )(sv(rq), sv(rk), sv(rv), sv(rg), cos, sin,
      jnp.asarray(dmat), jnp.asarray(qdec), jnp.asarray(kdec)).reshape(nb * seq, HEAD_W)


def _outproj_router_kernel(x_ref, oa_ref, ob_ref, oc_ref, wo_ref, g1_ref, nw_ref, sc_ref, sh_ref,
                           wr_ref, br_ref, tri_ref, upper_ref,
                           xo_ref, h_ref, tw_ref, lo_ref, tab_ref, cnt_ref, *, tm):
    @pl.when(pl.program_id(0) == 0)
    def _():
        cnt_ref[...] = jnp.zeros_like(cnt_ref)

    wa, wb = oa_ref.shape[1], ob_ref.shape[1]
    mix = (_dot(oa_ref[...], wo_ref[0:wa, :]) + _dot(ob_ref[...], wo_ref[wa:wa + wb, :])
           + _dot(oc_ref[...], wo_ref[wa + wb:, :]))
    x = x_ref[...] + g1_ref[0] * mix
    xo_ref[...] = x
    y = x * lax.rsqrt(jnp.mean(x * x, axis=-1, keepdims=True) + EPS) * nw_ref[...]
    h = y * (1.0 + sc_ref[0]) + sh_ref[0]
    h_ref[...] = h.astype(h_ref.dtype)

    lg = _dot_f32(h, wr_ref[...]) + br_ref[...]
    lane = lax.broadcasted_iota(jnp.int32, (tm, LANE), 1)
    sel_e, sel_v = [], []
    for _ in range(TOP_K):
        m = jnp.max(lg, axis=-1, keepdims=True)
        idx = jnp.min(jnp.where(lg == m, lane, LANE), axis=-1, keepdims=True)
        sel_e.append(idx)
        sel_v.append(m)
        lg = jnp.where(lane == idx, -jnp.inf, lg)
    ex = [jnp.exp(v - sel_v[0]) for v in sel_v]
    den = ex[0] + ex[1] + ex[2] + ex[3]
    hot = [(lane == idx) for idx in sel_e]
    onehot = jnp.zeros((tm, LANE), F32)
    for hk in hot:
        onehot = onehot + jnp.where(hk, 1.0, 0.0)
    in_tile = _dot(tri_ref[...], onehot.astype(BF16))
    earlier = cnt_ref[...]
    tile_cnt = jnp.floor((jnp.sum(onehot, axis=0, keepdims=True) + (RUN_ALIGN - 1.0)) * (1.0 / RUN_ALIGN)) * RUN_ALIGN
    tile_start = _dot_f32(jnp.broadcast_to(tile_cnt, (8, LANE)), upper_ref[...])[0:1]
    local = tile_start + in_tile
    tw = jnp.zeros((tm, LANE), F32)
    lo = jnp.zeros((tm, LANE), jnp.int32)
    for kk in range(TOP_K):
        lrow = jnp.sum(jnp.where(hot[kk], local, 0.0), axis=-1, keepdims=True).astype(jnp.int32)
        tw = jnp.where(lane == kk, ex[kk] / den, tw)
        lo = jnp.where(lane == kk, lrow, lo)
    tw_ref[...] = tw
    lo_ref[...] = lo
    tab_ref[0, 0:1, :] = tile_start.astype(jnp.int32)
    tab_ref[0, 1:2, :] = tile_cnt.astype(jnp.int32)
    tab_ref[0, 2:3, :] = earlier.astype(jnp.int32)
    tab_ref[0, 3:4, :] = jnp.broadcast_to(jnp.sum(tile_cnt, axis=-1, keepdims=True), (1, LANE)).astype(jnp.int32)
    tab_ref[0, 4:8, :] = jnp.zeros((4, LANE), jnp.int32)
    cnt_ref[...] = earlier + tile_cnt


def _outproj_router_call(x, oa, ob, oc, wo, layer, g1, nw, sc, sh, wr, br, seq):
    t, d = x.shape
    tm = min(ROUTE_TILE, seq)
    per_b = seq // tm
    tri = np.tril(np.ones((tm, tm), np.float32), -1)
    upper = np.triu(np.ones((LANE, LANE), np.float32), 1)
    row = lambda w: pl.BlockSpec((tm, w), lambda i: (i, 0))
    const = lambda shape: pl.BlockSpec(shape, lambda i: (0,) * len(shape))
    perb = pl.BlockSpec((1, 1, d), lambda i: (i // per_b, 0, 0))
    n_tiles = t // tm
    return pl.pallas_call(
        functools.partial(_outproj_router_kernel, tm=tm),
        grid=(n_tiles,),
        in_specs=[row(d), row(oa.shape[1]), row(ob.shape[1]), row(oc.shape[1]),
                  pl.BlockSpec((None,) + wo.shape[1:], lambda i: (layer, 0, 0)), perb,
                  const((1, d)), perb, perb, const(wr.shape), const(br.shape), const(tri.shape),
                  const(upper.shape)],
        out_specs=[row(d), row(d), row(LANE), row(LANE),
                   pl.BlockSpec((1, 8, LANE), lambda i: (i, 0, 0)), const((1, LANE))],
        out_shape=[jax.ShapeDtypeStruct((t, d), F32), jax.ShapeDtypeStruct((t, d), BF16),
                   jax.ShapeDtypeStruct((t, LANE), F32), jax.ShapeDtypeStruct((t, LANE), jnp.int32),
                   jax.ShapeDtypeStruct((n_tiles, 8, LANE), jnp.int32), jax.ShapeDtypeStruct((1, LANE), F32)],
        compiler_params=_cparams(("arbitrary",)),
        name="outproj_norm2_router",
    )(x, oa, ob, oc, wo, g1, nw.reshape(1, d), sc, sh, wr, br, jnp.asarray(tri, BF16), jnp.asarray(upper))


def _run_copies(tab_ref, at_ref, tile, tm, make_copy, act):
    sizes = [s for s in (1 << p for p in range(tm.bit_length() - 1, -1, -1)) if s >= RUN_ALIGN]

    def per_expert(e, carry):
        first = tab_ref[tile, 0, e]
        length = tab_ref[tile, 1, e]
        at = at_ref[tile, e]
        done = jnp.int32(0)
        for size in sizes:
            part = length & size

            @pl.when(part != 0)
            def _():
                act(make_copy(pl.multiple_of(first + done, RUN_ALIGN), pl.multiple_of(at + done, RUN_ALIGN), size))
            done = done + part
        return carry

    lax.fori_loop(0, N_EXPERTS, per_expert, 0)


def _local_rows(tm):
    return TOP_K * tm + N_EXPERTS * RUN_ALIGN


def _dispatch_kernel(tab_ref, at_ref, h_ref, lo_ref, buf_in_ref, buf_ref, st_ref, sem, *, tm, n_tiles):
    del buf_in_ref
    i = pl.program_id(0)
    slot = i % 2

    def copies(tile, s, act):
        _run_copies(tab_ref, at_ref, tile, tm,
                    lambda row, at, size: pltpu.make_async_copy(st_ref.at[s, pl.ds(row, size)],
                                                                buf_ref.at[pl.ds(at, size)], sem.at[s]), act)

    def wait_tile(tile, s):
        total = pl.multiple_of(tab_ref[tile, 3, 0], RUN_ALIGN)
        pltpu.make_async_copy(st_ref.at[s, pl.ds(0, total)], buf_ref.at[pl.ds(0, total)], sem.at[s]).wait()

    @pl.when(i >= 2)
    def _():
        wait_tile(i - 2, slot)

    col = lax.broadcasted_iota(jnp.int32, (tm, st_ref.shape[1]), 1)
    lo = lo_ref[...]
    place = jnp.zeros(col.shape, F32)
    for kk in range(TOP_K):
        place = jnp.where(col == lo[:, kk:kk + 1], 1.0, place)
    st_ref[slot] = _dot_tn(place.astype(BF16), h_ref[...].astype(BF16))
    copies(i, slot, lambda cp: cp.start())

    @pl.when(i == n_tiles - 1)
    def _():
        if n_tiles >= 2:
            wait_tile(i - 1, 1 - slot)
        wait_tile(i, slot)


def _dispatch_call(h, tab, run_at, lo, buf):
    t, d = h.shape
    n_slots = buf.shape[0]
    tm = min(ROUTE_TILE, t)
    n_tiles = t // tm
    grid_spec = pltpu.PrefetchScalarGridSpec(
        num_scalar_prefetch=2,
        grid=(n_tiles,),
        in_specs=[pl.BlockSpec((tm, d), lambda i, tb, sr: (i, 0)),
                  pl.BlockSpec((tm, LANE), lambda i, tb, sr: (i, 0)),
                  pl.BlockSpec(memory_space=pl.ANY)],
        out_specs=pl.BlockSpec(memory_space=pl.ANY),
        scratch_shapes=[pltpu.VMEM((2, _local_rows(tm), d), F32), pltpu.SemaphoreType.DMA((2,))],
    )
    return pl.pallas_call(
        functools.partial(_dispatch_kernel, tm=tm, n_tiles=n_tiles),
        grid_spec=grid_spec,
        out_shape=jax.ShapeDtypeStruct((n_slots, d), F32),
        input_output_aliases={4: 0},
        compiler_params=_cparams(("arbitrary",)),
        name="moe_dispatch",
    )(tab, run_at, h, lo, buf)


def _expert_kernel(be_ref, nx_ref, used_ref, nu_ref, x_ref, wgu_ref, bgu_ref, wdn_ref, bdn_ref, o_ref,
                   wgu_f32, wdn_f32, wgu_bf, wdn_bf, sem, *, d_ff, layer):
    i = pl.program_id(0)
    live = i < nu_ref[0]
    e = be_ref[i]
    e_next = nx_ref[i]

    def weight_copies(expert):
        return (pltpu.make_async_copy(wgu_ref.at[layer, expert], wgu_f32, sem.at[0]),
                pltpu.make_async_copy(wdn_ref.at[layer, expert], wdn_f32, sem.at[1]))

    @pl.when(i == 0)
    def _():
        for cp in weight_copies(e):
            cp.start()

    @pl.when(jnp.logical_and(live, jnp.logical_or(i == 0, e != be_ref[jnp.maximum(i - 1, 0)])))
    def _():
        for cp in weight_copies(e):
            cp.wait()
        wgu_bf[...] = wgu_f32[...].astype(BF16)
        wdn_bf[...] = wdn_f32[...].astype(BF16)

        @pl.when(e_next != e)
        def _():
            for cp in weight_copies(e_next):
                cp.start()

    def ffn(rows):
        gu = _dot(x_ref[0:rows, :].astype(BF16), wgu_bf[...]) + bgu_ref[0, 0]
        gate = jnp.minimum(gu[:, :d_ff], SWIGLU_LIMIT)
        up = jnp.clip(gu[:, d_ff:], -SWIGLU_LIMIT, SWIGLU_LIMIT)
        act = (up + 1.0) * gate * _sigmoid(SWIGLU_ALPHA * gate)
        o_ref[0:rows, :] = _dot(act.astype(BF16), wdn_bf[...]) + bdn_ref[0, 0]
        if rows < MOE_BLOCK:
            o_ref[rows:, :] = jnp.zeros((MOE_BLOCK - rows, o_ref.shape[1]), o_ref.dtype)

    used = used_ref[i]
    prefixes = [MOE_BLOCK // 4, MOE_BLOCK // 2, MOE_BLOCK]
    for lo_rows, rows in zip([0] + prefixes[:-1], prefixes):
        pl.when(jnp.logical_and(live, jnp.logical_and(used > lo_rows, used <= rows)))(
            functools.partial(ffn, rows))

    @pl.when(jnp.logical_not(live))
    def _():
        o_ref[...] = jnp.zeros_like(o_ref)


def _expert_call(buf, block_e, next_e, block_used, n_used, wgu, bgu, wdn, bdn, layer):
    n_slots, d = buf.shape
    n_blocks = n_slots // MOE_BLOCK
    depth, ne, _, f2 = wgu.shape
    d_ff = f2 // 2
    blk = lambda i, be, nx, us, nu: (jnp.minimum(i, nu[0] - 1), 0)
    exp4 = lambda i, be, nx, us, nu: (layer, be[jnp.minimum(i, nu[0] - 1)], 0, 0)
    grid_spec = pltpu.PrefetchScalarGridSpec(
        num_scalar_prefetch=4,
        grid=(n_blocks,),
        in_specs=[pl.BlockSpec((MOE_BLOCK, d), blk),
                  pl.BlockSpec(memory_space=pl.ANY),
                  pl.BlockSpec((1, 1, 1, f2), exp4),
                  pl.BlockSpec(memory_space=pl.ANY),
                  pl.BlockSpec((1, 1, 1, d), exp4)],
        out_specs=pl.BlockSpec((MOE_BLOCK, d), lambda i, be, nx, us, nu: (i, 0)),
        scratch_shapes=[pltpu.VMEM((d, f2), F32), pltpu.VMEM((d_ff, d), F32),
                        pltpu.VMEM((d, f2), BF16), pltpu.VMEM((d_ff, d), BF16),
                        pltpu.SemaphoreType.DMA((2,))],
    )
    return pl.pallas_call(
        functools.partial(_expert_kernel, d_ff=d_ff, layer=layer),
        grid_spec=grid_spec,
        out_shape=jax.ShapeDtypeStruct((n_slots, d), F32),
        compiler_params=_cparams(("arbitrary",)),
        name="moe_experts",
    )(block_e, next_e, block_used, n_used, buf, wgu, bgu.reshape(depth, ne, 1, f2), wdn, bdn.reshape(depth, ne, 1, d))


def _combine_kernel(tab_ref, at_ref, yb_ref, lo_ref, w_ref, x_ref, g2_ref, o_ref, rs_ref, sem, *, tm):
    i = pl.program_id(0)
    n = pl.num_programs(0)
    rows = rs_ref.shape[1]

    def run_copies(tile, s, act):
        _run_copies(tab_ref, at_ref, tile, tm,
                    lambda row, at, size: pltpu.make_async_copy(yb_ref.at[pl.ds(at, size)],
                                                                rs_ref.at[s, pl.ds(row, size)], sem.at[s]), act)

    @pl.when(i == 0)
    def _():
        rs_ref[...] = jnp.zeros_like(rs_ref)
        run_copies(0, 0, lambda cp: cp.start())

    @pl.when(i + 1 < n)
    def _():
        run_copies(i + 1, (i + 1) % 2, lambda cp: cp.start())

    slot = i % 2
    total = pl.multiple_of(tab_ref[i, 3, 0], RUN_ALIGN)
    pltpu.make_async_copy(yb_ref.at[pl.ds(0, total)], rs_ref.at[slot, pl.ds(0, total)], sem.at[slot]).wait()
    r = rs_ref[slot].astype(BF16)
    col = lax.broadcasted_iota(jnp.int32, (tm, rows), 1)
    w = w_ref[...]
    lo = lo_ref[...]
    pw = jnp.zeros((tm, rows), F32)
    for kk in range(TOP_K):
        pw = jnp.where(col == lo[:, kk:kk + 1], w[:, kk:kk + 1], pw)
    y = _dot(pw.astype(BF16), r)
    o_ref[...] = x_ref[...] + g2_ref[0] * y


def _combine_call(yb, tab, src, lo, tw, x, g2, seq):
    t, d = x.shape
    tm = min(ROUTE_TILE, seq)
    per_b = seq // tm
    grid_spec = pltpu.PrefetchScalarGridSpec(
        num_scalar_prefetch=2,
        grid=(t // tm,),
        in_specs=[pl.BlockSpec(memory_space=pl.ANY),
                  pl.BlockSpec((tm, LANE), lambda i, tb, sr: (i, 0)),
                  pl.BlockSpec((tm, LANE), lambda i, tb, sr: (i, 0)),
                  pl.BlockSpec((tm, d), lambda i, tb, sr: (i, 0)),
                  pl.BlockSpec((1, 1, d), lambda i, tb, sr: (i // per_b, 0, 0))],
        out_specs=pl.BlockSpec((tm, d), lambda i, tb, sr: (i, 0)),
        scratch_shapes=[pltpu.VMEM((2, _local_rows(tm), d), F32), pltpu.SemaphoreType.DMA((2,))],
    )
    return pl.pallas_call(
        functools.partial(_combine_kernel, tm=tm),
        grid_spec=grid_spec,
        out_shape=jax.ShapeDtypeStruct((t, d), F32),
        compiler_params=_cparams(("arbitrary",)),
        name="moe_combine",
    )(tab, src, yb, lo, tw, x, g2)


def _final_norm_kernel(x_ref, w_ref, o_ref):
    x = x_ref[...]
    o_ref[...] = x * lax.rsqrt(jnp.mean(x * x, axis=-1, keepdims=True) + EPS) * w_ref[...]


def _final_norm_call(x, w):
    t, d = x.shape
    tm = min(512, t)
    return pl.pallas_call(
        _final_norm_kernel,
        grid=(t // tm,),
        in_specs=[pl.BlockSpec((tm, d), lambda i: (i, 0)), pl.BlockSpec((1, d), lambda i: (0, 0))],
        out_specs=pl.BlockSpec((tm, d), lambda i: (i, 0)),
        out_shape=jax.ShapeDtypeStruct((t, d), F32),
        compiler_params=_cparams(("parallel",)),
        name="final_norm",
    )(x, w.reshape(1, d))


def _split_heads(w, hd):
    return w.reshape(w.shape[:-1] + (N_HEADS, hd))


def _pad_last(w, lo, hi):
    return jnp.pad(w, [(0, 0)] * (w.ndim - 1) + [(lo, hi)])


def _tile_cols(w, hd):
    return _pad_last(_split_heads(w, hd), 0, LANE - hd).reshape(w.shape[:-1] + (HEAD_W,))


def _pair_cols(w, hd):
    return _pad_last(_split_heads(w, hd), 0, HALF - hd).reshape(w.shape[:-1] + (PAIR_W,))


def _query_cols(w, hd):
    w = _split_heads(w, hd)
    tiles = [_pad_last(w[..., h, :], (h % 2) * HALF, LANE - (h % 2) * HALF - hd) for h in range(N_HEADS)]
    return jnp.concatenate(tiles, axis=-1)


def _tile_rows(w, hd):
    return jnp.swapaxes(_tile_cols(jnp.swapaxes(w, -1, -2), hd), -1, -2)


IN_PARTS = (("gq", GLA_DK, _query_cols), ("gk", GLA_DK, _pair_cols), ("gv", GLA_DV, _tile_cols),
            ("gg", GLA_DV, _tile_cols), ("ga", None, None),
            ("hq", HG_DK, _query_cols), ("hf", HG_DK, _pair_cols), ("hi", HG_DV, _pair_cols),
            ("hg", HG_DV, _pair_cols),
            ("rq", RET_DK, _query_cols), ("rk", RET_DK, _pair_cols), ("rv", RET_DV, _tile_cols),
            ("rg", RET_DV, _tile_cols))
F32_PARTS = ("ga", "hf")


def _layout_w_in(w_in):
    cols, off = [], 0
    for _, hd, layout in IN_PARTS:
        if layout is None:
            part = _pad_last(w_in[..., off:off + GLA_RANK], 0, LANE - GLA_RANK)
            off += GLA_RANK
        else:
            part = layout(w_in[..., off:off + N_HEADS * hd], hd)
            off += N_HEADS * hd
        cols.append(part)
    return jnp.concatenate(cols, axis=-1).astype(BF16), tuple(int(p.shape[-1]) for p in cols)


def kernel(x, c, positions, w_mod, b_mod, norm1_w, w_in, gla_wa2, gla_ba2, hg_lb, gla_norm_w, hg_norm_w,
           w_out, norm2_w, w_r, b_r, w_gu, b_gu, w_dn, b_dn, final_norm_w):
    nb, seq, d = x.shape
    t = nb * seq
    depth = w_mod.shape[0]
    dtypes = tuple(F32 if name in F32_PARTS else BF16 for name, _, _ in IN_PARTS)

    mod = _mod_call(c, w_mod, b_mod)
    cos, sin = _rope_call(positions)
    lb_all = _lb_call(hg_lb)

    n_assign = t * TOP_K
    n_run_pad = N_EXPERTS * (RUN_ALIGN - 1) * (t // min(ROUTE_TILE, seq))
    n_blocks = (n_assign + n_run_pad + MOE_BLOCK - 1) // MOE_BLOCK + N_EXPERTS
    n_slots = n_blocks * MOE_BLOCK

    w_cat, widths = _layout_w_in(w_in)
    n_gla, n_hg = N_HEADS * GLA_DV, N_HEADS * HG_DV
    wo_p = jnp.concatenate([_tile_rows(w_out[:, :n_gla], GLA_DV), w_out[:, n_gla:n_gla + n_hg],
                            _tile_rows(w_out[:, n_gla + n_hg:], RET_DV)], axis=1).astype(BF16)
    wa2p = jnp.pad(_pair_cols(gla_wa2, GLA_DK), ((0, 0), (0, LANE - GLA_RANK), (0, 0)))
    ba2p = _pair_cols(gla_ba2, GLA_DK)
    gnw = _tile_cols(gla_norm_w, GLA_DV)
    wr_p = _pad_last(w_r, 0, LANE - N_EXPERTS)
    br_p = jnp.pad(b_r, ((0, 0), (0, LANE - N_EXPERTS)), constant_values=NEG_BIG)

    buf = jnp.zeros((n_slots, d), F32)
    xf = x.reshape(t, d)
    for layer in range(depth):
        sh1, sc1, g1, sh2, sc2, g2 = [m.reshape(nb, 1, d) for m in jnp.split(mod[layer], 6, axis=-1)]

        parts = _inproj_call(xf, norm1_w[layer], sc1, sh1, w_cat, layer, widths, dtypes, seq)
        gq, gk, gv, gg, ga, hq, hf, hi, hg, rq, rk, rv, rg = parts
        o_gla = _gla_call(gq, gk, gv, gg, ga, wa2p[layer], ba2p[layer:layer + 1], gnw[layer:layer + 1], nb, seq)
        o_hg = _hg_call(hq, hf, hi, hg, lb_all[layer:layer + 1], hg_norm_w[layer:layer + 1], nb, seq)
        o_ret = _ret_call(rq, rk, rv, rg, cos, sin, nb, seq)
        xf, h2, tw, lo, tab, cnt = _outproj_router_call(
            xf, o_gla, o_hg, o_ret, wo_p, layer, g1, norm2_w[layer], sc2, sh2, wr_p[layer],
            br_p[layer:layer + 1], seq)

        counts = cnt[0, :N_EXPERTS].astype(jnp.int32)
        padded = (counts + MOE_BLOCK - 1) // MOE_BLOCK * MOE_BLOCK
        pends = jnp.cumsum(padded)
        pstarts = pends - padded
        block_start = jnp.arange(n_blocks, dtype=jnp.int32) * MOE_BLOCK
        block_e = jnp.minimum(jnp.sum((pends[None, :] <= block_start[:, None]).astype(jnp.int32), axis=1),
                              N_EXPERTS - 1)
        n_used = (pends[-1:] // MOE_BLOCK).astype(jnp.int32)
        run_at = tab[:, 2, :] + jnp.pad(pstarts, (0, LANE - N_EXPERTS))[None, :]

        ids = jnp.arange(N_EXPERTS, dtype=jnp.int32)
        later = jnp.where((ids[None, :] > ids[:, None]) & (padded[None, :] > 0), ids[None, :], N_EXPERTS)
        follower = jnp.min(later, axis=1)
        follower = jnp.where(follower == N_EXPERTS, ids, follower)
        of_block = block_e[:, None] == ids[None, :]
        next_e = jnp.sum(jnp.where(of_block, follower[None, :], 0), axis=1)
        run_end = jnp.sum(jnp.where(of_block, (pstarts + counts)[None, :], 0), axis=1)
        block_used = jnp.clip(run_end - block_start, 0, MOE_BLOCK).astype(jnp.int32)

        buf = _dispatch_call(h2, tab, run_at, lo, buf)
        yb = _expert_call(buf, block_e, next_e, block_used, n_used, w_gu, b_gu, w_dn, b_dn, layer)
        xf = _combine_call(yb, tab, run_at, lo, tw, xf, g2, seq)

    return _final_norm_call(xf, final_norm_w).reshape(nb, seq, d)
```

```python
import functools

import numpy as np
import jax
import jax.numpy as jnp
from jax import lax
from jax.experimental import pallas as pl
from jax.experimental.pallas import tpu as pltpu

F32 = jnp.float32
BF16 = jnp.bfloat16

N_HEADS = 4
GLA_DK, GLA_DV, GLA_RANK, GLA_TAU = 48, 96, 16, 16.0
HG_DK, HG_DV = 64, 64
RET_DK, RET_DV = 48, 96
ROPE_BASE = 10000.0
N_EXPERTS, TOP_K = 32, 4
SWIGLU_LIMIT, SWIGLU_ALPHA = 7.0, 1.702
MOE_BLOCK = 512
ROUTE_TILE = 512
RUN_ALIGN = 8
EPS = 1e-6

LANE = 128
HALF = LANE // 2
N_PAIRS = N_HEADS // 2
HEAD_W = N_HEADS * LANE
PAIR_W = N_PAIRS * LANE
VMEM_LIMIT = 56 * 1024 * 1024
NEG_BIG = -1e30


def _cparams(sem):
    return pltpu.CompilerParams(dimension_semantics=sem, vmem_limit_bytes=VMEM_LIMIT)


def _dot(a, b):
    return jnp.dot(a, b, preferred_element_type=F32)


def _dot_nt(a, b):
    return lax.dot_general(a, b, (((1,), (1,)), ((), ())), preferred_element_type=F32)


def _dot_tn(a, b):
    return lax.dot_general(a, b, (((0,), (0,)), ((), ())), preferred_element_type=F32)


def _split2(a):
    hi = a.astype(BF16)
    return hi, (a - hi.astype(F32)).astype(BF16)


def _dot_f32(a, b):
    a_hi, a_lo = _split2(a)
    b_hi, b_lo = _split2(b)
    return _dot(a_hi, b_hi) + _dot(a_hi, b_lo) + _dot(a_lo, b_hi)


def _sigmoid(x):
    return 1.0 / (1.0 + jnp.exp(-x))


def _log_sigmoid(x):
    return jnp.minimum(x, 0.0) - jnp.log1p(jnp.exp(-jnp.abs(x)))


def _mod_kernel(c_ref, w_ref, b_ref, o_ref):
    c = c_ref[...]
    o_ref[0] = _dot_f32(c * _sigmoid(c), w_ref[0]) + b_ref[0]


def _mod_call(c, w_mod, b_mod):
    depth, d, d6 = w_mod.shape
    nb = c.shape[0]
    rows = 8
    c_pad = jnp.zeros((rows, d), F32).at[:nb].set(c)
    out = pl.pallas_call(
        _mod_kernel,
        grid=(depth, d6 // d),
        in_specs=[
            pl.BlockSpec((rows, d), lambda l, j: (0, 0)),
            pl.BlockSpec((1, d, d), lambda l, j: (l, 0, j)),
            pl.BlockSpec((1, 1, d), lambda l, j: (l, 0, j)),
        ],
        out_specs=pl.BlockSpec((1, rows, d), lambda l, j: (l, 0, j)),
        out_shape=jax.ShapeDtypeStruct((depth, rows, d6), F32),
        compiler_params=_cparams(("parallel", "parallel")),
        name="mod",
    )(c_pad, w_mod, b_mod.reshape(depth, 1, d6))
    return out[:, :nb]


def _rope_kernel(pos_ref, freq_ref, cos_ref, sin_ref):
    ang = pos_ref[0] * freq_ref[...]
    cos_ref[0] = jnp.cos(ang)
    sin_ref[0] = jnp.sin(ang)


def _rope_call(positions):
    nb, l = positions.shape
    half = RET_DK // 2
    inv = (ROPE_BASE ** (-np.arange(half, dtype=np.float32) / half)).astype(np.float32)
    freq = np.zeros((1, LANE), np.float32)
    for base in (0, HALF):
        freq[0, base:base + half] = inv
        freq[0, base + half:base + 2 * half] = inv
    tl = min(l, 512)
    pos = positions.astype(F32).reshape(nb, l, 1)
    shp = jax.ShapeDtypeStruct((nb, l, LANE), F32)
    return pl.pallas_call(
        _rope_kernel,
        grid=(nb, l // tl),
        in_specs=[pl.BlockSpec((1, tl, 1), lambda b, i: (b, i, 0)),
                  pl.BlockSpec((1, LANE), lambda b, i: (0, 0))],
        out_specs=[pl.BlockSpec((1, tl, LANE), lambda b, i: (b, i, 0))] * 2,
        out_shape=[shp, shp],
        compiler_params=_cparams(("parallel", "parallel")),
        name="rope_tables",
    )(pos, jnp.asarray(freq))


def _lb_kernel(p_ref, o_ref):
    p = p_ref[...]
    depth = p.shape[0]
    m = jnp.max(p, axis=0, keepdims=True)
    e = jnp.exp(p - m)
    sm = e / jnp.sum(e, axis=0, keepdims=True)
    acc = jnp.zeros_like(sm[0:1])
    for i in range(depth):
        if i > 0:
            acc = acc + sm[i:i + 1]
        o_ref[i:i + 1, :] = acc


def _lb_call(hg_lb):
    return pl.pallas_call(
        _lb_kernel,
        out_shape=jax.ShapeDtypeStruct(hg_lb.shape, F32),
        name="hg_lower_bounds",
    )(hg_lb.astype(F32))


def _inproj_kernel(x_ref, nw_ref, sc_ref, sh_ref, w_ref, *out_refs, widths):
    x = x_ref[...]
    y = x * lax.rsqrt(jnp.mean(x * x, axis=-1, keepdims=True) + EPS) * nw_ref[...]
    hb = (y * (1.0 + sc_ref[0]) + sh_ref[0]).astype(BF16)
    off = 0
    for o_ref, wd in zip(out_refs, widths):
        o_ref[...] = _dot(hb, w_ref[:, off:off + wd]).astype(o_ref.dtype)
        off += wd


def _inproj_call(x, norm_w, sc, sh, w_cat, layer, widths, dtypes, seq):
    t, d = x.shape
    tm = min(512, seq)
    per_b = seq // tm
    return pl.pallas_call(
        functools.partial(_inproj_kernel, widths=widths),
        grid=(t // tm,),
        in_specs=[
            pl.BlockSpec((tm, d), lambda i: (i, 0)),
            pl.BlockSpec((1, d), lambda i: (0, 0)),
            pl.BlockSpec((1, 1, d), lambda i: (i // per_b, 0, 0)),
            pl.BlockSpec((1, 1, d), lambda i: (i // per_b, 0, 0)),
            pl.BlockSpec((None,) + w_cat.shape[1:], lambda i: (layer, 0, 0)),
        ],
        out_specs=[pl.BlockSpec((tm, wd), lambda i: (i, 0)) for wd in widths],
        out_shape=[jax.ShapeDtypeStruct((t, wd), dt) for wd, dt in zip(widths, dtypes)],
        compiler_params=_cparams(("parallel",)),
        name="norm1_inproj",
    )(x, norm_w.reshape(1, d), sc, sh, w_cat)


def _level_tables(c):
    nl = int(np.log2(c))
    assert 1 << nl == c
    idx = np.arange(c)
    mats, masks = [], [np.eye(c, dtype=np.float32)]
    t = idx[None, :]
    i = idx[:, None]
    for lvl in range(nl):
        h = c >> (lvl + 1)
        blk, pos = idx // (2 * h), idx % (2 * h)
        m = (blk * 2 * h + h - 1)[:, None]
        right = (pos >= h)[:, None]
        a = np.where(right, (t > m) & (t <= i), (t > i) & (t <= m))
        mats.append(a.astype(np.float32))
        same = blk[:, None] == blk[None, :]
        masks.append((same & right & (pos < h)[None, :]).astype(np.float32))
    mats.append((t <= i).astype(np.float32))
    mats.append((t > i).astype(np.float32))
    return np.concatenate(mats, 0), np.stack(masks, 0), nl


def _pair_decays(g, mall_ref):
    gs = _dot(mall_ref[...], jnp.concatenate(_split2(g), axis=1))
    return jnp.exp(gs[:, :LANE] + gs[:, LANE:])


def _pair_keys(k, e, c, nl):
    levels = [(k * e[lvl * c:(lvl + 1) * c]).astype(BF16) for lvl in range(nl)]
    return k.astype(BF16), levels, (k * e[(nl + 1) * c:(nl + 2) * c]).astype(BF16)


def _gated_head(q, kb, klv, e, v, st, masks_ref, c, nl):
    scores = masks_ref[0] * _dot_nt(q.astype(BF16), kb)
    for lvl in range(nl):
        scores = scores + masks_ref[lvl + 1] * _dot_nt((q * e[lvl * c:(lvl + 1) * c]).astype(BF16), klv[lvl])
    q_in = (q * e[nl * c:(nl + 1) * c]).astype(BF16)
    return _dot(scores.astype(BF16), v) + _dot_nt(q_in, st.astype(BF16))


def _gla_kernel(q_ref, k_ref, v_ref, gg_ref, ga_ref, wa2_ref, ba2_ref, nw_ref, mall_ref, masks_ref,
                o_ref, st_ref, *, c, nl):
    for b in range(q_ref.shape[0]):
        log_alpha = _log_sigmoid(_dot_f32(ga_ref[b], wa2_ref[...]) + ba2_ref[...]) * (1.0 / GLA_TAU)
        for pr in range(N_PAIRS):
            psl = slice(pr * LANE, (pr + 1) * LANE)
            e = _pair_decays(log_alpha[:, psl], mall_ref)
            kb, klv, k_end = _pair_keys(k_ref[b, :, psl].astype(F32), e, c, nl)
            dec = e[(nl + 1) * c - 1:(nl + 1) * c, :]
            for hd in (2 * pr, 2 * pr + 1):
                sl = slice(hd * LANE, (hd + 1) * LANE)
                q = q_ref[b, :, sl].astype(F32) * (GLA_DK ** -0.5)
                v = v_ref[b, :, sl]
                st = st_ref[b, hd]
                o = _gated_head(q, kb, klv, e, v, st, masks_ref, c, nl)
                st_ref[b, hd] = dec * st + _dot_tn(v, k_end)
                y = o * lax.rsqrt(jnp.sum(o * o, axis=-1, keepdims=True) * (1.0 / GLA_DV) + EPS) * nw_ref[:, sl]
                gate = gg_ref[b, :, sl].astype(F32)
                o_ref[b, :, sl] = (y * gate * _sigmoid(gate)).astype(o_ref.dtype)


def _hg_kernel(q_ref, f_ref, v_ref, og_ref, lb_ref, nw_ref, mall_ref, masks_ref,
               o_ref, st_ref, *, c, nl):
    low = lax.broadcasted_iota(jnp.int32, (c, LANE), 1) < HALF
    for b in range(q_ref.shape[0]):
        for pr in range(N_PAIRS):
            psl = slice(pr * LANE, (pr + 1) * LANE)
            lb = lb_ref[:, psl]
            hf = f_ref[b, :, psl]
            la = jnp.log(lb)
            lc = jnp.log1p(-lb) + _log_sigmoid(hf)
            log_f = jnp.maximum(la, lc) + jnp.log1p(jnp.exp(-jnp.abs(la - lc)))
            e = _pair_decays(log_f, mall_ref)
            kb, klv, k_end = _pair_keys((1.0 - lb) * _sigmoid(-hf), e, c, nl)
            v = v_ref[b, :, psl]
            st = st_ref[b, pr]
            o_even = _gated_head(q_ref[b, :, (2 * pr) * LANE:(2 * pr + 1) * LANE].astype(F32),
                                 kb, klv, e, v, st, masks_ref, c, nl)
            o_odd = _gated_head(q_ref[b, :, (2 * pr + 1) * LANE:(2 * pr + 2) * LANE].astype(F32),
                                kb, klv, e, v, st, masks_ref, c, nl)
            o = jnp.where(low, o_even, o_odd)
            st_ref[b, pr] = e[(nl + 1) * c - 1:(nl + 1) * c, :] * st + _dot_tn(v, k_end)
            sq = o * o
            ms = jnp.where(low, jnp.sum(jnp.where(low, sq, 0.0), axis=-1, keepdims=True),
                           jnp.sum(jnp.where(low, 0.0, sq), axis=-1, keepdims=True)) * (1.0 / HG_DV)
            y = o * lax.rsqrt(ms + EPS) * nw_ref[:, psl]
            o_ref[b, :, psl] = (y * _sigmoid(og_ref[b, :, psl].astype(F32))).astype(o_ref.dtype)


def _tok_spec(nb, c, width):
    return pl.BlockSpec((nb, c, width), lambda i: (0, i, 0))


def _const_spec(shape):
    nd = len(shape)
    return pl.BlockSpec(shape, lambda i: (0,) * nd)


def _seq_view(a, nb, seq):
    return a.reshape(nb, seq, a.shape[-1])


def _ret_tables(c):
    hs = np.arange(N_HEADS, dtype=np.float64)
    log_gamma = np.log(1.0 - np.exp2(-5.0 - hs))
    idx = np.arange(c, dtype=np.float64)
    rel = idx[:, None] - idx[None, :]
    dmat = np.where(rel >= 0, np.exp(log_gamma[:, None, None] * np.maximum(rel, 0.0)), 0.0)
    qdec = np.exp(log_gamma[:, None] * (idx + 1.0))
    kdec = np.exp(log_gamma[:, None] * (c - 1.0 - idx))
    cdec = np.exp(log_gamma * c)
    qfull = np.broadcast_to(qdec[:, :, None], (N_HEADS, c, LANE)).astype(np.float32)
    kpair = np.zeros((N_PAIRS, c, LANE), np.float32)
    for hd in range(N_HEADS):
        kpair[hd // 2, :, (hd % 2) * HALF:(hd % 2 + 1) * HALF] = kdec[hd][:, None]
    return dmat.astype(np.float32), qfull, kpair, [float(np.float32(v)) for v in cdec]


def _ret_kernel(q_ref, k_ref, v_ref, og_ref, cos_ref, sin_ref, dmat_ref, qdec_ref, kdec_ref,
                o_ref, st_ref, *, c, cdec):
    half = RET_DK // 2
    lane = lax.broadcasted_iota(jnp.int32, (c, LANE), 1)
    first = (lane & (HALF - 1)) < half

    for b in range(q_ref.shape[0]):
        cos = cos_ref[b]
        sin = sin_ref[b]

        def rotary(t):
            rot = jnp.where(first, -pltpu.roll(t, LANE - half, 1), pltpu.roll(t, half, 1))
            return t * cos + rot * sin

        for pr in range(N_PAIRS):
            psl = slice(pr * LANE, (pr + 1) * LANE)
            k = rotary(k_ref[b, :, psl].astype(F32))
            kb = k.astype(BF16)
            k_end = (k * kdec_ref[pr]).astype(BF16)
            for hd in (2 * pr, 2 * pr + 1):
                sl = slice(hd * LANE, (hd + 1) * LANE)
                q = rotary(q_ref[b, :, sl].astype(F32)) * (RET_DK ** -0.5)
                v = v_ref[b, :, sl]
                st = st_ref[b, hd]
                scores = _dot_nt(q.astype(BF16), kb) * dmat_ref[hd]
                o = _dot(scores.astype(BF16), v) + _dot_nt((q * qdec_ref[hd]).astype(BF16), st.astype(BF16))
                st_ref[b, hd] = cdec[hd] * st + _dot_tn(v, k_end)
                mu = jnp.sum(o, axis=-1, keepdims=True) * (1.0 / RET_DV)
                dlt = jnp.where(lane < RET_DV, o - mu, 0.0)
                var = jnp.sum(dlt * dlt, axis=-1, keepdims=True) * (1.0 / RET_DV)
                gate = og_ref[b, :, sl].astype(F32)
                o_ref[b, :, sl] = (dlt * lax.rsqrt(var + EPS) * gate * _sigmoid(gate)).astype(o_ref.dtype)


def _mixer_kernel(gq, gk, gv, gg, ga, wa2, ba2, gnw, hq, hf, hi, hg, lb, hnw, rq, rk, rv, rg, cos, sin,
                  mall, masks, dmat, qdec, kdec, o_gla, o_hg, o_ret, st_gla, st_hg, st_ret, *, c, nl, cdec):
    @pl.when(pl.program_id(0) == 0)
    def _():
        st_gla[...] = jnp.zeros_like(st_gla)
        st_hg[...] = jnp.zeros_like(st_hg)
        st_ret[...] = jnp.zeros_like(st_ret)

    _gla_kernel(gq, gk, gv, gg, ga, wa2, ba2, gnw, mall, masks, o_gla, st_gla, c=c, nl=nl)
    _hg_kernel(hq, hf, hi, hg, lb, hnw, mall, masks, o_hg, st_hg, c=c, nl=nl)
    _ret_kernel(rq, rk, rv, rg, cos, sin, dmat, qdec, kdec, o_ret, st_ret, c=c, cdec=cdec)


def _mixer_call(parts, wa2p, ba2p, gnw, lbp, hnw, cos, sin, nb, seq):
    gq, gk, gv, gg, ga, hq, hf, hi, hg, rq, rk, rv, rg = parts
    c = min(128, seq)
    n = seq // c
    mall, masks, nl = _level_tables(c)
    dmat, qdec, kdec, cdec = _ret_tables(c)
    consts = [wa2p, ba2p, gnw]
    sv = lambda a: _seq_view(a, nb, seq)
    tok = lambda a: _tok_spec(nb, c, a.shape[-1])
    const = lambda a: _const_spec(a.shape)
    tables = [jnp.asarray(mall, BF16), jnp.asarray(masks), jnp.asarray(dmat), jnp.asarray(qdec),
              jnp.asarray(kdec)]
    o_gla, o_hg, o_ret = pl.pallas_call(
        functools.partial(_mixer_kernel, c=c, nl=nl, cdec=cdec),
        grid=(n,),
        in_specs=([tok(a) for a in (gq, gk, gv, gg, ga)] + [const(a) for a in consts]
                  + [tok(a) for a in (hq, hf, hi, hg)] + [const(lbp), const(hnw)]
                  + [tok(a) for a in (rq, rk, rv, rg, cos, sin)] + [const(a) for a in tables]),
        out_specs=[_tok_spec(nb, c, HEAD_W), _tok_spec(nb, c, PAIR_W), _tok_spec(nb, c, HEAD_W)],
        out_shape=[jax.ShapeDtypeStruct((nb, seq, HEAD_W), BF16), jax.ShapeDtypeStruct((nb, seq, PAIR_W), BF16),
                   jax.ShapeDtypeStruct((nb, seq, HEAD_W), BF16)],
        scratch_shapes=[pltpu.VMEM((nb, N_HEADS, LANE, LANE), F32), pltpu.VMEM((nb, N_PAIRS, LANE, LANE), F32),
                        pltpu.VMEM((nb, N_HEADS, LANE, LANE), F32)],
        compiler_params=_cparams(("arbitrary",)),
        name="token_mixers",
    )(sv(gq), sv(gk), sv(gv), sv(gg), sv(ga), wa2p, ba2p, gnw, sv(hq), sv(hf), sv(hi), sv(hg), lbp, hnw,
      sv(rq), sv(rk), sv(rv), sv(rg), cos, sin, *tables)
    t = nb * seq
    return o_gla.reshape(t, HEAD_W), o_hg.reshape(t, PAIR_W), o_ret.reshape(t, HEAD_W)


def _outproj_router_kernel(x_ref, oa_ref, ob_ref, oc_ref, wo_ref, g1_ref, nw_ref, sc_ref, sh_ref,
                           wr_ref, br_ref, tri_ref, upper_ref,
                           xo_ref, h_ref, tw_ref, lo_ref, tab_ref, cnt_ref, *, tm):
    @pl.when(pl.program_id(0) == 0)
    def _():
        cnt_ref[...] = jnp.zeros_like(cnt_ref)

    wa, wb = oa_ref.shape[1], ob_ref.shape[1]
    mix = (_dot(oa_ref[...], wo_ref[0:wa, :]) + _dot(ob_ref[...], wo_ref[wa:wa + wb, :])
           + _dot(oc_ref[...], wo_ref[wa + wb:, :]))
    x = x_ref[...] + g1_ref[0] * mix
    xo_ref[...] = x
    y = x * lax.rsqrt(jnp.mean(x * x, axis=-1, keepdims=True) + EPS) * nw_ref[...]
    h = y * (1.0 + sc_ref[0]) + sh_ref[0]
    h_ref[...] = h.astype(h_ref.dtype)

    lg = _dot_f32(h, wr_ref[...]) + br_ref[...]
    lane = lax.broadcasted_iota(jnp.int32, (tm, LANE), 1)
    sel_e, sel_v = [], []
    for _ in range(TOP_K):
        m = jnp.max(lg, axis=-1, keepdims=True)
        idx = jnp.min(jnp.where(lg == m, lane, LANE), axis=-1, keepdims=True)
        sel_e.append(idx)
        sel_v.append(m)
        lg = jnp.where(lane == idx, -jnp.inf, lg)
    ex = [jnp.exp(v - sel_v[0]) for v in sel_v]
    den = ex[0] + ex[1] + ex[2] + ex[3]
    hot = [(lane == idx) for idx in sel_e]
    onehot = jnp.zeros((tm, LANE), F32)
    for hk in hot:
        onehot = onehot + jnp.where(hk, 1.0, 0.0)
    in_tile = _dot(tri_ref[...], onehot.astype(BF16))
    earlier = cnt_ref[...]
    tile_cnt = jnp.floor((jnp.sum(onehot, axis=0, keepdims=True) + (RUN_ALIGN - 1.0)) * (1.0 / RUN_ALIGN)) * RUN_ALIGN
    tile_start = _dot_f32(jnp.broadcast_to(tile_cnt, (8, LANE)), upper_ref[...])[0:1]
    local = tile_start + in_tile
    tw = jnp.zeros((tm, LANE), F32)
    lo = jnp.zeros((tm, LANE), jnp.int32)
    for kk in range(TOP_K):
        lrow = jnp.sum(jnp.where(hot[kk], local, 0.0), axis=-1, keepdims=True).astype(jnp.int32)
        tw = jnp.where(lane == kk, ex[kk] / den, tw)
        lo = jnp.where(lane == kk, lrow, lo)
    tw_ref[...] = tw
    lo_ref[...] = lo
    tab_ref[0, 0:1, :] = tile_start.astype(jnp.int32)
    tab_ref[0, 1:2, :] = tile_cnt.astype(jnp.int32)
    tab_ref[0, 2:3, :] = earlier.astype(jnp.int32)
    tab_ref[0, 3:4, :] = jnp.broadcast_to(jnp.sum(tile_cnt, axis=-1, keepdims=True), (1, LANE)).astype(jnp.int32)
    tab_ref[0, 4:8, :] = jnp.zeros((4, LANE), jnp.int32)
    cnt_ref[...] = earlier + tile_cnt


def _outproj_router_call(x, oa, ob, oc, wo, layer, g1, nw, sc, sh, wr, br, seq):
    t, d = x.shape
    tm = min(ROUTE_TILE, seq)
    per_b = seq // tm
    tri = np.tril(np.ones((tm, tm), np.float32), -1)
    upper = np.triu(np.ones((LANE, LANE), np.float32), 1)
    row = lambda w: pl.BlockSpec((tm, w), lambda i: (i, 0))
    const = lambda shape: pl.BlockSpec(shape, lambda i: (0,) * len(shape))
    perb = pl.BlockSpec((1, 1, d), lambda i: (i // per_b, 0, 0))
    n_tiles = t // tm
    return pl.pallas_call(
        functools.partial(_outproj_router_kernel, tm=tm),
        grid=(n_tiles,),
        in_specs=[row(d), row(oa.shape[1]), row(ob.shape[1]), row(oc.shape[1]),
                  pl.BlockSpec((None,) + wo.shape[1:], lambda i: (layer, 0, 0)), perb,
                  const((1, d)), perb, perb, const(wr.shape), const(br.shape), const(tri.shape),
                  const(upper.shape)],
        out_specs=[row(d), row(d), row(LANE), row(LANE),
                   pl.BlockSpec((1, 8, LANE), lambda i: (i, 0, 0)), const((1, LANE))],
        out_shape=[jax.ShapeDtypeStruct((t, d), F32), jax.ShapeDtypeStruct((t, d), BF16),
                   jax.ShapeDtypeStruct((t, LANE), F32), jax.ShapeDtypeStruct((t, LANE), jnp.int32),
                   jax.ShapeDtypeStruct((n_tiles, 8, LANE), jnp.int32), jax.ShapeDtypeStruct((1, LANE), F32)],
        compiler_params=_cparams(("arbitrary",)),
        name="outproj_norm2_router",
    )(x, oa, ob, oc, wo, g1, nw.reshape(1, d), sc, sh, wr, br, jnp.asarray(tri, BF16), jnp.asarray(upper))


def _run_copies(tab_ref, at_ref, tile, tm, make_copy, act):
    sizes = [s for s in (1 << p for p in range(tm.bit_length() - 1, -1, -1)) if s >= RUN_ALIGN]

    def per_expert(e, carry):
        first = tab_ref[tile, 0, e]
        length = tab_ref[tile, 1, e]
        at = at_ref[tile, e]
        done = jnp.int32(0)
        for size in sizes:
            part = length & size

            @pl.when(part != 0)
            def _():
                act(make_copy(pl.multiple_of(first + done, RUN_ALIGN), pl.multiple_of(at + done, RUN_ALIGN), size))
            done = done + part
        return carry

    lax.fori_loop(0, N_EXPERTS, per_expert, 0)


def _local_rows(tm):
    return TOP_K * tm + N_EXPERTS * RUN_ALIGN


def _dispatch_kernel(tab_ref, at_ref, h_ref, lo_ref, buf_in_ref, buf_ref, st_ref, sem, *, tm, n_tiles):
    del buf_in_ref
    i = pl.program_id(0)
    slot = i % 2

    def copies(tile, s, act):
        _run_copies(tab_ref, at_ref, tile, tm,
                    lambda row, at, size: pltpu.make_async_copy(st_ref.at[s, pl.ds(row, size)],
                                                                buf_ref.at[pl.ds(at, size)], sem.at[s]), act)

    def wait_tile(tile, s):
        total = pl.multiple_of(tab_ref[tile, 3, 0], RUN_ALIGN)
        pltpu.make_async_copy(st_ref.at[s, pl.ds(0, total)], buf_ref.at[pl.ds(0, total)], sem.at[s]).wait()

    @pl.when(i >= 2)
    def _():
        wait_tile(i - 2, slot)

    col = lax.broadcasted_iota(jnp.int32, (tm, st_ref.shape[1]), 1)
    lo = lo_ref[...]
    place = jnp.zeros(col.shape, F32)
    for kk in range(TOP_K):
        place = jnp.where(col == lo[:, kk:kk + 1], 1.0, place)
    st_ref[slot] = _dot_tn(place.astype(BF16), h_ref[...].astype(BF16))
    copies(i, slot, lambda cp: cp.start())

    @pl.when(i == n_tiles - 1)
    def _():
        if n_tiles >= 2:
            wait_tile(i - 1, 1 - slot)
        wait_tile(i, slot)


def _dispatch_call(h, tab, run_at, lo, buf, seq):
    t, d = h.shape
    n_slots = buf.shape[0]
    tm = min(ROUTE_TILE, seq)
    n_tiles = t // tm
    grid_spec = pltpu.PrefetchScalarGridSpec(
        num_scalar_prefetch=2,
        grid=(n_tiles,),
        in_specs=[pl.BlockSpec((tm, d), lambda i, tb, sr: (i, 0)),
                  pl.BlockSpec((tm, LANE), lambda i, tb, sr: (i, 0)),
                  pl.BlockSpec(memory_space=pl.ANY)],
        out_specs=pl.BlockSpec(memory_space=pl.ANY),
        scratch_shapes=[pltpu.VMEM((2, _local_rows(tm), d), F32), pltpu.SemaphoreType.DMA((2,))],
    )
    return pl.pallas_call(
        functools.partial(_dispatch_kernel, tm=tm, n_tiles=n_tiles),
        grid_spec=grid_spec,
        out_shape=jax.ShapeDtypeStruct((n_slots, d), F32),
        input_output_aliases={4: 0},
        compiler_params=_cparams(("arbitrary",)),
        name="moe_dispatch",
    )(tab, run_at, h, lo, buf)


def _expert_kernel(be_ref, nx_ref, used_ref, nu_ref, x_ref, wgu_ref, bgu_ref, wdn_ref, bdn_ref, o_ref,
                   wgu_f32, wdn_f32, wgu_bf, wdn_bf, sem, *, d_ff, layer):
    i = pl.program_id(0)
    live = i < nu_ref[0]
    e = be_ref[i]
    e_next = nx_ref[i]

    def weight_copies(expert):
        return (pltpu.make_async_copy(wgu_ref.at[layer, expert], wgu_f32, sem.at[0]),
                pltpu.make_async_copy(wdn_ref.at[layer, expert], wdn_f32, sem.at[1]))

    @pl.when(i == 0)
    def _():
        for cp in weight_copies(e):
            cp.start()

    @pl.when(jnp.logical_and(live, jnp.logical_or(i == 0, e != be_ref[jnp.maximum(i - 1, 0)])))
    def _():
        for cp in weight_copies(e):
            cp.wait()
        wgu_bf[...] = wgu_f32[...].astype(BF16)
        wdn_bf[...] = wdn_f32[...].astype(BF16)

        @pl.when(e_next != e)
        def _():
            for cp in weight_copies(e_next):
                cp.start()

    def ffn(rows):
        gu = _dot(x_ref[0:rows, :].astype(BF16), wgu_bf[...]) + bgu_ref[0, 0]
        gate = jnp.minimum(gu[:, :d_ff], SWIGLU_LIMIT)
        up = jnp.clip(gu[:, d_ff:], -SWIGLU_LIMIT, SWIGLU_LIMIT)
        act = (up + 1.0) * gate * _sigmoid(SWIGLU_ALPHA * gate)
        o_ref[0:rows, :] = _dot(act.astype(BF16), wdn_bf[...]) + bdn_ref[0, 0]
        if rows < MOE_BLOCK:
            o_ref[rows:, :] = jnp.zeros((MOE_BLOCK - rows, o_ref.shape[1]), o_ref.dtype)

    used = used_ref[i]
    prefixes = [MOE_BLOCK // 4, MOE_BLOCK // 2, MOE_BLOCK]
    for lo_rows, rows in zip([0] + prefixes[:-1], prefixes):
        pl.when(jnp.logical_and(live, jnp.logical_and(used > lo_rows, used <= rows)))(
            functools.partial(ffn, rows))

    @pl.when(jnp.logical_not(live))
    def _():
        o_ref[...] = jnp.zeros_like(o_ref)


def _expert_call(buf, block_e, next_e, block_used, n_used, wgu, bgu, wdn, bdn, layer):
    n_slots, d = buf.shape
    n_blocks = n_slots // MOE_BLOCK
    depth, ne, _, f2 = wgu.shape
    d_ff = f2 // 2
    blk = lambda i, be, nx, us, nu: (jnp.minimum(i, nu[0] - 1), 0)
    exp4 = lambda i, be, nx, us, nu: (layer, be[jnp.minimum(i, nu[0] - 1)], 0, 0)
    grid_spec = pltpu.PrefetchScalarGridSpec(
        num_scalar_prefetch=4,
        grid=(n_blocks,),
        in_specs=[pl.BlockSpec((MOE_BLOCK, d), blk),
                  pl.BlockSpec(memory_space=pl.ANY),
                  pl.BlockSpec((1, 1, 1, f2), exp4),
                  pl.BlockSpec(memory_space=pl.ANY),
                  pl.BlockSpec((1, 1, 1, d), exp4)],
        out_specs=pl.BlockSpec((MOE_BLOCK, d), lambda i, be, nx, us, nu: (i, 0)),
        scratch_shapes=[pltpu.VMEM((d, f2), F32), pltpu.VMEM((d_ff, d), F32),
                        pltpu.VMEM((d, f2), BF16), pltpu.VMEM((d_ff, d), BF16),
                        pltpu.SemaphoreType.DMA((2,))],
    )
    return pl.pallas_call(
        functools.partial(_expert_kernel, d_ff=d_ff, layer=layer),
        grid_spec=grid_spec,
        out_shape=jax.ShapeDtypeStruct((n_slots, d), F32),
        compiler_params=_cparams(("arbitrary",)),
        name="moe_experts",
    )(block_e, next_e, block_used, n_used, buf, wgu, bgu.reshape(depth, ne, 1, f2), wdn, bdn.reshape(depth, ne, 1, d))


def _combine_kernel(tab_ref, at_ref, yb_ref, lo_ref, w_ref, x_ref, g2_ref, o_ref, rs_ref, sem, *, tm):
    i = pl.program_id(0)
    n = pl.num_programs(0)
    rows = rs_ref.shape[1]

    def run_copies(tile, s, act):
        _run_copies(tab_ref, at_ref, tile, tm,
                    lambda row, at, size: pltpu.make_async_copy(yb_ref.at[pl.ds(at, size)],
                                                                rs_ref.at[s, pl.ds(row, size)], sem.at[s]), act)

    @pl.when(i == 0)
    def _():
        rs_ref[...] = jnp.zeros_like(rs_ref)
        run_copies(0, 0, lambda cp: cp.start())

    @pl.when(i + 1 < n)
    def _():
        run_copies(i + 1, (i + 1) % 2, lambda cp: cp.start())

    slot = i % 2
    total = pl.multiple_of(tab_ref[i, 3, 0], RUN_ALIGN)
    pltpu.make_async_copy(yb_ref.at[pl.ds(0, total)], rs_ref.at[slot, pl.ds(0, total)], sem.at[slot]).wait()
    r = rs_ref[slot].astype(BF16)
    col = lax.broadcasted_iota(jnp.int32, (tm, rows), 1)
    w = w_ref[...]
    lo = lo_ref[...]
    pw = jnp.zeros((tm, rows), F32)
    for kk in range(TOP_K):
        pw = jnp.where(col == lo[:, kk:kk + 1], w[:, kk:kk + 1], pw)
    y = _dot(pw.astype(BF16), r)
    o_ref[...] = x_ref[...] + g2_ref[0] * y


def _combine_call(yb, tab, src, lo, tw, x, g2, seq):
    t, d = x.shape
    tm = min(ROUTE_TILE, seq)
    per_b = seq // tm
    grid_spec = pltpu.PrefetchScalarGridSpec(
        num_scalar_prefetch=2,
        grid=(t // tm,),
        in_specs=[pl.BlockSpec(memory_space=pl.ANY),
                  pl.BlockSpec((tm, LANE), lambda i, tb, sr: (i, 0)),
                  pl.BlockSpec((tm, LANE), lambda i, tb, sr: (i, 0)),
                  pl.BlockSpec((tm, d), lambda i, tb, sr: (i, 0)),
                  pl.BlockSpec((1, 1, d), lambda i, tb, sr: (i // per_b, 0, 0))],
        out_specs=pl.BlockSpec((tm, d), lambda i, tb, sr: (i, 0)),
        scratch_shapes=[pltpu.VMEM((2, _local_rows(tm), d), F32), pltpu.SemaphoreType.DMA((2,))],
    )
    return pl.pallas_call(
        functools.partial(_combine_kernel, tm=tm),
        grid_spec=grid_spec,
        out_shape=jax.ShapeDtypeStruct((t, d), F32),
        compiler_params=_cparams(("arbitrary",)),
        name="moe_combine",
    )(tab, src, yb, lo, tw, x, g2)


def _final_norm_kernel(x_ref, w_ref, o_ref):
    x = x_ref[...]
    o_ref[...] = x * lax.rsqrt(jnp.mean(x * x, axis=-1, keepdims=True) + EPS) * w_ref[...]


def _final_norm_call(x, w):
    t, d = x.shape
    tm = min(512, t)
    return pl.pallas_call(
        _final_norm_kernel,
        grid=(t // tm,),
        in_specs=[pl.BlockSpec((tm, d), lambda i: (i, 0)), pl.BlockSpec((1, d), lambda i: (0, 0))],
        out_specs=pl.BlockSpec((tm, d), lambda i: (i, 0)),
        out_shape=jax.ShapeDtypeStruct((t, d), F32),
        compiler_params=_cparams(("parallel",)),
        name="final_norm",
    )(x, w.reshape(1, d))


def _split_heads(w, hd):
    return w.reshape(w.shape[:-1] + (N_HEADS, hd))


def _pad_last(w, lo, hi):
    return jnp.pad(w, [(0, 0)] * (w.ndim - 1) + [(lo, hi)])


def _tile_cols(w, hd):
    return _pad_last(_split_heads(w, hd), 0, LANE - hd).reshape(w.shape[:-1] + (HEAD_W,))


def _pair_cols(w, hd):
    return _pad_last(_split_heads(w, hd), 0, HALF - hd).reshape(w.shape[:-1] + (PAIR_W,))


def _query_cols(w, hd):
    w = _split_heads(w, hd)
    tiles = [_pad_last(w[..., h, :], (h % 2) * HALF, LANE - (h % 2) * HALF - hd) for h in range(N_HEADS)]
    return jnp.concatenate(tiles, axis=-1)


def _tile_rows(w, hd):
    lead, cols = w.shape[:-2], w.shape[-1]
    w = w.reshape(lead + (N_HEADS, hd, cols))
    w = jnp.pad(w, [(0, 0)] * (w.ndim - 2) + [(0, LANE - hd), (0, 0)])
    return w.reshape(lead + (HEAD_W, cols))


IN_PARTS = (("gq", GLA_DK, _query_cols), ("gk", GLA_DK, _pair_cols), ("gv", GLA_DV, _tile_cols),
            ("gg", GLA_DV, _tile_cols), ("ga", None, None),
            ("hq", HG_DK, _query_cols), ("hf", HG_DK, _pair_cols), ("hi", HG_DV, _pair_cols),
            ("hg", HG_DV, _pair_cols),
            ("rq", RET_DK, _query_cols), ("rk", RET_DK, _pair_cols), ("rv", RET_DV, _tile_cols),
            ("rg", RET_DV, _tile_cols))
F32_PARTS = ("ga", "hf")


def _layout_w_in(w_in):
    cols, off = [], 0
    for _, hd, layout in IN_PARTS:
        if layout is None:
            part = _pad_last(w_in[..., off:off + GLA_RANK], 0, LANE - GLA_RANK)
            off += GLA_RANK
        else:
            part = layout(w_in[..., off:off + N_HEADS * hd], hd)
            off += N_HEADS * hd
        cols.append(part)
    return jnp.concatenate(cols, axis=-1).astype(BF16), tuple(int(p.shape[-1]) for p in cols)


def kernel(x, c, positions, w_mod, b_mod, norm1_w, w_in, gla_wa2, gla_ba2, hg_lb, gla_norm_w, hg_norm_w,
           w_out, norm2_w, w_r, b_r, w_gu, b_gu, w_dn, b_dn, final_norm_w):
    nb, seq, d = x.shape
    t = nb * seq
    depth = w_mod.shape[0]
    dtypes = tuple(F32 if name in F32_PARTS else BF16 for name, _, _ in IN_PARTS)

    mod = _mod_call(c, w_mod, b_mod)
    cos, sin = _rope_call(positions)
    lb_all = _lb_call(hg_lb)

    n_assign = t * TOP_K
    n_run_pad = N_EXPERTS * (RUN_ALIGN - 1) * (t // min(ROUTE_TILE, seq))
    n_blocks = (n_assign + n_run_pad + MOE_BLOCK - 1) // MOE_BLOCK + N_EXPERTS
    n_slots = n_blocks * MOE_BLOCK

    w_cat, widths = _layout_w_in(w_in)
    n_gla, n_hg = N_HEADS * GLA_DV, N_HEADS * HG_DV
    wo_p = jnp.concatenate([_tile_rows(w_out[:, :n_gla], GLA_DV), w_out[:, n_gla:n_gla + n_hg],
                            _tile_rows(w_out[:, n_gla + n_hg:], RET_DV)], axis=1).astype(BF16)
    wa2p = jnp.pad(_pair_cols(gla_wa2, GLA_DK), ((0, 0), (0, LANE - GLA_RANK), (0, 0)))
    ba2p = _pair_cols(gla_ba2, GLA_DK)
    gnw = _tile_cols(gla_norm_w, GLA_DV)
    wr_p = _pad_last(w_r, 0, LANE - N_EXPERTS)
    br_p = jnp.pad(b_r, ((0, 0), (0, LANE - N_EXPERTS)), constant_values=NEG_BIG)

    buf = jnp.zeros((n_slots, d), F32)
    xf = x.reshape(t, d)
    for layer in range(depth):
        sh1, sc1, g1, sh2, sc2, g2 = [m.reshape(nb, 1, d) for m in jnp.split(mod[layer], 6, axis=-1)]

        parts = _inproj_call(xf, norm1_w[layer], sc1, sh1, w_cat, layer, widths, dtypes, seq)
        o_gla, o_hg, o_ret = _mixer_call(parts, wa2p[layer], ba2p[layer:layer + 1], gnw[layer:layer + 1],
                                         lb_all[layer:layer + 1], hg_norm_w[layer:layer + 1], cos, sin, nb, seq)
        xf, h2, tw, lo, tab, cnt = _outproj_router_call(
            xf, o_gla, o_hg, o_ret, wo_p, layer, g1, norm2_w[layer], sc2, sh2, wr_p[layer],
            br_p[layer:layer + 1], seq)

        counts = cnt[0, :N_EXPERTS].astype(jnp.int32)
        padded = (counts + MOE_BLOCK - 1) // MOE_BLOCK * MOE_BLOCK
        pends = jnp.cumsum(padded)
        pstarts = pends - padded
        block_start = jnp.arange(n_blocks, dtype=jnp.int32) * MOE_BLOCK
        block_e = jnp.minimum(jnp.sum((pends[None, :] <= block_start[:, None]).astype(jnp.int32), axis=1),
                              N_EXPERTS - 1)
        n_used = (pends[-1:] // MOE_BLOCK).astype(jnp.int32)
        run_at = tab[:, 2, :] + jnp.pad(pstarts, (0, LANE - N_EXPERTS))[None, :]

        ids = jnp.arange(N_EXPERTS, dtype=jnp.int32)
        later = jnp.where((ids[None, :] > ids[:, None]) & (padded[None, :] > 0), ids[None, :], N_EXPERTS)
        follower = jnp.min(later, axis=1)
        follower = jnp.where(follower == N_EXPERTS, ids, follower)
        of_block = block_e[:, None] == ids[None, :]
        next_e = jnp.sum(jnp.where(of_block, follower[None, :], 0), axis=1)
        run_end = jnp.sum(jnp.where(of_block, (pstarts + counts)[None, :], 0), axis=1)
        block_used = jnp.clip(run_end - block_start, 0, MOE_BLOCK).astype(jnp.int32)

        buf = _dispatch_call(h2, tab, run_at, lo, buf, seq)
        yb = _expert_call(buf, block_e, next_e, block_used, n_used, w_gu, b_gu, w_dn, b_dn, layer)
        xf = _combine_call(yb, tab, run_at, lo, tw, xf, g2, seq)

    return _final_norm_call(xf, final_norm_w).reshape(nb, seq, d)
```

```python
import functools

import numpy as np
import jax
import jax.numpy as jnp
from jax import lax
from jax.experimental import pallas as pl
from jax.experimental.pallas import tpu as pltpu

F32 = jnp.float32
BF16 = jnp.bfloat16

N_HEADS = 4
GLA_DK, GLA_DV, GLA_RANK, GLA_TAU = 48, 96, 16, 16.0
HG_DK, HG_DV = 64, 64
RET_DK, RET_DV = 48, 96
ROPE_BASE = 10000.0
N_EXPERTS, TOP_K = 32, 4
SWIGLU_LIMIT, SWIGLU_ALPHA = 7.0, 1.702
MOE_BLOCK = 512
ROUTE_TILE = 512
RUN_ALIGN = 8
EPS = 1e-6

LANE = 128
HALF = LANE // 2
N_PAIRS = N_HEADS // 2
HEAD_W = N_HEADS * LANE
PAIR_W = N_PAIRS * LANE
VMEM_LIMIT = 56 * 1024 * 1024
NEG_BIG = -1e30


def _cparams(sem):
    return pltpu.CompilerParams(dimension_semantics=sem, vmem_limit_bytes=VMEM_LIMIT)


def _dot(a, b):
    return jnp.dot(a, b, preferred_element_type=F32)


def _dot_nt(a, b):
    return lax.dot_general(a, b, (((1,), (1,)), ((), ())), preferred_element_type=F32)


def _dot_tn(a, b):
    return lax.dot_general(a, b, (((0,), (0,)), ((), ())), preferred_element_type=F32)


def _split2(a):
    hi = a.astype(BF16)
    return hi, (a - hi.astype(F32)).astype(BF16)


def _dot_f32(a, b):
    a_hi, a_lo = _split2(a)
    b_hi, b_lo = _split2(b)
    return _dot(a_hi, b_hi) + _dot(a_hi, b_lo) + _dot(a_lo, b_hi)


def _sigmoid(x):
    return 1.0 / (1.0 + jnp.exp(-x))


def _log_sigmoid(x):
    return jnp.minimum(x, 0.0) - jnp.log1p(jnp.exp(-jnp.abs(x)))


def _mod_kernel(c_ref, w_ref, b_ref, o_ref):
    c = c_ref[...]
    o_ref[0] = _dot_f32(c * _sigmoid(c), w_ref[0]) + b_ref[0]


def _mod_call(c, w_mod, b_mod):
    depth, d, d6 = w_mod.shape
    nb = c.shape[0]
    rows = 8
    c_pad = jnp.zeros((rows, d), F32).at[:nb].set(c)
    out = pl.pallas_call(
        _mod_kernel,
        grid=(depth, d6 // d),
        in_specs=[
            pl.BlockSpec((rows, d), lambda l, j: (0, 0)),
            pl.BlockSpec((1, d, d), lambda l, j: (l, 0, j)),
            pl.BlockSpec((1, 1, d), lambda l, j: (l, 0, j)),
        ],
        out_specs=pl.BlockSpec((1, rows, d), lambda l, j: (l, 0, j)),
        out_shape=jax.ShapeDtypeStruct((depth, rows, d6), F32),
        compiler_params=_cparams(("parallel", "parallel")),
        name="mod",
    )(c_pad, w_mod, b_mod.reshape(depth, 1, d6))
    return out[:, :nb]


def _rope_kernel(pos_ref, freq_ref, cos_ref, sin_ref):
    ang = pos_ref[0] * freq_ref[...]
    cos_ref[0] = jnp.cos(ang)
    sin_ref[0] = jnp.sin(ang)


def _rope_call(positions):
    nb, l = positions.shape
    half = RET_DK // 2
    inv = (ROPE_BASE ** (-np.arange(half, dtype=np.float32) / half)).astype(np.float32)
    freq = np.zeros((1, LANE), np.float32)
    for base in (0, HALF):
        freq[0, base:base + half] = inv
        freq[0, base + half:base + 2 * half] = inv
    tl = min(l, 512)
    pos = positions.astype(F32).reshape(nb, l, 1)
    shp = jax.ShapeDtypeStruct((nb, l, LANE), F32)
    return pl.pallas_call(
        _rope_kernel,
        grid=(nb, l // tl),
        in_specs=[pl.BlockSpec((1, tl, 1), lambda b, i: (b, i, 0)),
                  pl.BlockSpec((1, LANE), lambda b, i: (0, 0))],
        out_specs=[pl.BlockSpec((1, tl, LANE), lambda b, i: (b, i, 0))] * 2,
        out_shape=[shp, shp],
        compiler_params=_cparams(("parallel", "parallel")),
        name="rope_tables",
    )(pos, jnp.asarray(freq))


def _lb_kernel(p_ref, o_ref):
    p = p_ref[...]
    depth = p.shape[0]
    m = jnp.max(p, axis=0, keepdims=True)
    e = jnp.exp(p - m)
    sm = e / jnp.sum(e, axis=0, keepdims=True)
    acc = jnp.zeros_like(sm[0:1])
    for i in range(depth):
        if i > 0:
            acc = acc + sm[i:i + 1]
        o_ref[i:i + 1, :] = acc


def _lb_call(hg_lb):
    return pl.pallas_call(
        _lb_kernel,
        out_shape=jax.ShapeDtypeStruct(hg_lb.shape, F32),
        name="hg_lower_bounds",
    )(hg_lb.astype(F32))


def _inproj_kernel(x_ref, nw_ref, sc_ref, sh_ref, w_ref, *out_refs, widths):
    x = x_ref[...]
    y = x * lax.rsqrt(jnp.mean(x * x, axis=-1, keepdims=True) + EPS) * nw_ref[...]
    hb = (y * (1.0 + sc_ref[0]) + sh_ref[0]).astype(BF16)
    off = 0
    for o_ref, wd in zip(out_refs, widths):
        o_ref[...] = _dot(hb, w_ref[:, off:off + wd]).astype(o_ref.dtype)
        off += wd


def _inproj_call(x, norm_w, sc, sh, w_cat, layer, widths, dtypes, seq):
    t, d = x.shape
    tm = min(512, seq)
    per_b = seq // tm
    return pl.pallas_call(
        functools.partial(_inproj_kernel, widths=widths),
        grid=(t // tm,),
        in_specs=[
            pl.BlockSpec((tm, d), lambda i: (i, 0)),
            pl.BlockSpec((1, d), lambda i: (0, 0)),
            pl.BlockSpec((1, 1, d), lambda i: (i // per_b, 0, 0)),
            pl.BlockSpec((1, 1, d), lambda i: (i // per_b, 0, 0)),
            pl.BlockSpec((None,) + w_cat.shape[1:], lambda i: (layer, 0, 0)),
        ],
        out_specs=[pl.BlockSpec((tm, wd), lambda i: (i, 0)) for wd in widths],
        out_shape=[jax.ShapeDtypeStruct((t, wd), dt) for wd, dt in zip(widths, dtypes)],
        compiler_params=_cparams(("parallel",)),
        name="norm1_inproj",
    )(x, norm_w.reshape(1, d), sc, sh, w_cat)


def _level_tables(c):
    nl = int(np.log2(c))
    assert 1 << nl == c
    idx = np.arange(c)
    mats, masks = [], [np.eye(c, dtype=np.float32)]
    t = idx[None, :]
    i = idx[:, None]
    for lvl in range(nl):
        h = c >> (lvl + 1)
        blk, pos = idx // (2 * h), idx % (2 * h)
        m = (blk * 2 * h + h - 1)[:, None]
        right = (pos >= h)[:, None]
        a = np.where(right, (t > m) & (t <= i), (t > i) & (t <= m))
        mats.append(a.astype(np.float32))
        same = blk[:, None] == blk[None, :]
        masks.append((same & right & (pos < h)[None, :]).astype(np.float32))
    mats.append((t <= i).astype(np.float32))
    mats.append((t > i).astype(np.float32))
    return np.concatenate(mats, 0), np.stack(masks, 0), nl


def _pair_decays(g, mall_ref):
    gs = _dot(mall_ref[...], jnp.concatenate(_split2(g), axis=1))
    return jnp.exp(gs[:, :LANE] + gs[:, LANE:])


def _pair_keys(k, e, c, nl):
    levels = [(k * e[lvl * c:(lvl + 1) * c]).astype(BF16) for lvl in range(nl)]
    return k.astype(BF16), levels, (k * e[(nl + 1) * c:(nl + 2) * c]).astype(BF16)


def _own_half(pair_tile, hd, low):
    return jnp.where(low, pair_tile, 0.0) if hd % 2 == 0 else jnp.where(low, 0.0, pair_tile)


def _gated_head(q, kb, klv, e, v, st, masks_ref, c, nl):
    scores = masks_ref[0] * _dot_nt(q.astype(BF16), kb)
    for lvl in range(nl):
        scores = scores + masks_ref[lvl + 1] * _dot_nt((q * e[lvl * c:(lvl + 1) * c]).astype(BF16), klv[lvl])
    q_in = (q * e[nl * c:(nl + 1) * c]).astype(BF16)
    return _dot(scores.astype(BF16), v) + _dot_nt(q_in, st.astype(BF16))


def _gla_kernel(q_ref, k_ref, v_ref, gg_ref, ga_ref, wa2_ref, ba2_ref, nw_ref, mall_ref, masks_ref,
                o_ref, st_ref, *, c, nl):
    low = lax.broadcasted_iota(jnp.int32, (c, LANE), 1) < HALF
    for b in range(q_ref.shape[0]):
        log_alpha = _log_sigmoid(_dot_f32(ga_ref[b], wa2_ref[...]) + ba2_ref[...]) * (1.0 / GLA_TAU)
        for pr in range(N_PAIRS):
            psl = slice(pr * LANE, (pr + 1) * LANE)
            e = _pair_decays(log_alpha[:, psl], mall_ref)
            kb, klv, k_end = _pair_keys(k_ref[b, :, psl].astype(F32), e, c, nl)
            dec = e[(nl + 1) * c - 1:(nl + 1) * c, :]
            qp = q_ref[b, :, psl].astype(F32) * (GLA_DK ** -0.5)
            for hd in (2 * pr, 2 * pr + 1):
                sl = slice(hd * LANE, (hd + 1) * LANE)
                q = _own_half(qp, hd, low)
                v = v_ref[b, :, sl]
                st = st_ref[b, hd]
                o = _gated_head(q, kb, klv, e, v, st, masks_ref, c, nl)
                st_ref[b, hd] = dec * st + _dot_tn(v, k_end)
                y = o * lax.rsqrt(jnp.sum(o * o, axis=-1, keepdims=True) * (1.0 / GLA_DV) + EPS) * nw_ref[:, sl]
                gate = gg_ref[b, :, sl].astype(F32)
                o_ref[b, :, sl] = (y * gate * _sigmoid(gate)).astype(o_ref.dtype)


def _hg_kernel(q_ref, f_ref, v_ref, og_ref, lb_ref, nw_ref, mall_ref, masks_ref,
               o_ref, st_ref, *, c, nl):
    low = lax.broadcasted_iota(jnp.int32, (c, LANE), 1) < HALF
    for b in range(q_ref.shape[0]):
        for pr in range(N_PAIRS):
            psl = slice(pr * LANE, (pr + 1) * LANE)
            lb = lb_ref[:, psl]
            hf = f_ref[b, :, psl]
            la = jnp.log(lb)
            lc = jnp.log1p(-lb) + _log_sigmoid(hf)
            log_f = jnp.maximum(la, lc) + jnp.log1p(jnp.exp(-jnp.abs(la - lc)))
            e = _pair_decays(log_f, mall_ref)
            kb, klv, k_end = _pair_keys((1.0 - lb) * _sigmoid(-hf), e, c, nl)
            v = v_ref[b, :, psl]
            st = st_ref[b, pr]
            qp = q_ref[b, :, psl].astype(F32)
            o_even = _gated_head(_own_half(qp, 0, low), kb, klv, e, v, st, masks_ref, c, nl)
            o_odd = _gated_head(_own_half(qp, 1, low), kb, klv, e, v, st, masks_ref, c, nl)
            o = jnp.where(low, o_even, o_odd)
            st_ref[b, pr] = e[(nl + 1) * c - 1:(nl + 1) * c, :] * st + _dot_tn(v, k_end)
            sq = o * o
            ms = jnp.where(low, jnp.sum(jnp.where(low, sq, 0.0), axis=-1, keepdims=True),
                           jnp.sum(jnp.where(low, 0.0, sq), axis=-1, keepdims=True)) * (1.0 / HG_DV)
            y = o * lax.rsqrt(ms + EPS) * nw_ref[:, psl]
            o_ref[b, :, psl] = (y * _sigmoid(og_ref[b, :, psl].astype(F32))).astype(o_ref.dtype)


def _tok_spec(nb, c, width):
    return pl.BlockSpec((nb, c, width), lambda i: (0, i, 0))


def _const_spec(shape):
    nd = len(shape)
    return pl.BlockSpec(shape, lambda i: (0,) * nd)


def _seq_view(a, nb, seq):
    return a.reshape(nb, seq, a.shape[-1])


def _ret_tables(c):
    hs = np.arange(N_HEADS, dtype=np.float64)
    log_gamma = np.log(1.0 - np.exp2(-5.0 - hs))
    idx = np.arange(c, dtype=np.float64)
    rel = idx[:, None] - idx[None, :]
    dmat = np.where(rel >= 0, np.exp(log_gamma[:, None, None] * np.maximum(rel, 0.0)), 0.0)
    qdec = np.exp(log_gamma[:, None] * (idx + 1.0))
    kdec = np.exp(log_gamma[:, None] * (c - 1.0 - idx))
    cdec = np.exp(log_gamma * c)
    qfull = np.broadcast_to(qdec[:, :, None], (N_HEADS, c, LANE)).astype(np.float32)
    kpair = np.zeros((N_PAIRS, c, LANE), np.float32)
    for hd in range(N_HEADS):
        kpair[hd // 2, :, (hd % 2) * HALF:(hd % 2 + 1) * HALF] = kdec[hd][:, None]
    return dmat.astype(np.float32), qfull, kpair, [float(np.float32(v)) for v in cdec]


def _ret_kernel(q_ref, k_ref, v_ref, og_ref, cos_ref, sin_ref, dmat_ref, qdec_ref, kdec_ref,
                o_ref, st_ref, *, c, cdec):
    half = RET_DK // 2
    lane = lax.broadcasted_iota(jnp.int32, (c, LANE), 1)
    first = (lane & (HALF - 1)) < half
    low = lane < HALF

    for b in range(q_ref.shape[0]):
        cos = cos_ref[b]
        sin = sin_ref[b]

        def rotary(t):
            rot = jnp.where(first, -pltpu.roll(t, LANE - half, 1), pltpu.roll(t, half, 1))
            return t * cos + rot * sin

        for pr in range(N_PAIRS):
            psl = slice(pr * LANE, (pr + 1) * LANE)
            k = rotary(k_ref[b, :, psl].astype(F32))
            kb = k.astype(BF16)
            k_end = (k * kdec_ref[pr]).astype(BF16)
            qp = rotary(q_ref[b, :, psl].astype(F32)) * (RET_DK ** -0.5)
            for hd in (2 * pr, 2 * pr + 1):
                sl = slice(hd * LANE, (hd + 1) * LANE)
                q = _own_half(qp, hd, low)
                v = v_ref[b, :, sl]
                st = st_ref[b, hd]
                scores = _dot_nt(q.astype(BF16), kb) * dmat_ref[hd]
                o = _dot(scores.astype(BF16), v) + _dot_nt((q * qdec_ref[hd]).astype(BF16), st.astype(BF16))
                st_ref[b, hd] = cdec[hd] * st + _dot_tn(v, k_end)
                mu = jnp.sum(o, axis=-1, keepdims=True) * (1.0 / RET_DV)
                dlt = jnp.where(lane < RET_DV, o - mu, 0.0)
                var = jnp.sum(dlt * dlt, axis=-1, keepdims=True) * (1.0 / RET_DV)
                gate = og_ref[b, :, sl].astype(F32)
                o_ref[b, :, sl] = (dlt * lax.rsqrt(var + EPS) * gate * _sigmoid(gate)).astype(o_ref.dtype)


def _mixer_kernel(gq, gk, gv, gg, ga, wa2, ba2, gnw, hq, hf, hi, hg, lb, hnw, rq, rk, rv, rg, cos, sin,
                  mall, masks, dmat, qdec, kdec, o_gla, o_hg, o_ret, st_gla, st_hg, st_ret, *, c, nl, cdec):
    @pl.when(pl.program_id(0) == 0)
    def _():
        st_gla[...] = jnp.zeros_like(st_gla)
        st_hg[...] = jnp.zeros_like(st_hg)
        st_ret[...] = jnp.zeros_like(st_ret)

    _gla_kernel(gq, gk, gv, gg, ga, wa2, ba2, gnw, mall, masks, o_gla, st_gla, c=c, nl=nl)
    _hg_kernel(hq, hf, hi, hg, lb, hnw, mall, masks, o_hg, st_hg, c=c, nl=nl)
    _ret_kernel(rq, rk, rv, rg, cos, sin, dmat, qdec, kdec, o_ret, st_ret, c=c, cdec=cdec)


def _mixer_call(parts, wa2p, ba2p, gnw, lbp, hnw, cos, sin, nb, seq):
    gq, gk, gv, gg, ga, hq, hf, hi, hg, rq, rk, rv, rg = parts
    c = min(128, seq)
    n = seq // c
    mall, masks, nl = _level_tables(c)
    dmat, qdec, kdec, cdec = _ret_tables(c)
    consts = [wa2p, ba2p, gnw]
    sv = lambda a: _seq_view(a, nb, seq)
    tok = lambda a: _tok_spec(nb, c, a.shape[-1])
    const = lambda a: _const_spec(a.shape)
    tables = [jnp.asarray(mall, BF16), jnp.asarray(masks), jnp.asarray(dmat), jnp.asarray(qdec),
              jnp.asarray(kdec)]
    o_gla, o_hg, o_ret = pl.pallas_call(
        functools.partial(_mixer_kernel, c=c, nl=nl, cdec=cdec),
        grid=(n,),
        in_specs=([tok(a) for a in (gq, gk, gv, gg, ga)] + [const(a) for a in consts]
                  + [tok(a) for a in (hq, hf, hi, hg)] + [const(lbp), const(hnw)]
                  + [tok(a) for a in (rq, rk, rv, rg, cos, sin)] + [const(a) for a in tables]),
        out_specs=[_tok_spec(nb, c, HEAD_W), _tok_spec(nb, c, PAIR_W), _tok_spec(nb, c, HEAD_W)],
        out_shape=[jax.ShapeDtypeStruct((nb, seq, HEAD_W), BF16), jax.ShapeDtypeStruct((nb, seq, PAIR_W), BF16),
                   jax.ShapeDtypeStruct((nb, seq, HEAD_W), BF16)],
        scratch_shapes=[pltpu.VMEM((nb, N_HEADS, LANE, LANE), F32), pltpu.VMEM((nb, N_PAIRS, LANE, LANE), F32),
                        pltpu.VMEM((nb, N_HEADS, LANE, LANE), F32)],
        compiler_params=_cparams(("arbitrary",)),
        name="token_mixers",
    )(sv(gq), sv(gk), sv(gv), sv(gg), sv(ga), wa2p, ba2p, gnw, sv(hq), sv(hf), sv(hi), sv(hg), lbp, hnw,
      sv(rq), sv(rk), sv(rv), sv(rg), cos, sin, *tables)
    t = nb * seq
    return o_gla.reshape(t, HEAD_W), o_hg.reshape(t, PAIR_W), o_ret.reshape(t, HEAD_W)


def _outproj_router_kernel(x_ref, oa_ref, ob_ref, oc_ref, wo_ref, g1_ref, nw_ref, sc_ref, sh_ref,
                           wr_ref, br_ref, tri_ref, upper_ref,
                           xo_ref, h_ref, tw_ref, lo_ref, tab_ref, cnt_ref, *, tm):
    @pl.when(pl.program_id(0) == 0)
    def _():
        cnt_ref[...] = jnp.zeros_like(cnt_ref)

    wa, wb = oa_ref.shape[1], ob_ref.shape[1]
    mix = (_dot(oa_ref[...], wo_ref[0:wa, :]) + _dot(ob_ref[...], wo_ref[wa:wa + wb, :])
           + _dot(oc_ref[...], wo_ref[wa + wb:, :]))
    x = x_ref[...] + g1_ref[0] * mix
    xo_ref[...] = x
    y = x * lax.rsqrt(jnp.mean(x * x, axis=-1, keepdims=True) + EPS) * nw_ref[...]
    h = y * (1.0 + sc_ref[0]) + sh_ref[0]
    h_ref[...] = h.astype(h_ref.dtype)

    lg = _dot_f32(h, wr_ref[...]) + br_ref[...]
    lane = lax.broadcasted_iota(jnp.int32, (tm, LANE), 1)
    sel_e, sel_v = [], []
    for _ in range(TOP_K):
        m = jnp.max(lg, axis=-1, keepdims=True)
        idx = jnp.min(jnp.where(lg == m, lane, LANE), axis=-1, keepdims=True)
        sel_e.append(idx)
        sel_v.append(m)
        lg = jnp.where(lane == idx, -jnp.inf, lg)
    ex = [jnp.exp(v - sel_v[0]) for v in sel_v]
    den = ex[0] + ex[1] + ex[2] + ex[3]
    hot = [(lane == idx) for idx in sel_e]
    onehot = jnp.zeros((tm, LANE), F32)
    for hk in hot:
        onehot = onehot + jnp.where(hk, 1.0, 0.0)
    in_tile = _dot(tri_ref[...], onehot.astype(BF16))
    earlier = cnt_ref[...]
    tile_cnt = jnp.floor((jnp.sum(onehot, axis=0, keepdims=True) + (RUN_ALIGN - 1.0)) * (1.0 / RUN_ALIGN)) * RUN_ALIGN
    tile_start = _dot_f32(jnp.broadcast_to(tile_cnt, (8, LANE)), upper_ref[...])[0:1]
    local = tile_start + in_tile
    tw = jnp.zeros((tm, LANE), F32)
    lo = jnp.zeros((tm, LANE), jnp.int32)
    for kk in range(TOP_K):
        lrow = jnp.sum(jnp.where(hot[kk], local, 0.0), axis=-1, keepdims=True).astype(jnp.int32)
        tw = jnp.where(lane == kk, ex[kk] / den, tw)
        lo = jnp.where(lane == kk, lrow, lo)
    tw_ref[...] = tw
    lo_ref[...] = lo
    tab_ref[0, 0:1, :] = tile_start.astype(jnp.int32)
    tab_ref[0, 1:2, :] = tile_cnt.astype(jnp.int32)
    tab_ref[0, 2:3, :] = earlier.astype(jnp.int32)
    tab_ref[0, 3:4, :] = jnp.broadcast_to(jnp.sum(tile_cnt, axis=-1, keepdims=True), (1, LANE)).astype(jnp.int32)
    tab_ref[0, 4:8, :] = jnp.zeros((4, LANE), jnp.int32)
    cnt_ref[...] = earlier + tile_cnt


def _outproj_router_call(x, oa, ob, oc, wo, layer, g1, nw, sc, sh, wr, br, seq):
    t, d = x.shape
    tm = min(ROUTE_TILE, seq)
    per_b = seq // tm
    tri = np.tril(np.ones((tm, tm), np.float32), -1)
    upper = np.triu(np.ones((LANE, LANE), np.float32), 1)
    row = lambda w: pl.BlockSpec((tm, w), lambda i: (i, 0))
    const = lambda shape: pl.BlockSpec(shape, lambda i: (0,) * len(shape))
    perb = pl.BlockSpec((1, 1, d), lambda i: (i // per_b, 0, 0))
    n_tiles = t // tm
    return pl.pallas_call(
        functools.partial(_outproj_router_kernel, tm=tm),
        grid=(n_tiles,),
        in_specs=[row(d), row(oa.shape[1]), row(ob.shape[1]), row(oc.shape[1]),
                  pl.BlockSpec((None,) + wo.shape[1:], lambda i: (layer, 0, 0)), perb,
                  const((1, d)), perb, perb, const(wr.shape), const(br.shape), const(tri.shape),
                  const(upper.shape)],
        out_specs=[row(d), row(d), row(LANE), row(LANE),
                   pl.BlockSpec((1, 8, LANE), lambda i: (i, 0, 0)), const((1, LANE))],
        out_shape=[jax.ShapeDtypeStruct((t, d), F32), jax.ShapeDtypeStruct((t, d), BF16),
                   jax.ShapeDtypeStruct((t, LANE), F32), jax.ShapeDtypeStruct((t, LANE), jnp.int32),
                   jax.ShapeDtypeStruct((n_tiles, 8, LANE), jnp.int32), jax.ShapeDtypeStruct((1, LANE), F32)],
        compiler_params=_cparams(("arbitrary",)),
        name="outproj_norm2_router",
    )(x, oa, ob, oc, wo, g1, nw.reshape(1, d), sc, sh, wr, br, jnp.asarray(tri, BF16), jnp.asarray(upper))


def _run_copies(tab_ref, at_ref, tile, tm, make_copy, act):
    sizes = [s for s in (1 << p for p in range(tm.bit_length() - 1, -1, -1)) if s >= RUN_ALIGN]

    def per_expert(e, carry):
        first = tab_ref[tile, 0, e]
        length = tab_ref[tile, 1, e]
        at = at_ref[tile, e]
        done = jnp.int32(0)
        for size in sizes:
            part = length & size

            @pl.when(part != 0)
            def _():
                act(make_copy(pl.multiple_of(first + done, RUN_ALIGN), pl.multiple_of(at + done, RUN_ALIGN), size))
            done = done + part
        return carry

    lax.fori_loop(0, N_EXPERTS, per_expert, 0)


def _local_rows(tm):
    return TOP_K * tm + N_EXPERTS * RUN_ALIGN


def _dispatch_kernel(tab_ref, at_ref, h_ref, lo_ref, buf_in_ref, buf_ref, st_ref, sem, *, tm, n_tiles):
    del buf_in_ref
    i = pl.program_id(0)
    slot = i % 2

    def copies(tile, s, act):
        _run_copies(tab_ref, at_ref, tile, tm,
                    lambda row, at, size: pltpu.make_async_copy(st_ref.at[s, pl.ds(row, size)],
                                                                buf_ref.at[pl.ds(at, size)], sem.at[s]), act)

    def wait_tile(tile, s):
        total = pl.multiple_of(tab_ref[tile, 3, 0], RUN_ALIGN)
        pltpu.make_async_copy(st_ref.at[s, pl.ds(0, total)], buf_ref.at[pl.ds(0, total)], sem.at[s]).wait()

    @pl.when(i >= 2)
    def _():
        wait_tile(i - 2, slot)

    col = lax.broadcasted_iota(jnp.int32, (tm, st_ref.shape[1]), 1)
    lo = lo_ref[...]
    place = jnp.zeros(col.shape, F32)
    for kk in range(TOP_K):
        place = jnp.where(col == lo[:, kk:kk + 1], 1.0, place)
    st_ref[slot] = _dot_tn(place.astype(BF16), h_ref[...].astype(BF16))
    copies(i, slot, lambda cp: cp.start())

    @pl.when(i == n_tiles - 1)
    def _():
        if n_tiles >= 2:
            wait_tile(i - 1, 1 - slot)
        wait_tile(i, slot)


def _dispatch_call(h, tab, run_at, lo, buf, seq):
    t, d = h.shape
    n_slots = buf.shape[0]
    tm = min(ROUTE_TILE, seq)
    n_tiles = t // tm
    grid_spec = pltpu.PrefetchScalarGridSpec(
        num_scalar_prefetch=2,
        grid=(n_tiles,),
        in_specs=[pl.BlockSpec((tm, d), lambda i, tb, sr: (i, 0)),
                  pl.BlockSpec((tm, LANE), lambda i, tb, sr: (i, 0)),
                  pl.BlockSpec(memory_space=pl.ANY)],
        out_specs=pl.BlockSpec(memory_space=pl.ANY),
        scratch_shapes=[pltpu.VMEM((2, _local_rows(tm), d), F32), pltpu.SemaphoreType.DMA((2,))],
    )
    return pl.pallas_call(
        functools.partial(_dispatch_kernel, tm=tm, n_tiles=n_tiles),
        grid_spec=grid_spec,
        out_shape=jax.ShapeDtypeStruct((n_slots, d), F32),
        input_output_aliases={4: 0},
        compiler_params=_cparams(("arbitrary",)),
        name="moe_dispatch",
    )(tab, run_at, h, lo, buf)


def _expert_kernel(be_ref, nx_ref, used_ref, nu_ref, x_ref, wgu_ref, bgu_ref, wdn_ref, bdn_ref, o_ref,
                   wgu_f32, wdn_f32, wgu_bf, wdn_bf, sem, *, d_ff, layer):
    i = pl.program_id(0)
    live = i < nu_ref[0]
    e = be_ref[i]
    e_next = nx_ref[i]

    def weight_copies(expert):
        return (pltpu.make_async_copy(wgu_ref.at[layer, expert], wgu_f32, sem.at[0]),
                pltpu.make_async_copy(wdn_ref.at[layer, expert], wdn_f32, sem.at[1]))

    @pl.when(i == 0)
    def _():
        for cp in weight_copies(e):
            cp.start()

    @pl.when(jnp.logical_and(live, jnp.logical_or(i == 0, e != be_ref[jnp.maximum(i - 1, 0)])))
    def _():
        for cp in weight_copies(e):
            cp.wait()
        wgu_bf[...] = wgu_f32[...].astype(BF16)
        wdn_bf[...] = wdn_f32[...].astype(BF16)

        @pl.when(e_next != e)
        def _():
            for cp in weight_copies(e_next):
                cp.start()

    def ffn(rows):
        gu = _dot(x_ref[0:rows, :].astype(BF16), wgu_bf[...]) + bgu_ref[0, 0]
        gate = jnp.minimum(gu[:, :d_ff], SWIGLU_LIMIT)
        up = jnp.clip(gu[:, d_ff:], -SWIGLU_LIMIT, SWIGLU_LIMIT)
        act = (up + 1.0) * gate * _sigmoid(SWIGLU_ALPHA * gate)
        o_ref[0:rows, :] = _dot(act.astype(BF16), wdn_bf[...]) + bdn_ref[0, 0]
        if rows < MOE_BLOCK:
            o_ref[rows:, :] = jnp.zeros((MOE_BLOCK - rows, o_ref.shape[1]), o_ref.dtype)

    used = used_ref[i]
    prefixes = [MOE_BLOCK // 4, MOE_BLOCK // 2, MOE_BLOCK]
    for lo_rows, rows in zip([0] + prefixes[:-1], prefixes):
        pl.when(jnp.logical_and(live, jnp.logical_and(used > lo_rows, used <= rows)))(
            functools.partial(ffn, rows))

    @pl.when(jnp.logical_not(live))
    def _():
        o_ref[...] = jnp.zeros_like(o_ref)


def _expert_call(buf, block_e, next_e, block_used, n_used, wgu, bgu, wdn, bdn, layer):
    n_slots, d = buf.shape
    n_blocks = n_slots // MOE_BLOCK
    depth, ne, _, f2 = wgu.shape
    d_ff = f2 // 2
    blk = lambda i, be, nx, us, nu: (jnp.minimum(i, nu[0] - 1), 0)
    exp4 = lambda i, be, nx, us, nu: (layer, be[jnp.minimum(i, nu[0] - 1)], 0, 0)
    grid_spec = pltpu.PrefetchScalarGridSpec(
        num_scalar_prefetch=4,
        grid=(n_blocks,),
        in_specs=[pl.BlockSpec((MOE_BLOCK, d), blk),
                  pl.BlockSpec(memory_space=pl.ANY),
                  pl.BlockSpec((1, 1, 1, f2), exp4),
                  pl.BlockSpec(memory_space=pl.ANY),
                  pl.BlockSpec((1, 1, 1, d), exp4)],
        out_specs=pl.BlockSpec((MOE_BLOCK, d), lambda i, be, nx, us, nu: (i, 0)),
        scratch_shapes=[pltpu.VMEM((d, f2), F32), pltpu.VMEM((d_ff, d), F32),
                        pltpu.VMEM((d, f2), BF16), pltpu.VMEM((d_ff, d), BF16),
                        pltpu.SemaphoreType.DMA((2,))],
    )
    return pl.pallas_call(
        functools.partial(_expert_kernel, d_ff=d_ff, layer=layer),
        grid_spec=grid_spec,
        out_shape=jax.ShapeDtypeStruct((n_slots, d), F32),
        compiler_params=_cparams(("arbitrary",)),
        name="moe_experts",
    )(block_e, next_e, block_used, n_used, buf, wgu, bgu.reshape(depth, ne, 1, f2), wdn, bdn.reshape(depth, ne, 1, d))


def _combine_kernel(tab_ref, at_ref, yb_ref, lo_ref, w_ref, x_ref, g2_ref, o_ref, rs_ref, sem, *, tm):
    i = pl.program_id(0)
    n = pl.num_programs(0)
    rows = rs_ref.shape[1]

    def run_copies(tile, s, act):
        _run_copies(tab_ref, at_ref, tile, tm,
                    lambda row, at, size: pltpu.make_async_copy(yb_ref.at[pl.ds(at, size)],
                                                                rs_ref.at[s, pl.ds(row, size)], sem.at[s]), act)

    @pl.when(i == 0)
    def _():
        rs_ref[...] = jnp.zeros_like(rs_ref)
        run_copies(0, 0, lambda cp: cp.start())

    @pl.when(i + 1 < n)
    def _():
        run_copies(i + 1, (i + 1) % 2, lambda cp: cp.start())

    slot = i % 2
    total = pl.multiple_of(tab_ref[i, 3, 0], RUN_ALIGN)
    pltpu.make_async_copy(yb_ref.at[pl.ds(0, total)], rs_ref.at[slot, pl.ds(0, total)], sem.at[slot]).wait()
    r = rs_ref[slot].astype(BF16)
    col = lax.broadcasted_iota(jnp.int32, (tm, rows), 1)
    w = w_ref[...]
    lo = lo_ref[...]
    pw = jnp.zeros((tm, rows), F32)
    for kk in range(TOP_K):
        pw = jnp.where(col == lo[:, kk:kk + 1], w[:, kk:kk + 1], pw)
    y = _dot(pw.astype(BF16), r)
    o_ref[...] = x_ref[...] + g2_ref[0] * y


def _combine_call(yb, tab, src, lo, tw, x, g2, seq):
    t, d = x.shape
    tm = min(ROUTE_TILE, seq)
    per_b = seq // tm
    grid_spec = pltpu.PrefetchScalarGridSpec(
        num_scalar_prefetch=2,
        grid=(t // tm,),
        in_specs=[pl.BlockSpec(memory_space=pl.ANY),
                  pl.BlockSpec((tm, LANE), lambda i, tb, sr: (i, 0)),
                  pl.BlockSpec((tm, LANE), lambda i, tb, sr: (i, 0)),
                  pl.BlockSpec((tm, d), lambda i, tb, sr: (i, 0)),
                  pl.BlockSpec((1, 1, d), lambda i, tb, sr: (i // per_b, 0, 0))],
        out_specs=pl.BlockSpec((tm, d), lambda i, tb, sr: (i, 0)),
        scratch_shapes=[pltpu.VMEM((2, _local_rows(tm), d), F32), pltpu.SemaphoreType.DMA((2,))],
    )
    return pl.pallas_call(
        functools.partial(_combine_kernel, tm=tm),
        grid_spec=grid_spec,
        out_shape=jax.ShapeDtypeStruct((t, d), F32),
        compiler_params=_cparams(("arbitrary",)),
        name="moe_combine",
    )(tab, src, yb, lo, tw, x, g2)


def _final_norm_kernel(x_ref, w_ref, o_ref):
    x = x_ref[...]
    o_ref[...] = x * lax.rsqrt(jnp.mean(x * x, axis=-1, keepdims=True) + EPS) * w_ref[...]


def _final_norm_call(x, w):
    t, d = x.shape
    tm = min(512, t)
    return pl.pallas_call(
        _final_norm_kernel,
        grid=(t // tm,),
        in_specs=[pl.BlockSpec((tm, d), lambda i: (i, 0)), pl.BlockSpec((1, d), lambda i: (0, 0))],
        out_specs=pl.BlockSpec((tm, d), lambda i: (i, 0)),
        out_shape=jax.ShapeDtypeStruct((t, d), F32),
        compiler_params=_cparams(("parallel",)),
        name="final_norm",
    )(x, w.reshape(1, d))


def _split_heads(w, hd):
    return w.reshape(w.shape[:-1] + (N_HEADS, hd))


def _pad_last(w, lo, hi):
    return jnp.pad(w, [(0, 0)] * (w.ndim - 1) + [(lo, hi)])


def _tile_cols(w, hd):
    return _pad_last(_split_heads(w, hd), 0, LANE - hd).reshape(w.shape[:-1] + (HEAD_W,))


def _pair_cols(w, hd):
    return _pad_last(_split_heads(w, hd), 0, HALF - hd).reshape(w.shape[:-1] + (PAIR_W,))


def _tile_rows(w, hd):
    lead, cols = w.shape[:-2], w.shape[-1]
    w = w.reshape(lead + (N_HEADS, hd, cols))
    w = jnp.pad(w, [(0, 0)] * (w.ndim - 2) + [(0, LANE - hd), (0, 0)])
    return w.reshape(lead + (HEAD_W, cols))


IN_PARTS = (("gq", GLA_DK, _pair_cols), ("gk", GLA_DK, _pair_cols), ("gv", GLA_DV, _tile_cols),
            ("gg", GLA_DV, _tile_cols), ("ga", None, None),
            ("hq", HG_DK, _pair_cols), ("hf", HG_DK, _pair_cols), ("hi", HG_DV, _pair_cols),
            ("hg", HG_DV, _pair_cols),
            ("rq", RET_DK, _pair_cols), ("rk", RET_DK, _pair_cols), ("rv", RET_DV, _tile_cols),
            ("rg", RET_DV, _tile_cols))
F32_PARTS = ("ga", "hf")


def _layout_w_in(w_in):
    cols, off = [], 0
    for _, hd, layout in IN_PARTS:
        if layout is None:
            part = _pad_last(w_in[..., off:off + GLA_RANK], 0, LANE - GLA_RANK)
            off += GLA_RANK
        else:
            part = layout(w_in[..., off:off + N_HEADS * hd], hd)
            off += N_HEADS * hd
        cols.append(part)
    return jnp.concatenate(cols, axis=-1).astype(BF16), tuple(int(p.shape[-1]) for p in cols)


def kernel(x, c, positions, w_mod, b_mod, norm1_w, w_in, gla_wa2, gla_ba2, hg_lb, gla_norm_w, hg_norm_w,
           w_out, norm2_w, w_r, b_r, w_gu, b_gu, w_dn, b_dn, final_norm_w):
    nb, seq, d = x.shape
    t = nb * seq
    depth = w_mod.shape[0]
    dtypes = tuple(F32 if name in F32_PARTS else BF16 for name, _, _ in IN_PARTS)

    mod = _mod_call(c, w_mod, b_mod)
    cos, sin = _rope_call(positions)
    lb_all = _lb_call(hg_lb)

    n_assign = t * TOP_K
    n_run_pad = N_EXPERTS * (RUN_ALIGN - 1) * (t // min(ROUTE_TILE, seq))
    n_blocks = (n_assign + n_run_pad + MOE_BLOCK - 1) // MOE_BLOCK + N_EXPERTS
    n_slots = n_blocks * MOE_BLOCK

    w_cat, widths = _layout_w_in(w_in)
    n_gla, n_hg = N_HEADS * GLA_DV, N_HEADS * HG_DV
    wo_p = jnp.concatenate([_tile_rows(w_out[:, :n_gla], GLA_DV), w_out[:, n_gla:n_gla + n_hg],
                            _tile_rows(w_out[:, n_gla + n_hg:], RET_DV)], axis=1).astype(BF16)
    wa2p = jnp.pad(_pair_cols(gla_wa2, GLA_DK), ((0, 0), (0, LANE - GLA_RANK), (0, 0)))
    ba2p = _pair_cols(gla_ba2, GLA_DK)
    gnw = _tile_cols(gla_norm_w, GLA_DV)
    wr_p = _pad_last(w_r, 0, LANE - N_EXPERTS)
    br_p = jnp.pad(b_r, ((0, 0), (0, LANE - N_EXPERTS)), constant_values=NEG_BIG)

    buf = jnp.zeros((n_slots, d), F32)
    xf = x.reshape(t, d)
    for layer in range(depth):
        sh1, sc1, g1, sh2, sc2, g2 = [m.reshape(nb, 1, d) for m in jnp.split(mod[layer], 6, axis=-1)]

        parts = _inproj_call(xf, norm1_w[layer], sc1, sh1, w_cat, layer, widths, dtypes, seq)
        o_gla, o_hg, o_ret = _mixer_call(parts, wa2p[layer], ba2p[layer:layer + 1], gnw[layer:layer + 1],
                                         lb_all[layer:layer + 1], hg_norm_w[layer:layer + 1], cos, sin, nb, seq)
        xf, h2, tw, lo, tab, cnt = _outproj_router_call(
            xf, o_gla, o_hg, o_ret, wo_p, layer, g1, norm2_w[layer], sc2, sh2, wr_p[layer],
            br_p[layer:layer + 1], seq)

        counts = cnt[0, :N_EXPERTS].astype(jnp.int32)
        padded = (counts + MOE_BLOCK - 1) // MOE_BLOCK * MOE_BLOCK
        pends = jnp.cumsum(padded)
        pstarts = pends - padded
        block_start = jnp.arange(n_blocks, dtype=jnp.int32) * MOE_BLOCK
        block_e = jnp.minimum(jnp.sum((pends[None, :] <= block_start[:, None]).astype(jnp.int32), axis=1),
                              N_EXPERTS - 1)
        n_used = (pends[-1:] // MOE_BLOCK).astype(jnp.int32)
        run_at = tab[:, 2, :] + jnp.pad(pstarts, (0, LANE - N_EXPERTS))[None, :]

        ids = jnp.arange(N_EXPERTS, dtype=jnp.int32)
        later = jnp.where((ids[None, :] > ids[:, None]) & (padded[None, :] > 0), ids[None, :], N_EXPERTS)
        follower = jnp.min(later, axis=1)
        follower = jnp.where(follower == N_EXPERTS, ids, follower)
        of_block = block_e[:, None] == ids[None, :]
        next_e = jnp.sum(jnp.where(of_block, follower[None, :], 0), axis=1)
        run_end = jnp.sum(jnp.where(of_block, (pstarts + counts)[None, :], 0), axis=1)
        block_used = jnp.clip(run_end - block_start, 0, MOE_BLOCK).astype(jnp.int32)

        buf = _dispatch_call(h2, tab, run_at, lo, buf, seq)
        yb = _expert_call(buf, block_e, next_e, block_used, n_used, w_gu, b_gu, w_dn, b_dn, layer)
        xf = _combine_call(yb, tab, run_at, lo, tw, xf, g2, seq)

    return _final_norm_call(xf, final_norm_w).reshape(nb, seq, d)
```

```python
import functools

import numpy as np
import jax
import jax.numpy as jnp
from jax import lax
from jax.experimental import pallas as pl
from jax.experimental.pallas import tpu as pltpu

F32 = jnp.float32
BF16 = jnp.bfloat16

N_HEADS = 4
GLA_DK, GLA_DV, GLA_RANK, GLA_TAU = 48, 96, 16, 16.0
HG_DK, HG_DV = 64, 64
RET_DK, RET_DV = 48, 96
ROPE_BASE = 10000.0
N_EXPERTS, TOP_K = 32, 4
SWIGLU_LIMIT, SWIGLU_ALPHA = 7.0, 1.702
MOE_BLOCK = 512
ROUTE_TILE = 512
RUN_ALIGN = 8
EPS = 1e-6

LANE = 128
HALF = LANE // 2
N_PAIRS = N_HEADS // 2
HEAD_W = N_HEADS * LANE
PAIR_W = N_PAIRS * LANE
VMEM_LIMIT = 56 * 1024 * 1024
NEG_BIG = -1e30


def _cparams(sem):
    return pltpu.CompilerParams(dimension_semantics=sem, vmem_limit_bytes=VMEM_LIMIT)


def _dot(a, b):
    return jnp.dot(a, b, preferred_element_type=F32)


def _dot_nt(a, b):
    return lax.dot_general(a, b, (((1,), (1,)), ((), ())), preferred_element_type=F32)


def _dot_tn(a, b):
    return lax.dot_general(a, b, (((0,), (0,)), ((), ())), preferred_element_type=F32)


def _split2(a):
    hi = a.astype(BF16)
    return hi, (a - hi.astype(F32)).astype(BF16)


def _dot_f32(a, b):
    a_hi, a_lo = _split2(a)
    b_hi, b_lo = _split2(b)
    return _dot(a_hi, b_hi) + _dot(a_hi, b_lo) + _dot(a_lo, b_hi)


def _sigmoid(x):
    return 1.0 / (1.0 + jnp.exp(-x))


def _log_sigmoid(x):
    return jnp.minimum(x, 0.0) - jnp.log1p(jnp.exp(-jnp.abs(x)))


def _mod_kernel(c_ref, w_ref, b_ref, o_ref):
    c = c_ref[...]
    o_ref[0] = _dot_f32(c * _sigmoid(c), w_ref[0]) + b_ref[0]


def _mod_call(c, w_mod, b_mod):
    depth, d, d6 = w_mod.shape
    nb = c.shape[0]
    rows = 8
    c_pad = jnp.zeros((rows, d), F32).at[:nb].set(c)
    out = pl.pallas_call(
        _mod_kernel,
        grid=(depth, d6 // d),
        in_specs=[
            pl.BlockSpec((rows, d), lambda l, j: (0, 0)),
            pl.BlockSpec((1, d, d), lambda l, j: (l, 0, j)),
            pl.BlockSpec((1, 1, d), lambda l, j: (l, 0, j)),
        ],
        out_specs=pl.BlockSpec((1, rows, d), lambda l, j: (l, 0, j)),
        out_shape=jax.ShapeDtypeStruct((depth, rows, d6), F32),
        compiler_params=_cparams(("parallel", "parallel")),
        name="mod",
    )(c_pad, w_mod, b_mod.reshape(depth, 1, d6))
    return out[:, :nb]


def _rope_kernel(pos_ref, freq_ref, cos_ref, sin_ref):
    ang = pos_ref[0] * freq_ref[...]
    cos_ref[0] = jnp.cos(ang)
    sin_ref[0] = jnp.sin(ang)


def _rope_call(positions):
    nb, l = positions.shape
    half = RET_DK // 2
    inv = (ROPE_BASE ** (-np.arange(half, dtype=np.float32) / half)).astype(np.float32)
    freq = np.zeros((1, LANE), np.float32)
    for base in (0, HALF):
        freq[0, base:base + half] = inv
        freq[0, base + half:base + 2 * half] = inv
    tl = min(l, 512)
    pos = positions.astype(F32).reshape(nb, l, 1)
    shp = jax.ShapeDtypeStruct((nb, l, LANE), F32)
    return pl.pallas_call(
        _rope_kernel,
        grid=(nb, l // tl),
        in_specs=[pl.BlockSpec((1, tl, 1), lambda b, i: (b, i, 0)),
                  pl.BlockSpec((1, LANE), lambda b, i: (0, 0))],
        out_specs=[pl.BlockSpec((1, tl, LANE), lambda b, i: (b, i, 0))] * 2,
        out_shape=[shp, shp],
        compiler_params=_cparams(("parallel", "parallel")),
        name="rope_tables",
    )(pos, jnp.asarray(freq))


def _lb_kernel(p_ref, o_ref):
    p = p_ref[...]
    depth = p.shape[0]
    m = jnp.max(p, axis=0, keepdims=True)
    e = jnp.exp(p - m)
    sm = e / jnp.sum(e, axis=0, keepdims=True)
    acc = jnp.zeros_like(sm[0:1])
    for i in range(depth):
        if i > 0:
            acc = acc + sm[i:i + 1]
        o_ref[i:i + 1, :] = acc


def _lb_call(hg_lb):
    return pl.pallas_call(
        _lb_kernel,
        out_shape=jax.ShapeDtypeStruct(hg_lb.shape, F32),
        name="hg_lower_bounds",
    )(hg_lb.astype(F32))


def _inproj_kernel(x_ref, nw_ref, sc_ref, sh_ref, w_ref, *out_refs, widths):
    x = x_ref[...]
    y = x * lax.rsqrt(jnp.mean(x * x, axis=-1, keepdims=True) + EPS) * nw_ref[...]
    hb = (y * (1.0 + sc_ref[0]) + sh_ref[0]).astype(BF16)
    off = 0
    for o_ref, wd in zip(out_refs, widths):
        o_ref[...] = _dot(hb, w_ref[:, off:off + wd]).astype(o_ref.dtype)
        off += wd


def _inproj_call(x, norm_w, sc, sh, w_cat, layer, widths, dtypes, seq):
    t, d = x.shape
    tm = min(512, seq)
    per_b = seq // tm
    return pl.pallas_call(
        functools.partial(_inproj_kernel, widths=widths),
        grid=(t // tm,),
        in_specs=[
            pl.BlockSpec((tm, d), lambda i: (i, 0)),
            pl.BlockSpec((1, d), lambda i: (0, 0)),
            pl.BlockSpec((1, 1, d), lambda i: (i // per_b, 0, 0)),
            pl.BlockSpec((1, 1, d), lambda i: (i // per_b, 0, 0)),
            pl.BlockSpec((None,) + w_cat.shape[1:], lambda i: (layer, 0, 0)),
        ],
        out_specs=[pl.BlockSpec((tm, wd), lambda i: (i, 0)) for wd in widths],
        out_shape=[jax.ShapeDtypeStruct((t, wd), dt) for wd, dt in zip(widths, dtypes)],
        compiler_params=_cparams(("parallel",)),
        name="norm1_inproj",
    )(x, norm_w.reshape(1, d), sc, sh, w_cat)


def _level_tables(c):
    nl = int(np.log2(c))
    assert 1 << nl == c
    idx = np.arange(c)
    mats, masks = [], [np.eye(c, dtype=np.float32)]
    t = idx[None, :]
    i = idx[:, None]
    for lvl in range(nl):
        h = c >> (lvl + 1)
        blk, pos = idx // (2 * h), idx % (2 * h)
        m = (blk * 2 * h + h - 1)[:, None]
        right = (pos >= h)[:, None]
        a = np.where(right, (t > m) & (t <= i), (t > i) & (t <= m))
        mats.append(a.astype(np.float32))
        same = blk[:, None] == blk[None, :]
        masks.append((same & right & (pos < h)[None, :]).astype(np.float32))
    mats.append((t <= i).astype(np.float32))
    mats.append((t > i).astype(np.float32))
    return np.concatenate(mats, 0), np.stack(masks, 0), nl


def _pair_decays(g, mall_ref):
    gs = _dot(mall_ref[...], jnp.concatenate(_split2(g), axis=1))
    return jnp.exp(gs[:, :LANE] + gs[:, LANE:])


def _pair_keys(k, e, c, nl):
    levels = [(k * e[lvl * c:(lvl + 1) * c]).astype(BF16) for lvl in range(nl)]
    return k.astype(BF16), levels, (k * e[(nl + 1) * c:(nl + 2) * c]).astype(BF16)


def _own_half(pair_tile, hd, low):
    return jnp.where(low, pair_tile, 0.0) if hd % 2 == 0 else jnp.where(low, 0.0, pair_tile)


def _gated_head(q, kb, klv, e, v, st, masks_ref, c, nl):
    scores = masks_ref[0] * _dot_nt(q.astype(BF16), kb)
    for lvl in range(nl):
        scores = scores + masks_ref[lvl + 1] * _dot_nt((q * e[lvl * c:(lvl + 1) * c]).astype(BF16), klv[lvl])
    q_in = (q * e[nl * c:(nl + 1) * c]).astype(BF16)
    return _dot(scores.astype(BF16), v) + _dot_nt(q_in, st.astype(BF16))


def _gla_kernel(q_ref, k_ref, v_ref, gg_ref, ga_ref, wa2_ref, ba2_ref, nw_ref, mall_ref, masks_ref,
                o_ref, st_ref, *, c, nl):
    low = lax.broadcasted_iota(jnp.int32, (c, LANE), 1) < HALF
    for b in range(q_ref.shape[0]):
        log_alpha = _log_sigmoid(_dot_f32(ga_ref[b], wa2_ref[...]) + ba2_ref[...]) * (1.0 / GLA_TAU)
        for pr in range(N_PAIRS):
            psl = slice(pr * LANE, (pr + 1) * LANE)
            e = _pair_decays(log_alpha[:, psl], mall_ref)
            kb, klv, k_end = _pair_keys(k_ref[b, :, psl].astype(F32), e, c, nl)
            dec = e[(nl + 1) * c - 1:(nl + 1) * c, :]
            qp = q_ref[b, :, psl].astype(F32) * (GLA_DK ** -0.5)
            for hd in (2 * pr, 2 * pr + 1):
                sl = slice(hd * LANE, (hd + 1) * LANE)
                q = _own_half(qp, hd, low)
                v = v_ref[b, :, sl]
                st = st_ref[b, hd]
                o = _gated_head(q, kb, klv, e, v, st, masks_ref, c, nl)
                st_ref[b, hd] = dec * st + _dot_tn(v, k_end)
                y = o * lax.rsqrt(jnp.sum(o * o, axis=-1, keepdims=True) * (1.0 / GLA_DV) + EPS) * nw_ref[:, sl]
                gate = gg_ref[b, :, sl].astype(F32)
                o_ref[b, :, sl] = (y * gate * _sigmoid(gate)).astype(o_ref.dtype)


def _hg_kernel(q_ref, f_ref, v_ref, og_ref, lb_ref, nw_ref, mall_ref, masks_ref,
               o_ref, st_ref, *, c, nl):
    low = lax.broadcasted_iota(jnp.int32, (c, LANE), 1) < HALF
    for b in range(q_ref.shape[0]):
        for pr in range(N_PAIRS):
            psl = slice(pr * LANE, (pr + 1) * LANE)
            lb = lb_ref[:, psl]
            hf = f_ref[b, :, psl]
            la = jnp.log(lb)
            lc = jnp.log1p(-lb) + _log_sigmoid(hf)
            log_f = jnp.maximum(la, lc) + jnp.log1p(jnp.exp(-jnp.abs(la - lc)))
            e = _pair_decays(log_f, mall_ref)
            kb, klv, k_end = _pair_keys((1.0 - lb) * _sigmoid(-hf), e, c, nl)
            v = v_ref[b, :, psl]
            st = st_ref[b, pr]
            qp = q_ref[b, :, psl].astype(F32)
            o_even = _gated_head(_own_half(qp, 0, low), kb, klv, e, v, st, masks_ref, c, nl)
            o_odd = _gated_head(_own_half(qp, 1, low), kb, klv, e, v, st, masks_ref, c, nl)
            o = jnp.where(low, o_even, o_odd)
            st_ref[b, pr] = e[(nl + 1) * c - 1:(nl + 1) * c, :] * st + _dot_tn(v, k_end)
            sq = o * o
            ms = jnp.where(low, jnp.sum(jnp.where(low, sq, 0.0), axis=-1, keepdims=True),
                           jnp.sum(jnp.where(low, 0.0, sq), axis=-1, keepdims=True)) * (1.0 / HG_DV)
            y = o * lax.rsqrt(ms + EPS) * nw_ref[:, psl]
            o_ref[b, :, psl] = (y * _sigmoid(og_ref[b, :, psl].astype(F32))).astype(o_ref.dtype)


def _tok_spec(nb, c, width):
    return pl.BlockSpec((nb, c, width), lambda i: (0, i, 0))


def _const_spec(shape):
    nd = len(shape)
    return pl.BlockSpec(shape, lambda i: (0,) * nd)


def _seq_view(a, nb, seq):
    return a.reshape(nb, seq, a.shape[-1])


def _ret_tables(c):
    hs = np.arange(N_HEADS, dtype=np.float64)
    log_gamma = np.log(1.0 - np.exp2(-5.0 - hs))
    idx = np.arange(c, dtype=np.float64)
    rel = idx[:, None] - idx[None, :]
    dmat = np.where(rel >= 0, np.exp(log_gamma[:, None, None] * np.maximum(rel, 0.0)), 0.0)
    qdec = np.exp(log_gamma[:, None] * (idx + 1.0))
    kdec = np.exp(log_gamma[:, None] * (c - 1.0 - idx))
    cdec = np.exp(log_gamma * c)
    qfull = np.broadcast_to(qdec[:, :, None], (N_HEADS, c, LANE)).astype(np.float32)
    kpair = np.zeros((N_PAIRS, c, LANE), np.float32)
    for hd in range(N_HEADS):
        kpair[hd // 2, :, (hd % 2) * HALF:(hd % 2 + 1) * HALF] = kdec[hd][:, None]
    return dmat.astype(np.float32), qfull, kpair, [float(np.float32(v)) for v in cdec]


def _ret_kernel(q_ref, k_ref, v_ref, og_ref, cos_ref, sin_ref, dmat_ref, qdec_ref, kdec_ref,
                o_ref, st_ref, *, c, cdec):
    half = RET_DK // 2
    lane = lax.broadcasted_iota(jnp.int32, (c, LANE), 1)
    first = (lane & (HALF - 1)) < half
    low = lane < HALF

    for b in range(q_ref.shape[0]):
        cos = cos_ref[b]
        sin = sin_ref[b]

        def rotary(t):
            rot = jnp.where(first, -pltpu.roll(t, LANE - half, 1), pltpu.roll(t, half, 1))
            return t * cos + rot * sin

        for pr in range(N_PAIRS):
            psl = slice(pr * LANE, (pr + 1) * LANE)
            k = rotary(k_ref[b, :, psl].astype(F32))
            kb = k.astype(BF16)
            k_end = (k * kdec_ref[pr]).astype(BF16)
            qp = rotary(q_ref[b, :, psl].astype(F32)) * (RET_DK ** -0.5)
            for hd in (2 * pr, 2 * pr + 1):
                sl = slice(hd * LANE, (hd + 1) * LANE)
                q = _own_half(qp, hd, low)
                v = v_ref[b, :, sl]
                st = st_ref[b, hd]
                scores = _dot_nt(q.astype(BF16), kb) * dmat_ref[hd]
                o = _dot(scores.astype(BF16), v) + _dot_nt((q * qdec_ref[hd]).astype(BF16), st.astype(BF16))
                st_ref[b, hd] = cdec[hd] * st + _dot_tn(v, k_end)
                mu = jnp.sum(o, axis=-1, keepdims=True) * (1.0 / RET_DV)
                dlt = jnp.where(lane < RET_DV, o - mu, 0.0)
                var = jnp.sum(dlt * dlt, axis=-1, keepdims=True) * (1.0 / RET_DV)
                gate = og_ref[b, :, sl].astype(F32)
                o_ref[b, :, sl] = (dlt * lax.rsqrt(var + EPS) * gate * _sigmoid(gate)).astype(o_ref.dtype)


def _mixer_kernel(gq, gk, gv, gg, ga, wa2, ba2, gnw, hq, hf, hi, hg, lb, hnw, rq, rk, rv, rg, cos, sin,
                  mall, masks, dmat, qdec, kdec, o_gla, o_hg, o_ret, st_gla, st_hg, st_ret, *, c, nl, cdec):
    @pl.when(pl.program_id(0) == 0)
    def _():
        st_gla[...] = jnp.zeros_like(st_gla)
        st_hg[...] = jnp.zeros_like(st_hg)
        st_ret[...] = jnp.zeros_like(st_ret)

    _gla_kernel(gq, gk, gv, gg, ga, wa2, ba2, gnw, mall, masks, o_gla, st_gla, c=c, nl=nl)
    _hg_kernel(hq, hf, hi, hg, lb, hnw, mall, masks, o_hg, st_hg, c=c, nl=nl)
    _ret_kernel(rq, rk, rv, rg, cos, sin, dmat, qdec, kdec, o_ret, st_ret, c=c, cdec=cdec)


def _mixer_call(parts, wa2p, ba2p, gnw, lbp, hnw, cos, sin, nb, seq):
    gq, gk, gv, gg, ga, hq, hf, hi, hg, rq, rk, rv, rg = parts
    c = min(128, seq)
    n = seq // c
    mall, masks, nl = _level_tables(c)
    dmat, qdec, kdec, cdec = _ret_tables(c)
    consts = [wa2p, ba2p, gnw]
    sv = lambda a: _seq_view(a, nb, seq)
    tok = lambda a: _tok_spec(nb, c, a.shape[-1])
    const = lambda a: _const_spec(a.shape)
    tables = [jnp.asarray(mall, BF16), jnp.asarray(masks), jnp.asarray(dmat), jnp.asarray(qdec),
              jnp.asarray(kdec)]
    o_gla, o_hg, o_ret = pl.pallas_call(
        functools.partial(_mixer_kernel, c=c, nl=nl, cdec=cdec),
        grid=(n,),
        in_specs=([tok(a) for a in (gq, gk, gv, gg, ga)] + [const(a) for a in consts]
                  + [tok(a) for a in (hq, hf, hi, hg)] + [const(lbp), const(hnw)]
                  + [tok(a) for a in (rq, rk, rv, rg, cos, sin)] + [const(a) for a in tables]),
        out_specs=[_tok_spec(nb, c, HEAD_W), _tok_spec(nb, c, PAIR_W), _tok_spec(nb, c, HEAD_W)],
        out_shape=[jax.ShapeDtypeStruct((nb, seq, HEAD_W), BF16), jax.ShapeDtypeStruct((nb, seq, PAIR_W), BF16),
                   jax.ShapeDtypeStruct((nb, seq, HEAD_W), BF16)],
        scratch_shapes=[pltpu.VMEM((nb, N_HEADS, LANE, LANE), F32), pltpu.VMEM((nb, N_PAIRS, LANE, LANE), F32),
                        pltpu.VMEM((nb, N_HEADS, LANE, LANE), F32)],
        compiler_params=_cparams(("arbitrary",)),
        name="token_mixers",
    )(sv(gq), sv(gk), sv(gv), sv(gg), sv(ga), wa2p, ba2p, gnw, sv(hq), sv(hf), sv(hi), sv(hg), lbp, hnw,
      sv(rq), sv(rk), sv(rv), sv(rg), cos, sin, *tables)
    t = nb * seq
    return o_gla.reshape(t, HEAD_W), o_hg.reshape(t, PAIR_W), o_ret.reshape(t, HEAD_W)


def _outproj_router_kernel(x_ref, oa_ref, ob_ref, oc_ref, wo_ref, g1_ref, nw_ref, sc_ref, sh_ref,
                           wr_ref, br_ref, tri_ref, upper_ref,
                           xo_ref, h_ref, tw_ref, lo_ref, tab_ref, cnt_ref, *, tm):
    @pl.when(pl.program_id(0) == 0)
    def _():
        cnt_ref[...] = jnp.zeros_like(cnt_ref)

    wa, wb = oa_ref.shape[1], ob_ref.shape[1]
    mix = (_dot(oa_ref[...], wo_ref[0:wa, :]) + _dot(ob_ref[...], wo_ref[wa:wa + wb, :])
           + _dot(oc_ref[...], wo_ref[wa + wb:, :]))
    x = x_ref[...] + g1_ref[0] * mix
    xo_ref[...] = x
    y = x * lax.rsqrt(jnp.mean(x * x, axis=-1, keepdims=True) + EPS) * nw_ref[...]
    h = y * (1.0 + sc_ref[0]) + sh_ref[0]
    h_ref[...] = h.astype(h_ref.dtype)

    lg = _dot_f32(h, wr_ref[...]) + br_ref[...]
    lane = lax.broadcasted_iota(jnp.int32, (tm, LANE), 1)
    sel_e, sel_v = [], []
    for _ in range(TOP_K):
        m = jnp.max(lg, axis=-1, keepdims=True)
        idx = jnp.min(jnp.where(lg == m, lane, LANE), axis=-1, keepdims=True)
        sel_e.append(idx)
        sel_v.append(m)
        lg = jnp.where(lane == idx, -jnp.inf, lg)
    ex = [jnp.exp(v - sel_v[0]) for v in sel_v]
    den = ex[0] + ex[1] + ex[2] + ex[3]
    hot = [(lane == idx) for idx in sel_e]
    onehot = jnp.zeros((tm, LANE), F32)
    for hk in hot:
        onehot = onehot + jnp.where(hk, 1.0, 0.0)
    in_tile = _dot(tri_ref[...], onehot.astype(BF16))
    earlier = cnt_ref[...]
    tile_cnt = jnp.floor((jnp.sum(onehot, axis=0, keepdims=True) + (RUN_ALIGN - 1.0)) * (1.0 / RUN_ALIGN)) * RUN_ALIGN
    tile_start = _dot_f32(jnp.broadcast_to(tile_cnt, (8, LANE)), upper_ref[...])[0:1]
    local = tile_start + in_tile
    tw = jnp.zeros((tm, LANE), F32)
    lo = jnp.zeros((tm, LANE), jnp.int32)
    for kk in range(TOP_K):
        lrow = jnp.sum(jnp.where(hot[kk], local, 0.0), axis=-1, keepdims=True).astype(jnp.int32)
        tw = jnp.where(lane == kk, ex[kk] / den, tw)
        lo = jnp.where(lane == kk, lrow, lo)
    tw_ref[...] = tw
    lo_ref[...] = lo
    tab_ref[0, 0:1, :] = tile_start.astype(jnp.int32)
    tab_ref[0, 1:2, :] = tile_cnt.astype(jnp.int32)
    tab_ref[0, 2:3, :] = earlier.astype(jnp.int32)
    tab_ref[0, 3:4, :] = jnp.broadcast_to(jnp.sum(tile_cnt, axis=-1, keepdims=True), (1, LANE)).astype(jnp.int32)
    tab_ref[0, 4:8, :] = jnp.zeros((4, LANE), jnp.int32)
    cnt_ref[...] = earlier + tile_cnt


def _outproj_router_call(x, oa, ob, oc, wo, layer, g1, nw, sc, sh, wr, br, seq):
    t, d = x.shape
    tm = min(ROUTE_TILE, seq)
    per_b = seq // tm
    tri = np.tril(np.ones((tm, tm), np.float32), -1)
    upper = np.triu(np.ones((LANE, LANE), np.float32), 1)
    row = lambda w: pl.BlockSpec((tm, w), lambda i: (i, 0))
    const = lambda shape: pl.BlockSpec(shape, lambda i: (0,) * len(shape))
    perb = pl.BlockSpec((1, 1, d), lambda i: (i // per_b, 0, 0))
    n_tiles = t // tm
    return pl.pallas_call(
        functools.partial(_outproj_router_kernel, tm=tm),
        grid=(n_tiles,),
        in_specs=[row(d), row(oa.shape[1]), row(ob.shape[1]), row(oc.shape[1]),
                  pl.BlockSpec((None,) + wo.shape[1:], lambda i: (layer, 0, 0)), perb,
                  const((1, d)), perb, perb, const(wr.shape), const(br.shape), const(tri.shape),
                  const(upper.shape)],
        out_specs=[row(d), row(d), row(LANE), row(LANE),
                   pl.BlockSpec((1, 8, LANE), lambda i: (i, 0, 0)), const((1, LANE))],
        out_shape=[jax.ShapeDtypeStruct((t, d), F32), jax.ShapeDtypeStruct((t, d), BF16),
                   jax.ShapeDtypeStruct((t, LANE), F32), jax.ShapeDtypeStruct((t, LANE), jnp.int32),
                   jax.ShapeDtypeStruct((n_tiles, 8, LANE), jnp.int32), jax.ShapeDtypeStruct((1, LANE), F32)],
        compiler_params=_cparams(("arbitrary",)),
        name="outproj_norm2_router",
    )(x, oa, ob, oc, wo, g1, nw.reshape(1, d), sc, sh, wr, br, jnp.asarray(tri, BF16), jnp.asarray(upper))


def _run_copies(tab_ref, at_ref, tile, tm, make_copy, act):
    sizes = [s for s in (1 << p for p in range(tm.bit_length() - 1, -1, -1)) if s >= RUN_ALIGN]

    def per_expert(e, carry):
        first = tab_ref[tile, 0, e]
        length = tab_ref[tile, 1, e]
        at = at_ref[tile, e]
        done = jnp.int32(0)
        for size in sizes:
            part = length & size

            @pl.when(part != 0)
            def _():
                act(make_copy(pl.multiple_of(first + done, RUN_ALIGN), pl.multiple_of(at + done, RUN_ALIGN), size))
            done = done + part
        return carry

    lax.fori_loop(0, N_EXPERTS, per_expert, 0)


def _local_rows(tm):
    return TOP_K * tm + N_EXPERTS * RUN_ALIGN


def _dispatch_kernel(tab_ref, at_ref, h_ref, lo_ref, buf_in_ref, buf_ref, st_ref, sem, *, tm, n_tiles):
    del buf_in_ref
    i = pl.program_id(0)
    slot = i % 2

    def copies(tile, s, act):
        _run_copies(tab_ref, at_ref, tile, tm,
                    lambda row, at, size: pltpu.make_async_copy(st_ref.at[s, pl.ds(row, size)],
                                                                buf_ref.at[pl.ds(at, size)], sem.at[s]), act)

    def wait_tile(tile, s):
        total = pl.multiple_of(tab_ref[tile, 3, 0], RUN_ALIGN)
        pltpu.make_async_copy(st_ref.at[s, pl.ds(0, total)], buf_ref.at[pl.ds(0, total)], sem.at[s]).wait()

    @pl.when(i >= 2)
    def _():
        wait_tile(i - 2, slot)

    col = lax.broadcasted_iota(jnp.int32, (tm, st_ref.shape[1]), 1)
    lo = lo_ref[...]
    place = jnp.zeros(col.shape, F32)
    for kk in range(TOP_K):
        place = jnp.where(col == lo[:, kk:kk + 1], 1.0, place)
    st_ref[slot] = _dot_tn(place.astype(BF16), h_ref[...].astype(BF16))
    copies(i, slot, lambda cp: cp.start())

    @pl.when(i == n_tiles - 1)
    def _():
        if n_tiles >= 2:
            wait_tile(i - 1, 1 - slot)
        wait_tile(i, slot)


def _dispatch_call(h, tab, run_at, lo, buf, seq):
    t, d = h.shape
    n_slots = buf.shape[0]
    tm = min(ROUTE_TILE, seq)
    n_tiles = t // tm
    grid_spec = pltpu.PrefetchScalarGridSpec(
        num_scalar_prefetch=2,
        grid=(n_tiles,),
        in_specs=[pl.BlockSpec((tm, d), lambda i, tb, sr: (i, 0)),
                  pl.BlockSpec((tm, LANE), lambda i, tb, sr: (i, 0)),
                  pl.BlockSpec(memory_space=pl.ANY)],
        out_specs=pl.BlockSpec(memory_space=pl.ANY),
        scratch_shapes=[pltpu.VMEM((2, _local_rows(tm), d), F32), pltpu.SemaphoreType.DMA((2,))],
    )
    return pl.pallas_call(
        functools.partial(_dispatch_kernel, tm=tm, n_tiles=n_tiles),
        grid_spec=grid_spec,
        out_shape=jax.ShapeDtypeStruct((n_slots, d), F32),
        input_output_aliases={4: 0},
        compiler_params=_cparams(("arbitrary",)),
        name="moe_dispatch",
    )(tab, run_at, h, lo, buf)


def _expert_kernel(be_ref, nx_ref, used_ref, nu_ref, x_ref, wgu_ref, bgu_ref, wdn_ref, bdn_ref, o_ref,
                   wgu_f32, wdn_f32, wgu_bf, wdn_bf, sem, *, d_ff, layer):
    i = pl.program_id(0)
    live = i < nu_ref[0]
    e = be_ref[i]
    e_next = nx_ref[i]

    def weight_copies(expert):
        return (pltpu.make_async_copy(wgu_ref.at[layer, expert], wgu_f32, sem.at[0]),
                pltpu.make_async_copy(wdn_ref.at[layer, expert], wdn_f32, sem.at[1]))

    @pl.when(i == 0)
    def _():
        for cp in weight_copies(e):
            cp.start()

    @pl.when(jnp.logical_and(live, jnp.logical_or(i == 0, e != be_ref[jnp.maximum(i - 1, 0)])))
    def _():
        for cp in weight_copies(e):
            cp.wait()
        wgu_bf[...] = wgu_f32[...].astype(BF16)
        wdn_bf[...] = wdn_f32[...].astype(BF16)

        @pl.when(e_next != e)
        def _():
            for cp in weight_copies(e_next):
                cp.start()

    def ffn(rows):
        gu = _dot(x_ref[0:rows, :].astype(BF16), wgu_bf[...]) + bgu_ref[0, 0]
        gate = jnp.minimum(gu[:, :d_ff], SWIGLU_LIMIT)
        up = jnp.clip(gu[:, d_ff:], -SWIGLU_LIMIT, SWIGLU_LIMIT)
        act = (up + 1.0) * gate * _sigmoid(SWIGLU_ALPHA * gate)
        o_ref[0:rows, :] = _dot(act.astype(BF16), wdn_bf[...]) + bdn_ref[0, 0]
        if rows < MOE_BLOCK:
            o_ref[rows:, :] = jnp.zeros((MOE_BLOCK - rows, o_ref.shape[1]), o_ref.dtype)

    used = used_ref[i]
    prefixes = [MOE_BLOCK // 4, MOE_BLOCK // 2, MOE_BLOCK]
    for lo_rows, rows in zip([0] + prefixes[:-1], prefixes):
        pl.when(jnp.logical_and(live, jnp.logical_and(used > lo_rows, used <= rows)))(
            functools.partial(ffn, rows))

    @pl.when(jnp.logical_not(live))
    def _():
        o_ref[...] = jnp.zeros_like(o_ref)


def _expert_call(buf, block_e, next_e, block_used, n_used, wgu, bgu, wdn, bdn, layer):
    n_slots, d = buf.shape
    n_blocks = n_slots // MOE_BLOCK
    depth, ne, _, f2 = wgu.shape
    d_ff = f2 // 2
    blk = lambda i, be, nx, us, nu: (jnp.minimum(i, nu[0] - 1), 0)
    exp4 = lambda i, be, nx, us, nu: (layer, be[jnp.minimum(i, nu[0] - 1)], 0, 0)
    grid_spec = pltpu.PrefetchScalarGridSpec(
        num_scalar_prefetch=4,
        grid=(n_blocks,),
        in_specs=[pl.BlockSpec((MOE_BLOCK, d), blk),
                  pl.BlockSpec(memory_space=pl.ANY),
                  pl.BlockSpec((1, 1, 1, f2), exp4),
                  pl.BlockSpec(memory_space=pl.ANY),
                  pl.BlockSpec((1, 1, 1, d), exp4)],
        out_specs=pl.BlockSpec((MOE_BLOCK, d), lambda i, be, nx, us, nu: (i, 0)),
        scratch_shapes=[pltpu.VMEM((d, f2), F32), pltpu.VMEM((d_ff, d), F32),
                        pltpu.VMEM((d, f2), BF16), pltpu.VMEM((d_ff, d), BF16),
                        pltpu.SemaphoreType.DMA((2,))],
    )
    return pl.pallas_call(
        functools.partial(_expert_kernel, d_ff=d_ff, layer=layer),
        grid_spec=grid_spec,
        out_shape=jax.ShapeDtypeStruct((n_slots, d), F32),
        compiler_params=_cparams(("arbitrary",)),
        name="moe_experts",
    )(block_e, next_e, block_used, n_used, buf, wgu, bgu.reshape(depth, ne, 1, f2), wdn, bdn.reshape(depth, ne, 1, d))


def _combine_kernel(tab_ref, at_ref, yb_ref, lo_ref, w_ref, x_ref, g2_ref, fw_ref, o_ref, rs_ref, sem, *,
                    tm, final_norm):
    i = pl.program_id(0)
    n = pl.num_programs(0)
    rows = rs_ref.shape[1]

    def run_copies(tile, s, act):
        _run_copies(tab_ref, at_ref, tile, tm,
                    lambda row, at, size: pltpu.make_async_copy(yb_ref.at[pl.ds(at, size)],
                                                                rs_ref.at[s, pl.ds(row, size)], sem.at[s]), act)

    @pl.when(i == 0)
    def _():
        rs_ref[...] = jnp.zeros_like(rs_ref)
        run_copies(0, 0, lambda cp: cp.start())

    @pl.when(i + 1 < n)
    def _():
        run_copies(i + 1, (i + 1) % 2, lambda cp: cp.start())

    slot = i % 2
    total = pl.multiple_of(tab_ref[i, 3, 0], RUN_ALIGN)
    pltpu.make_async_copy(yb_ref.at[pl.ds(0, total)], rs_ref.at[slot, pl.ds(0, total)], sem.at[slot]).wait()
    r = rs_ref[slot].astype(BF16)
    col = lax.broadcasted_iota(jnp.int32, (tm, rows), 1)
    w = w_ref[...]
    lo = lo_ref[...]
    pw = jnp.zeros((tm, rows), F32)
    for kk in range(TOP_K):
        pw = jnp.where(col == lo[:, kk:kk + 1], w[:, kk:kk + 1], pw)
    y = _dot(pw.astype(BF16), r)
    out = x_ref[...] + g2_ref[0] * y
    if final_norm:
        out = out * lax.rsqrt(jnp.mean(out * out, axis=-1, keepdims=True) + EPS) * fw_ref[...]
    o_ref[...] = out


def _combine_call(yb, tab, src, lo, tw, x, g2, final_w, final_norm, seq):
    t, d = x.shape
    tm = min(ROUTE_TILE, seq)
    per_b = seq // tm
    grid_spec = pltpu.PrefetchScalarGridSpec(
        num_scalar_prefetch=2,
        grid=(t // tm,),
        in_specs=[pl.BlockSpec(memory_space=pl.ANY),
                  pl.BlockSpec((tm, LANE), lambda i, tb, sr: (i, 0)),
                  pl.BlockSpec((tm, LANE), lambda i, tb, sr: (i, 0)),
                  pl.BlockSpec((tm, d), lambda i, tb, sr: (i, 0)),
                  pl.BlockSpec((1, 1, d), lambda i, tb, sr: (i // per_b, 0, 0)),
                  pl.BlockSpec((1, d), lambda i, tb, sr: (0, 0))],
        out_specs=pl.BlockSpec((tm, d), lambda i, tb, sr: (i, 0)),
        scratch_shapes=[pltpu.VMEM((2, _local_rows(tm), d), F32), pltpu.SemaphoreType.DMA((2,))],
    )
    return pl.pallas_call(
        functools.partial(_combine_kernel, tm=tm, final_norm=final_norm),
        grid_spec=grid_spec,
        out_shape=jax.ShapeDtypeStruct((t, d), F32),
        compiler_params=_cparams(("arbitrary",)),
        name="moe_combine",
    )(tab, src, yb, lo, tw, x, g2, final_w.reshape(1, d))


def _split_heads(w, hd):
    return w.reshape(w.shape[:-1] + (N_HEADS, hd))


def _pad_last(w, lo, hi):
    return jnp.pad(w, [(0, 0)] * (w.ndim - 1) + [(lo, hi)])


def _tile_cols(w, hd):
    return _pad_last(_split_heads(w, hd), 0, LANE - hd).reshape(w.shape[:-1] + (HEAD_W,))


def _pair_cols(w, hd):
    return _pad_last(_split_heads(w, hd), 0, HALF - hd).reshape(w.shape[:-1] + (PAIR_W,))


def _tile_rows(w, hd):
    lead, cols = w.shape[:-2], w.shape[-1]
    w = w.reshape(lead + (N_HEADS, hd, cols))
    w = jnp.pad(w, [(0, 0)] * (w.ndim - 2) + [(0, LANE - hd), (0, 0)])
    return w.reshape(lead + (HEAD_W, cols))


IN_PARTS = (("gq", GLA_DK, _pair_cols), ("gk", GLA_DK, _pair_cols), ("gv", GLA_DV, _tile_cols),
            ("gg", GLA_DV, _tile_cols), ("ga", None, None),
            ("hq", HG_DK, _pair_cols), ("hf", HG_DK, _pair_cols), ("hi", HG_DV, _pair_cols),
            ("hg", HG_DV, _pair_cols),
            ("rq", RET_DK, _pair_cols), ("rk", RET_DK, _pair_cols), ("rv", RET_DV, _tile_cols),
            ("rg", RET_DV, _tile_cols))
F32_PARTS = ("ga", "hf")


def _layout_w_in(w_in):
    cols, off = [], 0
    for _, hd, layout in IN_PARTS:
        if layout is None:
            part = _pad_last(w_in[..., off:off + GLA_RANK], 0, LANE - GLA_RANK)
            off += GLA_RANK
        else:
            part = layout(w_in[..., off:off + N_HEADS * hd], hd)
            off += N_HEADS * hd
        cols.append(part)
    return jnp.concatenate(cols, axis=-1).astype(BF16), tuple(int(p.shape[-1]) for p in cols)


def kernel(x, c, positions, w_mod, b_mod, norm1_w, w_in, gla_wa2, gla_ba2, hg_lb, gla_norm_w, hg_norm_w,
           w_out, norm2_w, w_r, b_r, w_gu, b_gu, w_dn, b_dn, final_norm_w):
    nb, seq, d = x.shape
    t = nb * seq
    depth = w_mod.shape[0]
    dtypes = tuple(F32 if name in F32_PARTS else BF16 for name, _, _ in IN_PARTS)

    mod = _mod_call(c, w_mod, b_mod)
    cos, sin = _rope_call(positions)
    lb_all = _lb_call(hg_lb)

    n_assign = t * TOP_K
    n_run_pad = N_EXPERTS * (RUN_ALIGN - 1) * (t // min(ROUTE_TILE, seq))
    n_blocks = (n_assign + n_run_pad + MOE_BLOCK - 1) // MOE_BLOCK + N_EXPERTS
    n_slots = n_blocks * MOE_BLOCK

    w_cat, widths = _layout_w_in(w_in)
    n_gla, n_hg = N_HEADS * GLA_DV, N_HEADS * HG_DV
    wo_p = jnp.concatenate([_tile_rows(w_out[:, :n_gla], GLA_DV), w_out[:, n_gla:n_gla + n_hg],
                            _tile_rows(w_out[:, n_gla + n_hg:], RET_DV)], axis=1).astype(BF16)
    wa2p = jnp.pad(_pair_cols(gla_wa2, GLA_DK), ((0, 0), (0, LANE - GLA_RANK), (0, 0)))
    ba2p = _pair_cols(gla_ba2, GLA_DK)
    gnw = _tile_cols(gla_norm_w, GLA_DV)
    wr_p = _pad_last(w_r, 0, LANE - N_EXPERTS)
    br_p = jnp.pad(b_r, ((0, 0), (0, LANE - N_EXPERTS)), constant_values=NEG_BIG)

    buf = jnp.zeros((n_slots, d), F32)
    xf = x.reshape(t, d)
    for layer in range(depth):
        sh1, sc1, g1, sh2, sc2, g2 = [m.reshape(nb, 1, d) for m in jnp.split(mod[layer], 6, axis=-1)]

        parts = _inproj_call(xf, norm1_w[layer], sc1, sh1, w_cat, layer, widths, dtypes, seq)
        o_gla, o_hg, o_ret = _mixer_call(parts, wa2p[layer], ba2p[layer:layer + 1], gnw[layer:layer + 1],
                                         lb_all[layer:layer + 1], hg_norm_w[layer:layer + 1], cos, sin, nb, seq)
        xf, h2, tw, lo, tab, cnt = _outproj_router_call(
            xf, o_gla, o_hg, o_ret, wo_p, layer, g1, norm2_w[layer], sc2, sh2, wr_p[layer],
            br_p[layer:layer + 1], seq)

        counts = cnt[0, :N_EXPERTS].astype(jnp.int32)
        padded = (counts + MOE_BLOCK - 1) // MOE_BLOCK * MOE_BLOCK
        pends = jnp.cumsum(padded)
        pstarts = pends - padded
        block_start = jnp.arange(n_blocks, dtype=jnp.int32) * MOE_BLOCK
        block_e = jnp.minimum(jnp.sum((pends[None, :] <= block_start[:, None]).astype(jnp.int32), axis=1),
                              N_EXPERTS - 1)
        n_used = (pends[-1:] // MOE_BLOCK).astype(jnp.int32)
        run_at = tab[:, 2, :] + jnp.pad(pstarts, (0, LANE - N_EXPERTS))[None, :]

        ids = jnp.arange(N_EXPERTS, dtype=jnp.int32)
        later = jnp.where((ids[None, :] > ids[:, None]) & (padded[None, :] > 0), ids[None, :], N_EXPERTS)
        follower = jnp.min(later, axis=1)
        follower = jnp.where(follower == N_EXPERTS, ids, follower)
        of_block = block_e[:, None] == ids[None, :]
        next_e = jnp.sum(jnp.where(of_block, follower[None, :], 0), axis=1)
        run_end = jnp.sum(jnp.where(of_block, (pstarts + counts)[None, :], 0), axis=1)
        block_used = jnp.clip(run_end - block_start, 0, MOE_BLOCK).astype(jnp.int32)

        buf = _dispatch_call(h2, tab, run_at, lo, buf, seq)
        yb = _expert_call(buf, block_e, next_e, block_used, n_used, w_gu, b_gu, w_dn, b_dn, layer)
        xf = _combine_call(yb, tab, run_at, lo, tw, xf, g2, final_norm_w, layer == depth - 1, seq)

    return xf.reshape(nb, seq, d)
```

```python
import functools

import numpy as np
import jax
import jax.numpy as jnp
from jax import lax
from jax.experimental import pallas as pl
from jax.experimental.pallas import tpu as pltpu

F32 = jnp.float32
BF16 = jnp.bfloat16

N_HEADS = 4
GLA_DK, GLA_DV, GLA_RANK, GLA_TAU = 48, 96, 16, 16.0
HG_DK, HG_DV = 64, 64
RET_DK, RET_DV = 48, 96
ROPE_BASE = 10000.0
N_EXPERTS, TOP_K = 32, 4
SWIGLU_LIMIT, SWIGLU_ALPHA = 7.0, 1.702
MOE_BLOCK = 512
ROUTE_TILE = 512
RUN_ALIGN = 8
EPS = 1e-6

LANE = 128
HALF = LANE // 2
N_PAIRS = N_HEADS // 2
HEAD_W = N_HEADS * LANE
PAIR_W = N_PAIRS * LANE
VMEM_LIMIT = 56 * 1024 * 1024
NEG_BIG = -1e30


def _cparams(sem):
    return pltpu.CompilerParams(dimension_semantics=sem, vmem_limit_bytes=VMEM_LIMIT)


def _dot(a, b):
    return jnp.dot(a, b, preferred_element_type=F32)


def _dot_nt(a, b):
    return lax.dot_general(a, b, (((1,), (1,)), ((), ())), preferred_element_type=F32)


def _dot_tn(a, b):
    return lax.dot_general(a, b, (((0,), (0,)), ((), ())), preferred_element_type=F32)


def _split2(a):
    hi = a.astype(BF16)
    return hi, (a - hi.astype(F32)).astype(BF16)


def _dot_f32(a, b):
    a_hi, a_lo = _split2(a)
    b_hi, b_lo = _split2(b)
    return _dot(a_hi, b_hi) + _dot(a_hi, b_lo) + _dot(a_lo, b_hi)


def _sigmoid(x):
    return 1.0 / (1.0 + jnp.exp(-x))


def _log_sigmoid(x):
    return jnp.minimum(x, 0.0) - jnp.log1p(jnp.exp(-jnp.abs(x)))


def _mod_kernel(c_ref, w_ref, b_ref, o_ref):
    c = c_ref[...]
    o_ref[0] = _dot_f32(c * _sigmoid(c), w_ref[0]) + b_ref[0]


def _mod_call(c, w_mod, b_mod):
    depth, d, d6 = w_mod.shape
    nb = c.shape[0]
    rows = 8
    c_pad = jnp.zeros((rows, d), F32).at[:nb].set(c)
    out = pl.pallas_call(
        _mod_kernel,
        grid=(depth, d6 // d),
        in_specs=[
            pl.BlockSpec((rows, d), lambda l, j: (0, 0)),
            pl.BlockSpec((1, d, d), lambda l, j: (l, 0, j)),
            pl.BlockSpec((1, 1, d), lambda l, j: (l, 0, j)),
        ],
        out_specs=pl.BlockSpec((1, rows, d), lambda l, j: (l, 0, j)),
        out_shape=jax.ShapeDtypeStruct((depth, rows, d6), F32),
        compiler_params=_cparams(("parallel", "parallel")),
        name="mod",
    )(c_pad, w_mod, b_mod.reshape(depth, 1, d6))
    return out[:, :nb]


def _rope_kernel(pos_ref, freq_ref, cos_ref, sin_ref):
    ang = pos_ref[0] * freq_ref[...]
    cos_ref[0] = jnp.cos(ang)
    sin_ref[0] = jnp.sin(ang)


def _rope_call(positions):
    nb, l = positions.shape
    half = RET_DK // 2
    inv = (ROPE_BASE ** (-np.arange(half, dtype=np.float32) / half)).astype(np.float32)
    freq = np.zeros((1, LANE), np.float32)
    for base in (0, HALF):
        freq[0, base:base + half] = inv
        freq[0, base + half:base + 2 * half] = inv
    tl = min(l, 512)
    pos = positions.astype(F32).reshape(nb, l, 1)
    shp = jax.ShapeDtypeStruct((nb, l, LANE), F32)
    return pl.pallas_call(
        _rope_kernel,
        grid=(nb, l // tl),
        in_specs=[pl.BlockSpec((1, tl, 1), lambda b, i: (b, i, 0)),
                  pl.BlockSpec((1, LANE), lambda b, i: (0, 0))],
        out_specs=[pl.BlockSpec((1, tl, LANE), lambda b, i: (b, i, 0))] * 2,
        out_shape=[shp, shp],
        compiler_params=_cparams(("parallel", "parallel")),
        name="rope_tables",
    )(pos, jnp.asarray(freq))


def _lb_kernel(p_ref, o_ref):
    p = p_ref[...]
    depth = p.shape[0]
    m = jnp.max(p, axis=0, keepdims=True)
    e = jnp.exp(p - m)
    sm = e / jnp.sum(e, axis=0, keepdims=True)
    acc = jnp.zeros_like(sm[0:1])
    for i in range(depth):
        if i > 0:
            acc = acc + sm[i:i + 1]
        o_ref[i:i + 1, :] = acc


def _lb_call(hg_lb):
    return pl.pallas_call(
        _lb_kernel,
        out_shape=jax.ShapeDtypeStruct(hg_lb.shape, F32),
        name="hg_lower_bounds",
    )(hg_lb.astype(F32))


def _inproj_kernel(x_ref, nw_ref, sc_ref, sh_ref, w_ref, *out_refs, widths):
    x = x_ref[...]
    y = x * lax.rsqrt(jnp.mean(x * x, axis=-1, keepdims=True) + EPS) * nw_ref[...]
    hb = (y * (1.0 + sc_ref[0]) + sh_ref[0]).astype(BF16)
    off = 0
    for o_ref, wd in zip(out_refs, widths):
        o_ref[...] = _dot(hb, w_ref[:, off:off + wd]).astype(o_ref.dtype)
        off += wd


def _inproj_call(x, norm_w, sc, sh, w_cat, layer, widths, dtypes, seq):
    t, d = x.shape
    tm = min(512, seq)
    per_b = seq // tm
    return pl.pallas_call(
        functools.partial(_inproj_kernel, widths=widths),
        grid=(t // tm,),
        in_specs=[
            pl.BlockSpec((tm, d), lambda i: (i, 0)),
            pl.BlockSpec((1, d), lambda i: (0, 0)),
            pl.BlockSpec((1, 1, d), lambda i: (i // per_b, 0, 0)),
            pl.BlockSpec((1, 1, d), lambda i: (i // per_b, 0, 0)),
            pl.BlockSpec((None,) + w_cat.shape[1:], lambda i: (layer, 0, 0)),
        ],
        out_specs=[pl.BlockSpec((tm, wd), lambda i: (i, 0)) for wd in widths],
        out_shape=[jax.ShapeDtypeStruct((t, wd), dt) for wd, dt in zip(widths, dtypes)],
        compiler_params=_cparams(("parallel",)),
        name="norm1_inproj",
    )(x, norm_w.reshape(1, d), sc, sh, w_cat)


def _level_tables(c):
    nl = int(np.log2(c))
    assert 1 << nl == c
    idx = np.arange(c)
    mats, masks = [], [np.eye(c, dtype=np.float32)]
    t = idx[None, :]
    i = idx[:, None]
    for lvl in range(nl):
        h = c >> (lvl + 1)
        blk, pos = idx // (2 * h), idx % (2 * h)
        m = (blk * 2 * h + h - 1)[:, None]
        right = (pos >= h)[:, None]
        a = np.where(right, (t > m) & (t <= i), (t > i) & (t <= m))
        mats.append(a.astype(np.float32))
        same = blk[:, None] == blk[None, :]
        masks.append((same & right & (pos < h)[None, :]).astype(np.float32))
    mats.append((t <= i).astype(np.float32))
    mats.append((t > i).astype(np.float32))
    return np.concatenate(mats, 0), np.stack(masks, 0), nl


def _pair_decays(g, mall_ref):
    gs = _dot(mall_ref[...], jnp.concatenate(_split2(g), axis=1))
    return jnp.exp(gs[:, :LANE] + gs[:, LANE:])


def _pair_keys(k, e, c, nl):
    levels = [(k * e[lvl * c:(lvl + 1) * c]).astype(BF16) for lvl in range(nl)]
    return k.astype(BF16), levels, (k * e[(nl + 1) * c:(nl + 2) * c]).astype(BF16)


def _own_half(pair_tile, hd, low):
    return jnp.where(low, pair_tile, 0.0) if hd % 2 == 0 else jnp.where(low, 0.0, pair_tile)


def _gated_head(q, kb, klv, e, v, st, masks_ref, c, nl):
    scores = masks_ref[0] * _dot_nt(q.astype(BF16), kb)
    for lvl in range(nl):
        scores = scores + masks_ref[lvl + 1] * _dot_nt((q * e[lvl * c:(lvl + 1) * c]).astype(BF16), klv[lvl])
    q_in = (q * e[nl * c:(nl + 1) * c]).astype(BF16)
    return _dot(scores.astype(BF16), v) + _dot_nt(q_in, st.astype(BF16))


def _gla_kernel(q_ref, k_ref, v_ref, gg_ref, ga_ref, wa2_ref, ba2_ref, nw_ref, mall_ref, masks_ref,
                o_ref, st_ref, *, c, nl):
    low = lax.broadcasted_iota(jnp.int32, (c, LANE), 1) < HALF
    for b in range(q_ref.shape[0]):
        log_alpha = _log_sigmoid(_dot_f32(ga_ref[b], wa2_ref[...]) + ba2_ref[...]) * (1.0 / GLA_TAU)
        for pr in range(N_PAIRS):
            psl = slice(pr * LANE, (pr + 1) * LANE)
            e = _pair_decays(log_alpha[:, psl], mall_ref)
            kb, klv, k_end = _pair_keys(k_ref[b, :, psl].astype(F32), e, c, nl)
            dec = e[(nl + 1) * c - 1:(nl + 1) * c, :]
            qp = q_ref[b, :, psl].astype(F32) * (GLA_DK ** -0.5)
            for hd in (2 * pr, 2 * pr + 1):
                sl = slice(hd * LANE, (hd + 1) * LANE)
                q = _own_half(qp, hd, low)
                v = v_ref[b, :, sl]
                st = st_ref[b, hd]
                o = _gated_head(q, kb, klv, e, v, st, masks_ref, c, nl)
                st_ref[b, hd] = dec * st + _dot_tn(v, k_end)
                y = o * lax.rsqrt(jnp.sum(o * o, axis=-1, keepdims=True) * (1.0 / GLA_DV) + EPS) * nw_ref[:, sl]
                gate = gg_ref[b, :, sl].astype(F32)
                o_ref[b, :, sl] = (y * gate * _sigmoid(gate)).astype(o_ref.dtype)


def _hg_kernel(q_ref, f_ref, v_ref, og_ref, lb_ref, nw_ref, mall_ref, masks_ref,
               o_ref, st_ref, *, c, nl):
    low = lax.broadcasted_iota(jnp.int32, (c, LANE), 1) < HALF
    for b in range(q_ref.shape[0]):
        for pr in range(N_PAIRS):
            psl = slice(pr * LANE, (pr + 1) * LANE)
            lb = lb_ref[:, psl]
            hf = f_ref[b, :, psl]
            la = jnp.log(lb)
            lc = jnp.log1p(-lb) + _log_sigmoid(hf)
            log_f = jnp.maximum(la, lc) + jnp.log1p(jnp.exp(-jnp.abs(la - lc)))
            e = _pair_decays(log_f, mall_ref)
            kb, klv, k_end = _pair_keys((1.0 - lb) * _sigmoid(-hf), e, c, nl)
            v = v_ref[b, :, psl]
            st = st_ref[b, pr]
            qp = q_ref[b, :, psl].astype(F32)
            o_even = _gated_head(_own_half(qp, 0, low), kb, klv, e, v, st, masks_ref, c, nl)
            o_odd = _gated_head(_own_half(qp, 1, low), kb, klv, e, v, st, masks_ref, c, nl)
            o = jnp.where(low, o_even, o_odd)
            st_ref[b, pr] = e[(nl + 1) * c - 1:(nl + 1) * c, :] * st + _dot_tn(v, k_end)
            sq = o * o
            ms = jnp.where(low, jnp.sum(jnp.where(low, sq, 0.0), axis=-1, keepdims=True),
                           jnp.sum(jnp.where(low, 0.0, sq), axis=-1, keepdims=True)) * (1.0 / HG_DV)
            y = o * lax.rsqrt(ms + EPS) * nw_ref[:, psl]
            o_ref[b, :, psl] = (y * _sigmoid(og_ref[b, :, psl].astype(F32))).astype(o_ref.dtype)


def _tok_spec(nb, c, width):
    return pl.BlockSpec((nb, c, width), lambda i: (0, i, 0))


def _const_spec(shape):
    nd = len(shape)
    return pl.BlockSpec(shape, lambda i: (0,) * nd)


def _seq_view(a, nb, seq):
    return a.reshape(nb, seq, a.shape[-1])


def _ret_tables(c):
    hs = np.arange(N_HEADS, dtype=np.float64)
    log_gamma = np.log(1.0 - np.exp2(-5.0 - hs))
    idx = np.arange(c, dtype=np.float64)
    rel = idx[:, None] - idx[None, :]
    dmat = np.where(rel >= 0, np.exp(log_gamma[:, None, None] * np.maximum(rel, 0.0)), 0.0)
    qdec = np.exp(log_gamma[:, None] * (idx + 1.0))
    kdec = np.exp(log_gamma[:, None] * (c - 1.0 - idx))
    cdec = np.exp(log_gamma * c)
    qfull = np.broadcast_to(qdec[:, :, None], (N_HEADS, c, LANE)).astype(np.float32)
    kpair = np.zeros((N_PAIRS, c, LANE), np.float32)
    for hd in range(N_HEADS):
        kpair[hd // 2, :, (hd % 2) * HALF:(hd % 2 + 1) * HALF] = kdec[hd][:, None]
    return dmat.astype(np.float32), qfull, kpair, [float(np.float32(v)) for v in cdec]


def _ret_kernel(q_ref, k_ref, v_ref, og_ref, cos_ref, sin_ref, dmat_ref, qdec_ref, kdec_ref,
                o_ref, st_ref, *, c, cdec):
    half = RET_DK // 2
    lane = lax.broadcasted_iota(jnp.int32, (c, LANE), 1)
    first = (lane & (HALF - 1)) < half
    low = lane < HALF

    for b in range(q_ref.shape[0]):
        cos = cos_ref[b]
        sin = sin_ref[b]

        def rotary(t):
            rot = jnp.where(first, -pltpu.roll(t, LANE - half, 1), pltpu.roll(t, half, 1))
            return t * cos + rot * sin

        for pr in range(N_PAIRS):
            psl = slice(pr * LANE, (pr + 1) * LANE)
            k = rotary(k_ref[b, :, psl].astype(F32))
            kb = k.astype(BF16)
            k_end = (k * kdec_ref[pr]).astype(BF16)
            qp = rotary(q_ref[b, :, psl].astype(F32)) * (RET_DK ** -0.5)
            for hd in (2 * pr, 2 * pr + 1):
                sl = slice(hd * LANE, (hd + 1) * LANE)
                q = _own_half(qp, hd, low)
                v = v_ref[b, :, sl]
                st = st_ref[b, hd]
                scores = _dot_nt(q.astype(BF16), kb) * dmat_ref[hd]
                o = _dot(scores.astype(BF16), v) + _dot_nt((q * qdec_ref[hd]).astype(BF16), st.astype(BF16))
                st_ref[b, hd] = cdec[hd] * st + _dot_tn(v, k_end)
                mu = jnp.sum(o, axis=-1, keepdims=True) * (1.0 / RET_DV)
                dlt = jnp.where(lane < RET_DV, o - mu, 0.0)
                var = jnp.sum(dlt * dlt, axis=-1, keepdims=True) * (1.0 / RET_DV)
                gate = og_ref[b, :, sl].astype(F32)
                o_ref[b, :, sl] = (dlt * lax.rsqrt(var + EPS) * gate * _sigmoid(gate)).astype(o_ref.dtype)


def _mixer_kernel(gq, gk, gv, gg, ga, wa2, ba2, gnw, hq, hf, hi, hg, lb, hnw, rq, rk, rv, rg, cos, sin,
                  mall, masks, dmat, qdec, kdec, o_gla, o_hg, o_ret, st_gla, st_hg, st_ret, *, c, nl, cdec):
    @pl.when(pl.program_id(0) == 0)
    def _():
        st_gla[...] = jnp.zeros_like(st_gla)
        st_hg[...] = jnp.zeros_like(st_hg)
        st_ret[...] = jnp.zeros_like(st_ret)

    _gla_kernel(gq, gk, gv, gg, ga, wa2, ba2, gnw, mall, masks, o_gla, st_gla, c=c, nl=nl)
    _hg_kernel(hq, hf, hi, hg, lb, hnw, mall, masks, o_hg, st_hg, c=c, nl=nl)
    _ret_kernel(rq, rk, rv, rg, cos, sin, dmat, qdec, kdec, o_ret, st_ret, c=c, cdec=cdec)


def _mixer_call(parts, wa2p, ba2p, gnw, lbp, hnw, cos, sin, nb, seq):
    gq, gk, gv, gg, ga, hq, hf, hi, hg, rq, rk, rv, rg = parts
    c = min(128, seq)
    n = seq // c
    mall, masks, nl = _level_tables(c)
    dmat, qdec, kdec, cdec = _ret_tables(c)
    consts = [wa2p, ba2p, gnw]
    sv = lambda a: _seq_view(a, nb, seq)
    tok = lambda a: _tok_spec(nb, c, a.shape[-1])
    const = lambda a: _const_spec(a.shape)
    tables = [jnp.asarray(mall, BF16), jnp.asarray(masks), jnp.asarray(dmat), jnp.asarray(qdec),
              jnp.asarray(kdec)]
    o_gla, o_hg, o_ret = pl.pallas_call(
        functools.partial(_mixer_kernel, c=c, nl=nl, cdec=cdec),
        grid=(n,),
        in_specs=([tok(a) for a in (gq, gk, gv, gg, ga)] + [const(a) for a in consts]
                  + [tok(a) for a in (hq, hf, hi, hg)] + [const(lbp), const(hnw)]
                  + [tok(a) for a in (rq, rk, rv, rg, cos, sin)] + [const(a) for a in tables]),
        out_specs=[_tok_spec(nb, c, HEAD_W), _tok_spec(nb, c, PAIR_W), _tok_spec(nb, c, HEAD_W)],
        out_shape=[jax.ShapeDtypeStruct((nb, seq, HEAD_W), BF16), jax.ShapeDtypeStruct((nb, seq, PAIR_W), BF16),
                   jax.ShapeDtypeStruct((nb, seq, HEAD_W), BF16)],
        scratch_shapes=[pltpu.VMEM((nb, N_HEADS, LANE, LANE), F32), pltpu.VMEM((nb, N_PAIRS, LANE, LANE), F32),
                        pltpu.VMEM((nb, N_HEADS, LANE, LANE), F32)],
        compiler_params=_cparams(("arbitrary",)),
        name="token_mixers",
    )(sv(gq), sv(gk), sv(gv), sv(gg), sv(ga), wa2p, ba2p, gnw, sv(hq), sv(hf), sv(hi), sv(hg), lbp, hnw,
      sv(rq), sv(rk), sv(rv), sv(rg), cos, sin, *tables)
    t = nb * seq
    return o_gla.reshape(t, HEAD_W), o_hg.reshape(t, PAIR_W), o_ret.reshape(t, HEAD_W)


def _outproj_router_kernel(x_ref, oa_ref, ob_ref, oc_ref, wo_ref, g1_ref, nw_ref, sc_ref, sh_ref,
                           wr_ref, br_ref, tri_ref, upper_ref,
                           xo_ref, h_ref, tw_ref, lo_ref, tab_ref, cnt_ref, *, tm):
    @pl.when(pl.program_id(0) == 0)
    def _():
        cnt_ref[...] = jnp.zeros_like(cnt_ref)

    wa, wb = oa_ref.shape[1], ob_ref.shape[1]
    mix = (_dot(oa_ref[...], wo_ref[0:wa, :]) + _dot(ob_ref[...], wo_ref[wa:wa + wb, :])
           + _dot(oc_ref[...], wo_ref[wa + wb:, :]))
    x = x_ref[...] + g1_ref[0] * mix
    xo_ref[...] = x
    y = x * lax.rsqrt(jnp.mean(x * x, axis=-1, keepdims=True) + EPS) * nw_ref[...]
    h = y * (1.0 + sc_ref[0]) + sh_ref[0]
    h_ref[...] = h.astype(h_ref.dtype)

    h_hi, h_lo = _split2(h)
    w_hi, w_lo = _split2(wr_ref[...])
    both = _dot(h_hi, jnp.concatenate([w_hi, w_lo], axis=1))
    lg = both[:, :LANE] + both[:, LANE:] + _dot(h_lo, w_hi) + br_ref[...]
    lane = lax.broadcasted_iota(jnp.int32, (tm, LANE), 1)
    sel_e, sel_v = [], []
    for _ in range(TOP_K):
        m = jnp.max(lg, axis=-1, keepdims=True)
        idx = jnp.min(jnp.where(lg == m, lane, LANE), axis=-1, keepdims=True)
        sel_e.append(idx)
        sel_v.append(m)
        lg = jnp.where(lane == idx, -jnp.inf, lg)
    ex = [jnp.exp(v - sel_v[0]) for v in sel_v]
    den = ex[0] + ex[1] + ex[2] + ex[3]
    hot = [(lane == idx) for idx in sel_e]
    onehot = jnp.zeros((tm, LANE), F32)
    for hk in hot:
        onehot = onehot + jnp.where(hk, 1.0, 0.0)
    in_tile = _dot(tri_ref[...], onehot.astype(BF16))
    earlier = cnt_ref[...]
    tile_cnt = jnp.floor((jnp.sum(onehot, axis=0, keepdims=True) + (RUN_ALIGN - 1.0)) * (1.0 / RUN_ALIGN)) * RUN_ALIGN
    tile_start = _dot_f32(jnp.broadcast_to(tile_cnt, (8, LANE)), upper_ref[...])[0:1]
    local = tile_start + in_tile
    tw = jnp.zeros((tm, LANE), F32)
    lo = jnp.zeros((tm, LANE), jnp.int32)
    for kk in range(TOP_K):
        lrow = jnp.sum(jnp.where(hot[kk], local, 0.0), axis=-1, keepdims=True).astype(jnp.int32)
        tw = jnp.where(lane == kk, ex[kk] / den, tw)
        lo = jnp.where(lane == kk, lrow, lo)
    tw_ref[...] = tw
    lo_ref[...] = lo
    tab_ref[0, 0:1, :] = tile_start.astype(jnp.int32)
    tab_ref[0, 1:2, :] = tile_cnt.astype(jnp.int32)
    tab_ref[0, 2:3, :] = earlier.astype(jnp.int32)
    tab_ref[0, 3:4, :] = jnp.broadcast_to(jnp.sum(tile_cnt, axis=-1, keepdims=True), (1, LANE)).astype(jnp.int32)
    tab_ref[0, 4:8, :] = jnp.zeros((4, LANE), jnp.int32)
    cnt_ref[...] = earlier + tile_cnt


def _outproj_router_call(x, oa, ob, oc, wo, layer, g1, nw, sc, sh, wr, br, seq):
    t, d = x.shape
    tm = min(ROUTE_TILE, seq)
    per_b = seq // tm
    tri = np.tril(np.ones((tm, tm), np.float32), -1)
    upper = np.triu(np.ones((LANE, LANE), np.float32), 1)
    row = lambda w: pl.BlockSpec((tm, w), lambda i: (i, 0))
    const = lambda shape: pl.BlockSpec(shape, lambda i: (0,) * len(shape))
    perb = pl.BlockSpec((1, 1, d), lambda i: (i // per_b, 0, 0))
    n_tiles = t // tm
    return pl.pallas_call(
        functools.partial(_outproj_router_kernel, tm=tm),
        grid=(n_tiles,),
        in_specs=[row(d), row(oa.shape[1]), row(ob.shape[1]), row(oc.shape[1]),
                  pl.BlockSpec((None,) + wo.shape[1:], lambda i: (layer, 0, 0)), perb,
                  const((1, d)), perb, perb, const(wr.shape), const(br.shape), const(tri.shape),
                  const(upper.shape)],
        out_specs=[row(d), row(d), row(LANE), row(LANE),
                   pl.BlockSpec((1, 8, LANE), lambda i: (i, 0, 0)), const((1, LANE))],
        out_shape=[jax.ShapeDtypeStruct((t, d), F32), jax.ShapeDtypeStruct((t, d), BF16),
                   jax.ShapeDtypeStruct((t, LANE), F32), jax.ShapeDtypeStruct((t, LANE), jnp.int32),
                   jax.ShapeDtypeStruct((n_tiles, 8, LANE), jnp.int32), jax.ShapeDtypeStruct((1, LANE), F32)],
        compiler_params=_cparams(("arbitrary",)),
        name="outproj_norm2_router",
    )(x, oa, ob, oc, wo, g1, nw.reshape(1, d), sc, sh, wr, br, jnp.asarray(tri, BF16), jnp.asarray(upper))


def _run_copies(tab_ref, at_ref, tile, tm, make_copy, act):
    sizes = [s for s in (1 << p for p in range(tm.bit_length() - 1, -1, -1)) if s >= RUN_ALIGN]

    def per_expert(e, carry):
        first = tab_ref[tile, 0, e]
        length = tab_ref[tile, 1, e]
        at = at_ref[tile, e]
        done = jnp.int32(0)
        for size in sizes:
            part = length & size

            @pl.when(part != 0)
            def _():
                act(make_copy(pl.multiple_of(first + done, RUN_ALIGN), pl.multiple_of(at + done, RUN_ALIGN), size))
            done = done + part
        return carry

    lax.fori_loop(0, N_EXPERTS, per_expert, 0)


def _local_rows(tm):
    return TOP_K * tm + N_EXPERTS * RUN_ALIGN


def _dispatch_kernel(tab_ref, at_ref, h_ref, lo_ref, buf_in_ref, buf_ref, st_ref, sem, *, tm, n_tiles):
    del buf_in_ref
    i = pl.program_id(0)
    slot = i % 2

    def copies(tile, s, act):
        _run_copies(tab_ref, at_ref, tile, tm,
                    lambda row, at, size: pltpu.make_async_copy(st_ref.at[s, pl.ds(row, size)],
                                                                buf_ref.at[pl.ds(at, size)], sem.at[s]), act)

    def wait_tile(tile, s):
        total = pl.multiple_of(tab_ref[tile, 3, 0], RUN_ALIGN)
        pltpu.make_async_copy(st_ref.at[s, pl.ds(0, total)], buf_ref.at[pl.ds(0, total)], sem.at[s]).wait()

    @pl.when(i >= 2)
    def _():
        wait_tile(i - 2, slot)

    col = lax.broadcasted_iota(jnp.int32, (tm, st_ref.shape[1]), 1)
    lo = lo_ref[...]
    place = jnp.zeros(col.shape, F32)
    for kk in range(TOP_K):
        place = jnp.where(col == lo[:, kk:kk + 1], 1.0, place)
    st_ref[slot] = _dot_tn(place.astype(BF16), h_ref[...].astype(BF16))
    copies(i, slot, lambda cp: cp.start())

    @pl.when(i == n_tiles - 1)
    def _():
        if n_tiles >= 2:
            wait_tile(i - 1, 1 - slot)
        wait_tile(i, slot)


def _dispatch_call(h, tab, run_at, lo, buf, seq):
    t, d = h.shape
    n_slots = buf.shape[0]
    tm = min(ROUTE_TILE, seq)
    n_tiles = t // tm
    grid_spec = pltpu.PrefetchScalarGridSpec(
        num_scalar_prefetch=2,
        grid=(n_tiles,),
        in_specs=[pl.BlockSpec((tm, d), lambda i, tb, sr: (i, 0)),
                  pl.BlockSpec((tm, LANE), lambda i, tb, sr: (i, 0)),
                  pl.BlockSpec(memory_space=pl.ANY)],
        out_specs=pl.BlockSpec(memory_space=pl.ANY),
        scratch_shapes=[pltpu.VMEM((2, _local_rows(tm), d), F32), pltpu.SemaphoreType.DMA((2,))],
    )
    return pl.pallas_call(
        functools.partial(_dispatch_kernel, tm=tm, n_tiles=n_tiles),
        grid_spec=grid_spec,
        out_shape=jax.ShapeDtypeStruct((n_slots, d), F32),
        input_output_aliases={4: 0},
        compiler_params=_cparams(("arbitrary",)),
        name="moe_dispatch",
    )(tab, run_at, h, lo, buf)


def _expert_kernel(be_ref, nx_ref, used_ref, nu_ref, x_ref, wgu_ref, bgu_ref, wdn_ref, bdn_ref, o_ref,
                   wgu_f32, wdn_f32, wgu_bf, wdn_bf, sem, *, d_ff, layer):
    i = pl.program_id(0)
    live = i < nu_ref[0]
    e = be_ref[i]
    e_next = nx_ref[i]

    def weight_copies(expert):
        return (pltpu.make_async_copy(wgu_ref.at[layer, expert], wgu_f32, sem.at[0]),
                pltpu.make_async_copy(wdn_ref.at[layer, expert], wdn_f32, sem.at[1]))

    @pl.when(i == 0)
    def _():
        for cp in weight_copies(e):
            cp.start()

    @pl.when(jnp.logical_and(live, jnp.logical_or(i == 0, e != be_ref[jnp.maximum(i - 1, 0)])))
    def _():
        for cp in weight_copies(e):
            cp.wait()
        wgu_bf[...] = wgu_f32[...].astype(BF16)
        wdn_bf[...] = wdn_f32[...].astype(BF16)

        @pl.when(e_next != e)
        def _():
            for cp in weight_copies(e_next):
                cp.start()

    def ffn(rows):
        gu = _dot(x_ref[0:rows, :].astype(BF16), wgu_bf[...]) + bgu_ref[0, 0]
        gate = jnp.minimum(gu[:, :d_ff], SWIGLU_LIMIT)
        up = jnp.clip(gu[:, d_ff:], -SWIGLU_LIMIT, SWIGLU_LIMIT)
        act = (up + 1.0) * gate * _sigmoid(SWIGLU_ALPHA * gate)
        o_ref[0:rows, :] = _dot(act.astype(BF16), wdn_bf[...]) + bdn_ref[0, 0]
        if rows < MOE_BLOCK:
            o_ref[rows:, :] = jnp.zeros((MOE_BLOCK - rows, o_ref.shape[1]), o_ref.dtype)

    used = used_ref[i]
    prefixes = [MOE_BLOCK // 4, MOE_BLOCK // 2, MOE_BLOCK]
    for lo_rows, rows in zip([0] + prefixes[:-1], prefixes):
        pl.when(jnp.logical_and(live, jnp.logical_and(used > lo_rows, used <= rows)))(
            functools.partial(ffn, rows))

    @pl.when(jnp.logical_not(live))
    def _():
        o_ref[...] = jnp.zeros_like(o_ref)


def _expert_call(buf, block_e, next_e, block_used, n_used, wgu, bgu, wdn, bdn, layer):
    n_slots, d = buf.shape
    n_blocks = n_slots // MOE_BLOCK
    depth, ne, _, f2 = wgu.shape
    d_ff = f2 // 2
    blk = lambda i, be, nx, us, nu: (jnp.minimum(i, nu[0] - 1), 0)
    exp4 = lambda i, be, nx, us, nu: (layer, be[jnp.minimum(i, nu[0] - 1)], 0, 0)
    grid_spec = pltpu.PrefetchScalarGridSpec(
        num_scalar_prefetch=4,
        grid=(n_blocks,),
        in_specs=[pl.BlockSpec((MOE_BLOCK, d), blk),
                  pl.BlockSpec(memory_space=pl.ANY),
                  pl.BlockSpec((1, 1, 1, f2), exp4),
                  pl.BlockSpec(memory_space=pl.ANY),
                  pl.BlockSpec((1, 1, 1, d), exp4)],
        out_specs=pl.BlockSpec((MOE_BLOCK, d), lambda i, be, nx, us, nu: (i, 0)),
        scratch_shapes=[pltpu.VMEM((d, f2), F32), pltpu.VMEM((d_ff, d), F32),
                        pltpu.VMEM((d, f2), BF16), pltpu.VMEM((d_ff, d), BF16),
                        pltpu.SemaphoreType.DMA((2,))],
    )
    return pl.pallas_call(
        functools.partial(_expert_kernel, d_ff=d_ff, layer=layer),
        grid_spec=grid_spec,
        out_shape=jax.ShapeDtypeStruct((n_slots, d), F32),
        compiler_params=_cparams(("arbitrary",)),
        name="moe_experts",
    )(block_e, next_e, block_used, n_used, buf, wgu, bgu.reshape(depth, ne, 1, f2), wdn, bdn.reshape(depth, ne, 1, d))


def _combine_kernel(tab_ref, at_ref, yb_ref, lo_ref, w_ref, x_ref, g2_ref, fw_ref, o_ref, rs_ref, sem, *,
                    tm, final_norm):
    i = pl.program_id(0)
    n = pl.num_programs(0)
    rows = rs_ref.shape[1]

    def run_copies(tile, s, act):
        _run_copies(tab_ref, at_ref, tile, tm,
                    lambda row, at, size: pltpu.make_async_copy(yb_ref.at[pl.ds(at, size)],
                                                                rs_ref.at[s, pl.ds(row, size)], sem.at[s]), act)

    @pl.when(i == 0)
    def _():
        rs_ref[...] = jnp.zeros_like(rs_ref)
        run_copies(0, 0, lambda cp: cp.start())

    @pl.when(i + 1 < n)
    def _():
        run_copies(i + 1, (i + 1) % 2, lambda cp: cp.start())

    slot = i % 2
    total = pl.multiple_of(tab_ref[i, 3, 0], RUN_ALIGN)
    pltpu.make_async_copy(yb_ref.at[pl.ds(0, total)], rs_ref.at[slot, pl.ds(0, total)], sem.at[slot]).wait()
    r = rs_ref[slot].astype(BF16)
    col = lax.broadcasted_iota(jnp.int32, (tm, rows), 1)
    w = w_ref[...]
    lo = lo_ref[...]
    pw = jnp.zeros((tm, rows), F32)
    for kk in range(TOP_K):
        pw = jnp.where(col == lo[:, kk:kk + 1], w[:, kk:kk + 1], pw)
    y = _dot(pw.astype(BF16), r)
    out = x_ref[...] + g2_ref[0] * y
    if final_norm:
        out = out * lax.rsqrt(jnp.mean(out * out, axis=-1, keepdims=True) + EPS) * fw_ref[...]
    o_ref[...] = out


def _combine_call(yb, tab, src, lo, tw, x, g2, final_w, final_norm, seq):
    t, d = x.shape
    tm = min(ROUTE_TILE, seq)
    per_b = seq // tm
    grid_spec = pltpu.PrefetchScalarGridSpec(
        num_scalar_prefetch=2,
        grid=(t // tm,),
        in_specs=[pl.BlockSpec(memory_space=pl.ANY),
                  pl.BlockSpec((tm, LANE), lambda i, tb, sr: (i, 0)),
                  pl.BlockSpec((tm, LANE), lambda i, tb, sr: (i, 0)),
                  pl.BlockSpec((tm, d), lambda i, tb, sr: (i, 0)),
                  pl.BlockSpec((1, 1, d), lambda i, tb, sr: (i // per_b, 0, 0)),
                  pl.BlockSpec((1, d), lambda i, tb, sr: (0, 0))],
        out_specs=pl.BlockSpec((tm, d), lambda i, tb, sr: (i, 0)),
        scratch_shapes=[pltpu.VMEM((2, _local_rows(tm), d), F32), pltpu.SemaphoreType.DMA((2,))],
    )
    return pl.pallas_call(
        functools.partial(_combine_kernel, tm=tm, final_norm=final_norm),
        grid_spec=grid_spec,
        out_shape=jax.ShapeDtypeStruct((t, d), F32),
        compiler_params=_cparams(("arbitrary",)),
        name="moe_combine",
    )(tab, src, yb, lo, tw, x, g2, final_w.reshape(1, d))


def _split_heads(w, hd):
    return w.reshape(w.shape[:-1] + (N_HEADS, hd))


def _pad_last(w, lo, hi):
    return jnp.pad(w, [(0, 0)] * (w.ndim - 1) + [(lo, hi)])


def _tile_cols(w, hd):
    return _pad_last(_split_heads(w, hd), 0, LANE - hd).reshape(w.shape[:-1] + (HEAD_W,))


def _pair_cols(w, hd):
    return _pad_last(_split_heads(w, hd), 0, HALF - hd).reshape(w.shape[:-1] + (PAIR_W,))


def _tile_rows(w, hd):
    lead, cols = w.shape[:-2], w.shape[-1]
    w = w.reshape(lead + (N_HEADS, hd, cols))
    w = jnp.pad(w, [(0, 0)] * (w.ndim - 2) + [(0, LANE - hd), (0, 0)])
    return w.reshape(lead + (HEAD_W, cols))


IN_PARTS = (("gq", GLA_DK, _pair_cols), ("gk", GLA_DK, _pair_cols), ("gv", GLA_DV, _tile_cols),
            ("gg", GLA_DV, _tile_cols), ("ga", None, None),
            ("hq", HG_DK, _pair_cols), ("hf", HG_DK, _pair_cols), ("hi", HG_DV, _pair_cols),
            ("hg", HG_DV, _pair_cols),
            ("rq", RET_DK, _pair_cols), ("rk", RET_DK, _pair_cols), ("rv", RET_DV, _tile_cols),
            ("rg", RET_DV, _tile_cols))
F32_PARTS = ("ga", "hf")


def _layout_w_in(w_in):
    cols, off = [], 0
    for _, hd, layout in IN_PARTS:
        if layout is None:
            part = _pad_last(w_in[..., off:off + GLA_RANK], 0, LANE - GLA_RANK)
            off += GLA_RANK
        else:
            part = layout(w_in[..., off:off + N_HEADS * hd], hd)
            off += N_HEADS * hd
        cols.append(part)
    return jnp.concatenate(cols, axis=-1).astype(BF16), tuple(int(p.shape[-1]) for p in cols)


def kernel(x, c, positions, w_mod, b_mod, norm1_w, w_in, gla_wa2, gla_ba2, hg_lb, gla_norm_w, hg_norm_w,
           w_out, norm2_w, w_r, b_r, w_gu, b_gu, w_dn, b_dn, final_norm_w):
    nb, seq, d = x.shape
    t = nb * seq
    depth = w_mod.shape[0]
    dtypes = tuple(F32 if name in F32_PARTS else BF16 for name, _, _ in IN_PARTS)

    mod = _mod_call(c, w_mod, b_mod)
    cos, sin = _rope_call(positions)
    lb_all = _lb_call(hg_lb)

    n_assign = t * TOP_K
    n_run_pad = N_EXPERTS * (RUN_ALIGN - 1) * (t // min(ROUTE_TILE, seq))
    n_blocks = (n_assign + n_run_pad + MOE_BLOCK - 1) // MOE_BLOCK + N_EXPERTS
    n_slots = n_blocks * MOE_BLOCK

    w_cat, widths = _layout_w_in(w_in)
    n_gla, n_hg = N_HEADS * GLA_DV, N_HEADS * HG_DV
    wo_p = jnp.concatenate([_tile_rows(w_out[:, :n_gla], GLA_DV), w_out[:, n_gla:n_gla + n_hg],
                            _tile_rows(w_out[:, n_gla + n_hg:], RET_DV)], axis=1).astype(BF16)
    wa2p = jnp.pad(_pair_cols(gla_wa2, GLA_DK), ((0, 0), (0, LANE - GLA_RANK), (0, 0)))
    ba2p = _pair_cols(gla_ba2, GLA_DK)
    gnw = _tile_cols(gla_norm_w, GLA_DV)
    wr_p = _pad_last(w_r, 0, LANE - N_EXPERTS)
    br_p = jnp.pad(b_r, ((0, 0), (0, LANE - N_EXPERTS)), constant_values=NEG_BIG)

    buf = jnp.zeros((n_slots, d), F32)
    xf = x.reshape(t, d)
    for layer in range(depth):
        sh1, sc1, g1, sh2, sc2, g2 = [m.reshape(nb, 1, d) for m in jnp.split(mod[layer], 6, axis=-1)]

        parts = _inproj_call(xf, norm1_w[layer], sc1, sh1, w_cat, layer, widths, dtypes, seq)
        o_gla, o_hg, o_ret = _mixer_call(parts, wa2p[layer], ba2p[layer:layer + 1], gnw[layer:layer + 1],
                                         lb_all[layer:layer + 1], hg_norm_w[layer:layer + 1], cos, sin, nb, seq)
        xf, h2, tw, lo, tab, cnt = _outproj_router_call(
            xf, o_gla, o_hg, o_ret, wo_p, layer, g1, norm2_w[layer], sc2, sh2, wr_p[layer],
            br_p[layer:layer + 1], seq)

        counts = cnt[0, :N_EXPERTS].astype(jnp.int32)
        padded = (counts + MOE_BLOCK - 1) // MOE_BLOCK * MOE_BLOCK
        pends = jnp.cumsum(padded)
        pstarts = pends - padded
        block_start = jnp.arange(n_blocks, dtype=jnp.int32) * MOE_BLOCK
        block_e = jnp.minimum(jnp.sum((pends[None, :] <= block_start[:, None]).astype(jnp.int32), axis=1),
                              N_EXPERTS - 1)
        n_used = (pends[-1:] // MOE_BLOCK).astype(jnp.int32)
        run_at = tab[:, 2, :] + jnp.pad(pstarts, (0, LANE - N_EXPERTS))[None, :]

        ids = jnp.arange(N_EXPERTS, dtype=jnp.int32)
        later = jnp.where((ids[None, :] > ids[:, None]) & (padded[None, :] > 0), ids[None, :], N_EXPERTS)
        follower = jnp.min(later, axis=1)
        follower = jnp.where(follower == N_EXPERTS, ids, follower)
        of_block = block_e[:, None] == ids[None, :]
        next_e = jnp.sum(jnp.where(of_block, follower[None, :], 0), axis=1)
        run_end = jnp.sum(jnp.where(of_block, (pstarts + counts)[None, :], 0), axis=1)
        block_used = jnp.clip(run_end - block_start, 0, MOE_BLOCK).astype(jnp.int32)

        buf = _dispatch_call(h2, tab, run_at, lo, buf, seq)
        yb = _expert_call(buf, block_e, next_e, block_used, n_used, w_gu, b_gu, w_dn, b_dn, layer)
        xf = _combine_call(yb, tab, run_at, lo, tw, xf, g2, final_norm_w, layer == depth - 1, seq)

    return xf.reshape(nb, seq, d)
```
